```python
import jax, jax.numpy as jnp
from jax import lax
import numpy as np

D_MODEL = 1024
BATCH = 32
SEQ = 256
DEPTH = 2
DEC_BATCH = 2
DEC_SEQ = 4096
PAST_LEN = 512

GRID_W = 64
FT_GROUPS = 4
FT_GROUP_DIM = 128
FT_WIDTH = FT_GROUPS * FT_GROUP_DIM
HG_HEADS = 4
HG_DK = 128
HG_DV = 128
HG_WIDTH = HG_HEADS * HG_DK
HG_CHUNK = 16
NA_HEADS = 8
NA_HD = 64
NA_WIDTH = NA_HEADS * NA_HD
WIN_R = 8
WIN_C = 16
Q_COLS = 16
KEY_SPAN = Q_COLS + WIN_C
N_COL_BLK = GRID_W // Q_COLS
Q_BLOCK = 128
Q_ROWS = Q_BLOCK // GRID_W
N_BRANCH = 3
D_IN = FT_WIDTH + 5 * HG_WIDTH + 3 * NA_WIDTH + N_BRANCH * D_MODEL
D_FF = ((8 * D_MODEL // 3 + 127) // 128) * 128
N_EXPERTS = 8
TOP_K = 2
EPS = 1e-6

kernel_name = "fourier_hgrn2_natten_prefix_dit"


def rmsnorm(x, g):
    xf = x.astype(jnp.float32)
    y = xf * lax.rsqrt(jnp.mean(xf * xf, axis=-1, keepdims=True) + EPS)
    return (y * g.astype(jnp.float32)).astype(x.dtype)


def modulate(h, shift, scale):
    return h * (1.0 + scale[:, None]) + shift[:, None]


def split_proj(z):
    sizes = [FT_WIDTH, HG_WIDTH, HG_WIDTH, HG_WIDTH, HG_WIDTH, HG_WIDTH,
             NA_WIDTH, NA_WIDTH, NA_WIDTH, D_MODEL, D_MODEL, D_MODEL]
    return jnp.split(z, [int(s) for s in np.cumsum(sizes)[:-1]], axis=-1)


def fourier_mix(u):
    B, L, _ = u.shape
    ug = u.reshape(B, L, FT_GROUPS, FT_GROUP_DIM).astype(jnp.float32)
    y = jnp.fft.fft2(ug, axes=(1, 3), norm="ortho").real
    return y.reshape(B, L, FT_WIDTH).astype(u.dtype)


def hgrn2_gate(z, lb):
    zf = z.astype(jnp.float32)
    log_f = jnp.logaddexp(jnp.log(lb), jnp.log1p(-lb) + jax.nn.log_sigmoid(zf))
    return log_f, -jnp.expm1(log_f)


def hgrn2_scan(q, log_f, k, v, s0):
    B, L, H, _ = q.shape
    DV = v.shape[-1]
    n = L // HG_CHUNK

    def chunked(t):
        return t.astype(jnp.float32).reshape(B, n, HG_CHUNK, H, t.shape[-1])

    qc, gc, kc, vc = chunked(q), chunked(log_f), chunked(k), chunked(v)
    b = jnp.cumsum(gc, axis=2)
    b_last = b[:, :, -1]
    causal = jnp.tril(jnp.ones((HG_CHUNK, HG_CHUNK), bool))[:, :, None, None]
    diff = b[:, :, :, None] - b[:, :, None, :]
    decay = jnp.where(causal, jnp.exp(jnp.where(causal, diff, 0.0)), 0.0)
    attn = jnp.einsum('bnthk,bntshk,bnshk->bnhts', qc, decay, kc)
    o_intra = jnp.einsum('bnhts,bnshv->bnthv', attn, vc)
    kv = jnp.einsum('bnshk,bnshv->bnhkv', kc * jnp.exp(b_last[:, :, None] - b), vc)

    def step(s, inp):
        a, u = inp
        return a[..., None] * s + u, s

    s_final, s_starts = lax.scan(step, s0.astype(jnp.float32),
                                 (jnp.exp(b_last).swapaxes(0, 1), kv.swapaxes(0, 1)))
    s_starts = s_starts.swapaxes(0, 1)
    o_inter = jnp.einsum('bnthk,bnhkv->bnthv', qc * jnp.exp(b), s_starts)
    return (o_intra + o_inter).reshape(B, L, H, DV), s_final


def ctx_attention(q, k, v):
    B, Lc, H, d = q.shape
    nb = Lc // Q_BLOCK
    qb = q.reshape(B, nb, Q_BLOCK, H, d).swapaxes(0, 1)

    def blk(qi):
        s = jnp.einsum('bqhd,bkhd->bhqk', qi, k).astype(jnp.float32) * (NA_HD ** -0.5)
        p = jax.nn.softmax(s, axis=-1).astype(v.dtype)
        return jnp.einsum('bhqk,bkhd->bqhd', p, v)

    o = lax.map(blk, qb)
    return o.swapaxes(0, 1).reshape(B, Lc, H, d)


def na_latent(q, k, v, k_ctx, v_ctx, rpb):
    B, L, H, d = q.shape
    rows = L // GRID_W
    kr = min(WIN_R, rows)
    n_blk = rows // Q_ROWS
    row_ids = np.arange(rows)
    row_start = np.clip(row_ids - kr // 2, 0, rows - kr)
    key_rows = row_start[:, None] + np.arange(kr)[None]
    d_row = key_rows - row_ids[:, None] + (WIN_R - 1)
    q_cols = np.arange(GRID_W).reshape(N_COL_BLK, Q_COLS)
    col_start = np.clip(q_cols - WIN_C // 2, 0, GRID_W - WIN_C)
    blk_start = np.clip(np.arange(N_COL_BLK) * Q_COLS - WIN_C // 2, 0, GRID_W - KEY_SPAN)
    key_cols = blk_start[:, None] + np.arange(KEY_SPAN)[None]
    rel = key_cols[:, None, :] - col_start[:, :, None]
    col_valid = (rel >= 0) & (rel < WIN_C)
    d_col = np.clip(key_cols[:, None, :] - q_cols[:, :, None] + WIN_C - 1, 0, 2 * WIN_C - 2)
    rpb_c = rpb[:, :, d_col]

    kg = k.reshape(B, rows, GRID_W, H, d)
    vg = v.reshape(B, rows, GRID_W, H, d)
    k_cols = jnp.take(kg, key_cols.reshape(-1), axis=2).reshape(B, rows, N_COL_BLK, KEY_SPAN, H, d)
    v_cols = jnp.take(vg, key_cols.reshape(-1), axis=2).reshape(B, rows, N_COL_BLK, KEY_SPAN, H, d)
    qb = q.reshape(B, n_blk, Q_ROWS, N_COL_BLK, Q_COLS, H, d).swapaxes(0, 1)
    xs = (qb, jnp.asarray(key_rows.reshape(n_blk, Q_ROWS, kr), jnp.int32),
          jnp.asarray(d_row.reshape(n_blk, Q_ROWS, kr), jnp.int32))
    scale = NA_HD ** -0.5
    n_loc = kr * KEY_SPAN

    def block(inp):
        qi, kri, dri = inp
        kk = jnp.take(k_cols, kri, axis=1)
        vv = jnp.take(v_cols, kri, axis=1)
        s_loc = jnp.einsum('brjihd,brajshd->bhrjias', qi, kk).astype(jnp.float32) * scale
        bias = jnp.take(rpb_c, dri, axis=1).transpose(0, 1, 3, 4, 2, 5)
        s_loc = jnp.where(col_valid[:, :, None, :], s_loc + bias[None].astype(jnp.float32), -jnp.inf)
        s_loc = s_loc.reshape(s_loc.shape[:5] + (n_loc,))
        s_ctx = jnp.einsum('brjihd,bchd->bhrjic', qi, k_ctx).astype(jnp.float32) * scale
        p = jax.nn.softmax(jnp.concatenate([s_loc, s_ctx], axis=-1), axis=-1)
        p_loc = p[..., :n_loc].reshape(p.shape[:5] + (kr, KEY_SPAN)).astype(v.dtype)
        p_ctx = p[..., n_loc:].astype(v.dtype)
        return (jnp.einsum('bhrjias,brajshd->brjihd', p_loc, vv)
                + jnp.einsum('bhrjic,bchd->brjihd', p_ctx, v_ctx))

    o = lax.map(block, xs)
    return o.swapaxes(0, 1).reshape(B, L, H, d)


def mixer(h, w_in, lb, hg_norm, rpb, w_fo, w_ho, w_no, w_out, ctx):
    B, L, _ = h.shape
    z = h @ w_in
    u_ft, hq, hf_f, hf_b, hi, hg, nq, nk, nv, g_ft, g_hg, g_na = split_proj(z)
    y_ft = fourier_mix(u_ft) @ w_fo
    heads = lambda t: t.reshape(B, L, HG_HEADS, -1)
    rev = lambda t: jnp.flip(t, axis=1)
    lf_f, k_f = hgrn2_gate(hf_f, lb[0])
    lf_b, k_b = hgrn2_gate(hf_b, lb[1])
    q_h, v_h = heads(jax.nn.silu(hq)), heads(hi)
    lf_f, k_f, lf_b, k_b = heads(lf_f), heads(k_f), heads(lf_b), heads(k_b)
    if ctx is None:
        s_f0 = jnp.zeros((B, HG_HEADS, HG_DK, HG_DV), jnp.float32)
        s_b0 = s_f0
    else:
        s_f0, s_b0 = ctx[2], ctx[3]
    o_f, s_f = hgrn2_scan(q_h, lf_f, k_f, v_h, s_f0)
    o_b, s_b = hgrn2_scan(rev(q_h), rev(lf_b), rev(k_b), rev(v_h), s_b0)
    o_h = rmsnorm(o_f + rev(o_b), hg_norm.reshape(HG_HEADS, HG_DV)).reshape(B, L, HG_WIDTH)
    y_hg = (o_h.astype(h.dtype) * jax.nn.silu(hg)) @ w_ho
    na_heads = lambda t: t.reshape(B, L, NA_HEADS, NA_HD)
    qn, kn, vn = na_heads(nq), na_heads(nk), na_heads(nv)
    if ctx is None:
        o_n = ctx_attention(qn, kn, vn)
        new_ctx = (kn, vn, s_f, s_b)
    else:
        o_n = na_latent(qn, kn, vn, ctx[0], ctx[1], rpb)
        new_ctx = None
    y_na = o_n.reshape(B, L, NA_WIDTH) @ w_no
    merged = jax.nn.sigmoid(g_ft) * y_ft + jax.nn.sigmoid(g_hg) * y_hg + jax.nn.sigmoid(g_na) * y_na
    return merged @ w_out, new_ctx


def swiglu(h, w1, w3, w2):
    return (jax.nn.silu(h @ w1) * (h @ w3)) @ w2


def moe(h, w_router, w1, w3, w2):
    B, L, D = h.shape
    t = h.reshape(-1, D)
    logits = (t @ w_router).astype(jnp.float32)
    top_v, top_i = lax.top_k(logits, TOP_K)
    wts = jax.nn.softmax(top_v, axis=-1)
    out = jnp.zeros_like(t)
    for e in range(N_EXPERTS):
        g = jnp.sum(jnp.where(top_i == e, wts, 0.0), axis=-1)[:, None].astype(t.dtype)
        out = out + g * swiglu(t, w1[e], w3[e], w2[e])
    return out.reshape(B, L, D)


def trunk_layer(x, mod, l, lb, norm_mix, norm_ffn, w_in, hg_norm, na_rpb, w_fo, w_ho, w_no, w_out,
                w1_d, w3_d, w2_d, w_router, w1_e, w3_e, w2_e, ctx):
    sh1, sc1, g1, sh2, sc2, g2 = jnp.split(mod, 6, axis=-1)
    h = modulate(rmsnorm(x, norm_mix[l]), sh1, sc1)
    y, new_ctx = mixer(h, w_in[l], lb[l], hg_norm[l], na_rpb[l], w_fo[l], w_ho[l], w_no[l], w_out[l], ctx)
    x = x + g1[:, None] * y
    h = modulate(rmsnorm(x, norm_ffn[l]), sh2, sc2)
    i = l // 2
    if l % 2 == 0:
        f = swiglu(h, w1_d[i], w3_d[i], w2_d[i])
    else:
        f = moe(h, w_router[i], w1_e[i], w3_e[i], w2_e[i])
    return x + g2[:, None] * f, new_ctx


def setup_inputs(seed: int = 0) -> dict:
    key = jax.random.key(seed)
    ks = jax.random.split(key, 32)
    nrm = lambda k, shape, s: jax.random.normal(k, shape, jnp.float32) * s
    n_dense = (DEPTH + 1) // 2
    n_moe = DEPTH // 2
    return {
        "x_prompt": nrm(ks[0], (BATCH, SEQ, D_MODEL), 1.0),
        "x_sample": nrm(ks[1], (DEC_BATCH, DEC_SEQ, D_MODEL), 1.0),
        "cache_k": nrm(ks[2], (DEC_BATCH, DEPTH, PAST_LEN, NA_HEADS, NA_HD), 1.0),
        "cache_v": nrm(ks[3], (DEC_BATCH, DEPTH, PAST_LEN, NA_HEADS, NA_HD), 1.0),
        "state_hgrn": nrm(ks[4], (DEC_BATCH, DEPTH, 2, HG_HEADS, HG_DK, HG_DV), 0.5),
        "c": nrm(ks[5], (DEC_BATCH, D_MODEL), 1.0),
        "c_ctx": nrm(ks[6], (D_MODEL,), 1.0),
        "w_mod": nrm(ks[7], (DEPTH, D_MODEL, 6 * D_MODEL), 0.5 * D_MODEL ** -0.5),
        "b_mod": nrm(ks[8], (DEPTH, 6 * D_MODEL), 0.02),
        "norm_mix": 1.0 + nrm(ks[9], (DEPTH, D_MODEL), 0.02),
        "norm_ffn": 1.0 + nrm(ks[10], (DEPTH, D_MODEL), 0.02),
        "w_in": nrm(ks[11], (DEPTH, D_MODEL, D_IN), D_MODEL ** -0.5),
        "hg_lb": nrm(ks[12], (DEPTH, 2, HG_WIDTH), 1.0),
        "hg_norm": 1.0 + nrm(ks[13], (DEPTH, HG_HEADS * HG_DV), 0.02),
        "na_rpb": nrm(ks[14], (DEPTH, NA_HEADS, 2 * WIN_R - 1, 2 * WIN_C - 1), 0.1),
        "w_fo": nrm(ks[15], (DEPTH, FT_WIDTH, D_MODEL), FT_WIDTH ** -0.5),
        "w_ho": nrm(ks[16], (DEPTH, HG_WIDTH, D_MODEL), HG_WIDTH ** -0.5),
        "w_no": nrm(ks[17], (DEPTH, NA_WIDTH, D_MODEL), NA_WIDTH ** -0.5),
        "w_out": nrm(ks[18], (DEPTH, D_MODEL, D_MODEL), D_MODEL ** -0.5),
        "w1_d": nrm(ks[19], (n_dense, D_MODEL, D_FF), D_MODEL ** -0.5),
        "w3_d": nrm(ks[20], (n_dense, D_MODEL, D_FF), D_MODEL ** -0.5),
        "w2_d": nrm(ks[21], (n_dense, D_FF, D_MODEL), D_FF ** -0.5),
        "w_router": nrm(ks[22], (n_moe, D_MODEL, N_EXPERTS), D_MODEL ** -0.5),
        "w1_e": nrm(ks[23], (n_moe, N_EXPERTS, D_MODEL, D_FF), D_MODEL ** -0.5),
        "w3_e": nrm(ks[24], (n_moe, N_EXPERTS, D_MODEL, D_FF), D_MODEL ** -0.5),
        "w2_e": nrm(ks[25], (n_moe, N_EXPERTS, D_FF, D_MODEL), D_FF ** -0.5),
        "norm_final": 1.0 + nrm(ks[26], (D_MODEL,), 0.02),
    }


def reference(x_prompt, x_sample, cache_k, cache_v, state_hgrn, c, c_ctx, w_mod, b_mod, norm_mix,
              norm_ffn, w_in, hg_lb, hg_norm, na_rpb, w_fo, w_ho, w_no, w_out, w1_d, w3_d, w2_d,
              w_router, w1_e, w3_e, w2_e, norm_final):
    lbs = jnp.cumsum(jax.nn.softmax(hg_lb.astype(jnp.float32), axis=0), axis=0)
    lbs = lbs - lbs[0:1]
    weights = (norm_mix, norm_ffn, w_in, hg_norm, na_rpb, w_fo, w_ho, w_no, w_out,
               w1_d, w3_d, w2_d, w_router, w1_e, w3_e, w2_e)

    x = x_prompt
    ks, vs, sts = [], [], []
    for l in range(DEPTH):
        mod_ctx = (jax.nn.silu(c_ctx) @ w_mod[l] + b_mod[l])[None]
        x, (k_l, v_l, s_f, s_b) = trunk_layer(x, mod_ctx, l, lbs, *weights, None)
        ks.append(k_l)
        vs.append(v_l)
        sts.append(jnp.stack([s_f, s_b], axis=1))
    y_prompt = rmsnorm(x, norm_final)

    xs = x_sample
    for l in range(DEPTH):
        mod_lat = jax.nn.silu(c) @ w_mod[l] + b_mod[l]
        ctx = (cache_k[:, l], cache_v[:, l], state_hgrn[:, l, 0], state_hgrn[:, l, 1])
        xs, _ = trunk_layer(xs, mod_lat, l, lbs, *weights, ctx)
    y_sample = rmsnorm(xs, norm_final)

    new_cache_k = jnp.stack(ks, axis=1)
    new_cache_v = jnp.stack(vs, axis=1)
    new_state_hgrn = jnp.stack(sts, axis=1)
    return (y_prompt, y_sample, new_cache_k, new_cache_v, new_state_hgrn)
```

```python
import functools
import math

import numpy as np
import jax
import jax.numpy as jnp
from jax import lax
from jax.experimental import pallas as pl
from jax.experimental.pallas import tpu as pltpu

F32 = jnp.float32
BF16 = jnp.bfloat16

LANES = 128
SUBLANES = 8
VMEM_LIMIT = 56 * 1024 * 1024

FT_GROUPS = 4
FT_GD = 128
HG_HEADS = 4
HG_D = 128
NA_HEADS = 8
NA_HD = 64
GRID_W = 64
WIN_R = 8
WIN_C = 16
N_EXPERTS = 8
EPS = 1e-6
CH = 512
NEG = -1e30

ZB_SRC = (9, 10, 11, 12, 13, 14, 0, 1, 4, 5, 6, 7, 8)
ZB_UFT, ZB_HQ, ZB_HI, ZB_HG, ZB_NQ, ZB_NK, ZB_NV = 6, 7, 8, 9, 10, 11, 12
ZF_SRC = (2, 3)
SRC_NK, SRC_NV = 7, 8


def _cparams(sem):
    return pltpu.CompilerParams(dimension_semantics=sem, vmem_limit_bytes=VMEM_LIMIT)


def _const_spec(shape):
    nd = len(shape)
    return pl.BlockSpec(shape, lambda *_: (0,) * nd, pipeline_mode=pl.Buffered(1))


def _silu(x):
    return x * (1.0 / (1.0 + jnp.exp(-x)))


def _sigmoid(x):
    return 1.0 / (1.0 + jnp.exp(-x))


def _norm_mod(x, g, shift, scale):
    ms = jnp.mean(x * x, axis=-1, keepdims=True)
    return (x * lax.rsqrt(ms + EPS) * g) * (1.0 + scale) + shift


def _mod_kernel(c_ref, w_ref, b_ref, o_ref, *, nv, tn):
    w = w_ref[0]
    for v in range(nv):
        s = _silu(c_ref[v])
        s = jnp.concatenate([s] * (tn // LANES), axis=1)
        o_ref[0, v:v + 1, :] = jnp.sum(w * s, axis=0, keepdims=True) + b_ref[0]


def _mod_vectors(cvecs, w_mod, b_mod):
    nv, d = cvecs.shape
    depth, _, n6 = w_mod.shape
    tn = 512
    cb = jnp.broadcast_to(cvecs[:, :, None], (nv, d, LANES))
    out = pl.pallas_call(
        functools.partial(_mod_kernel, nv=nv, tn=tn),
        grid=(depth, n6 // tn),
        in_specs=[
            pl.BlockSpec((nv, d, LANES), lambda l, j: (0, 0, 0)),
            pl.BlockSpec((1, d, tn), lambda l, j: (l, 0, j)),
            pl.BlockSpec((1, 1, tn), lambda l, j: (l, 0, j)),
        ],
        out_specs=pl.BlockSpec((1, nv, tn), lambda l, j: (l, 0, j)),
        out_shape=jax.ShapeDtypeStruct((depth, nv, n6), F32),
        compiler_params=_cparams(("parallel", "parallel")),
        name="mod_vectors",
    )(cb, w_mod, b_mod.reshape(depth, 1, n6))
    return out.reshape(depth, nv, 6, d)


def _in_kernel(x_ref, g_ref, mod_ref, w_ref, *refs, with_cache):
    if with_cache:
        zb_ref, zf_ref, ck_ref, cv_ref = refs[-4:]
    else:
        zb_ref, zf_ref = refs[-2:]
    h = _norm_mod(x_ref[...], g_ref[...], mod_ref[0, 0:1, :], mod_ref[0, 1:2, :]).astype(BF16)
    for src in range(15):
        z = jnp.dot(h, w_ref[:, src * CH:(src + 1) * CH], preferred_element_type=F32)
        if src in ZB_SRC:
            j = ZB_SRC.index(src)
            zb_ref[:, j * CH:(j + 1) * CH] = z.astype(BF16)
        if src in ZF_SRC:
            j = ZF_SRC.index(src)
            zf_ref[:, j * CH:(j + 1) * CH] = z
        if with_cache and src == SRC_NK:
            ck_ref[...] = z
        if with_cache and src == SRC_NV:
            cv_ref[...] = z


def _in_proj(x, g, mod, w_bf, *, rows_per_seg, seq, layer, depth, caches):
    t, d = x.shape
    tm = 256
    with_cache = caches is not None
    spt = rows_per_seg // tm
    in_specs = [
        pl.BlockSpec((tm, d), lambda i: (i, 0)),
        _const_spec((1, d)),
        pl.BlockSpec((1, 6, d), lambda i: (i // spt, 0, 0)),
        _const_spec((d, 15 * CH)),
    ]
    args = [x, g.reshape(1, d), mod, w_bf]
    out_specs = [
        pl.BlockSpec((tm, len(ZB_SRC) * CH), lambda i: (i, 0)),
        pl.BlockSpec((tm, len(ZF_SRC) * CH), lambda i: (i, 0)),
    ]
    out_shape = [
        jax.ShapeDtypeStruct((t, len(ZB_SRC) * CH), BF16),
        jax.ShapeDtypeStruct((t, len(ZF_SRC) * CH), F32),
    ]
    aliases = {}
    if with_cache:
        assert seq == tm, "one row tile per context sequence"
        cspec = pl.BlockSpec((tm, CH), lambda i: (i * depth + layer, 0))
        out_specs += [cspec, cspec]
        cshape = jax.ShapeDtypeStruct((t * depth, CH), F32)
        out_shape += [cshape, cshape]
        if caches[0] is not None:
            in_specs += [pl.BlockSpec(memory_space=pl.ANY)] * 2
            args += list(caches)
            aliases = {4: 2, 5: 3}
    return pl.pallas_call(
        functools.partial(_in_kernel, with_cache=with_cache),
        grid=(t // tm,),
        in_specs=in_specs,
        out_specs=out_specs,
        out_shape=out_shape,
        input_output_aliases=aliases,
        compiler_params=_cparams(("parallel",)),
        name="in_proj",
    )(*args)


def _dft_tables(n, scale):
    k = np.arange(n)
    ang = 2.0 * np.pi * ((k[:, None] * k[None, :]) % n) / n
    return np.cos(ang) * scale, np.sin(ang) * scale


def _ft_ctx_kernel(u_ref, cs_ref, lm_ref, y_ref):
    u = u_ref[...]
    cs = cs_ref[...].astype(BF16)
    zc, zs = [], []
    for g in range(FT_GROUPS):
        z = jnp.dot(u[:, g * FT_GD:(g + 1) * FT_GD], cs, preferred_element_type=F32)
        zc.append(z[:, :FT_GD])
        zs.append(z[:, FT_GD:])
    zz = jnp.concatenate([jnp.concatenate(zc, axis=1), jnp.concatenate(zs, axis=1)], axis=0)
    y = jnp.dot(lm_ref[...].astype(BF16), zz.astype(BF16), preferred_element_type=F32)
    for g in range(FT_GROUPS):
        y_ref[g] = y[:, g * FT_GD:(g + 1) * FT_GD]


def _fourier_ctx(zb, nb, seq):
    cc, sc = _dft_tables(FT_GD, FT_GD ** -0.5)
    cl, sl = _dft_tables(seq, seq ** -0.5)
    cs = jnp.asarray(np.concatenate([cc, sc], axis=1), F32)
    lm = jnp.asarray(np.concatenate([cl, -sl], axis=1), F32)
    return pl.pallas_call(
        _ft_ctx_kernel,
        grid=(nb,),
        in_specs=[
            pl.BlockSpec((seq, CH), lambda b: (b, ZB_UFT)),
            _const_spec(cs.shape),
            _const_spec(lm.shape),
        ],
        out_specs=pl.BlockSpec((FT_GROUPS, seq, FT_GD), lambda b: (0, b, 0)),
        out_shape=jax.ShapeDtypeStruct((FT_GROUPS, nb * seq, FT_GD), F32),
        compiler_params=_cparams(("parallel",)),
        name="fourier_ctx",
    )(zb, cs, lm)


def _ft_lat_kernel(u_ref, cs_ref, w1_ref, twc_ref, tws_ref, w2_ref, y_ref, zr_ref, zi_ref, *, n, lw):
    rows = n * n
    rc = min(rows, 512)
    ng = lw // FT_GD
    cs = cs_ref[...].astype(BF16)
    w1 = w1_ref[...].astype(BF16)
    w2 = w2_ref[...].astype(BF16)
    for r0 in range(0, rows, rc):
        for g in range(ng):
            z = jnp.dot(u_ref[r0:r0 + rc, g * FT_GD:(g + 1) * FT_GD], cs,
                        preferred_element_type=F32)
            zr_ref[g, r0:r0 + rc, :] = z[:, :FT_GD]
            zi_ref[g, r0:r0 + rc, :] = -z[:, FT_GD:]

    def gather(idx):
        xr = jnp.concatenate([zr_ref[g, idx, :] for g in range(ng)], axis=1)
        xi = jnp.concatenate([zi_ref[g, idx, :] for g in range(ng)], axis=1)
        return jnp.concatenate([xr, xi], axis=0).astype(BF16)

    def stage1(a, carry):
        idx = pl.ds(a, n, stride=n)
        y = jnp.dot(w1, gather(idx), preferred_element_type=F32)
        yr, yi = y[:n], y[n:]
        tc = jnp.concatenate([twc_ref[a]] * ng, axis=1)
        ts = jnp.concatenate([tws_ref[a]] * ng, axis=1)
        zr = yr * tc - yi * ts
        zi = yr * ts + yi * tc
        for g in range(ng):
            zr_ref[g, idx, :] = zr[:, g * FT_GD:(g + 1) * FT_GD]
            zi_ref[g, idx, :] = zi[:, g * FT_GD:(g + 1) * FT_GD]
        return carry

    lax.fori_loop(0, n, stage1, 0)

    def stage2(q, carry):
        r0 = pl.multiple_of(q * n, n)
        y = jnp.dot(w2, gather(pl.ds(r0, n)), preferred_element_type=F32)
        for g in range(ng):
            y_ref[g, pl.ds(q, n, stride=n), :] = y[:, g * FT_GD:(g + 1) * FT_GD]
        return carry

    lax.fori_loop(0, n, stage2, 0)


def _fourier_lat(zb, nb, seq):
    n = math.isqrt(seq)
    assert n * n == seq and n % SUBLANES == 0
    lw = 256
    cc, sc = _dft_tables(FT_GD, FT_GD ** -0.5)
    cs = jnp.asarray(np.concatenate([cc, sc], axis=1), F32)
    pr, ps = _dft_tables(n, 1.0 / n)
    w1 = jnp.asarray(np.block([[pr, ps], [-ps, pr]]), F32)
    k = np.arange(n)
    ang = 2.0 * np.pi * (k[:, None] * k[None, :]) / seq
    twc = jnp.asarray(np.broadcast_to(np.cos(ang)[:, :, None], (n, n, LANES)), F32)
    tws = jnp.asarray(np.broadcast_to(-np.sin(ang)[:, :, None], (n, n, LANES)), F32)
    wr, ws = _dft_tables(n, 1.0)
    w2 = jnp.asarray(np.concatenate([wr, ws], axis=1), F32)
    nh = CH // lw
    return pl.pallas_call(
        functools.partial(_ft_lat_kernel, n=n, lw=lw),
        grid=(nb, nh),
        in_specs=[
            pl.BlockSpec((seq, lw), lambda b, h: (b, ZB_UFT * nh + h)),
            _const_spec(cs.shape),
            _const_spec(w1.shape),
            _const_spec(twc.shape),
            _const_spec(tws.shape),
            _const_spec(w2.shape),
        ],
        out_specs=pl.BlockSpec((lw // FT_GD, seq, FT_GD), lambda b, h: (h, b, 0)),
        out_shape=jax.ShapeDtypeStruct((FT_GROUPS, nb * seq, FT_GD), F32),
        scratch_shapes=[pltpu.VMEM((lw // FT_GD, seq, FT_GD), F32)] * 2,
        compiler_params=_cparams(("parallel", "parallel")),
        name="fourier_lat",
    )(zb, cs, w1, twc, tws, w2)


HG_BASE = 8


def _hgrn_tables(c, rev):
    t = np.arange(c)
    tri = (t[None, :] >= t[:, None]) if rev else (t[None, :] <= t[:, None])
    lv = np.full((c, c), -1, np.int32)
    n, i = HG_BASE, 0
    while n < c:
        same = (t[:, None] // (2 * n)) == (t[None, :] // (2 * n))
        hi_t, hi_s = (t[:, None] % (2 * n)) >= n, (t[None, :] % (2 * n)) >= n
        pair = same & (~hi_t & hi_s if rev else hi_t & ~hi_s)
        lv[pair] = i
        n, i = 2 * n, i + 1
    return tri.astype(np.float32), lv


def _split3(x):
    hi = x.astype(BF16)
    r = x - hi.astype(F32)
    mid = r.astype(BF16)
    lo = (r - mid.astype(F32)).astype(BF16)
    return hi, mid, lo


def _hgrn_kernel(hq_ref, hf_ref, hi_ref, lb_ref, tri_ref, lv_ref, *refs, c, nch, rev, has_s0):
    if has_s0:
        s0_ref, o_ref, st_ref, b_ref = refs
    else:
        o_ref, so_ref, st_ref, b_ref = refs
    j = pl.program_id(1)

    @pl.when(j == 0)
    def _():
        for h in range(HG_HEADS):
            st_ref[h] = s0_ref[0, h].T if has_s0 else jnp.zeros((HG_D, HG_D), F32)

    la = lb_ref[0:1, :]
    l1 = lb_ref[1:2, :]
    lv = lv_ref[...]
    nlev = int(round(math.log2(c // HG_BASE)))
    row8 = lax.broadcasted_iota(jnp.int32, (c // SUBLANES, SUBLANES, HG_D), 1)

    for ci in range(nch):
        cc = nch - 1 - ci if rev else ci
        rows = slice(cc * c, (cc + 1) * c)
        z = hf_ref[rows, :]
        ls = jnp.minimum(z, 0.0) - jnp.log1p(jnp.exp(-jnp.abs(z)))
        cc_ = l1 + ls
        logf = jnp.maximum(la, cc_) + jnp.log1p(jnp.exp(-jnp.abs(la - cc_)))
        f = jnp.exp(logf)
        kk = 1.0 - f
        b = sum(jnp.dot(tri_ref[...], p, preferred_element_type=F32) for p in _split3(logf))
        b_ref[...] = b
        bl_row = c - 1 if not rev else 0
        blast = b_ref[bl_row:bl_row + 1, :]
        q = _silu(hq_ref[rows, :].astype(F32))
        v = hi_ref[rows, :]
        vf = v.astype(F32)
        brefs = []
        n = HG_BASE
        for _ in range(nlev):
            parts = []
            for m in range(c // (2 * n)):
                r = 2 * n * m + (n if rev else n - 1)
                parts.append(jnp.broadcast_to(b_ref[r:r + 1, :], (2 * n, HG_HEADS * HG_D)))
            brefs.append(jnp.concatenate(parts, axis=0) if len(parts) > 1 else parts[0])
            n *= 2
        outs = []
        for h in range(HG_HEADS):
            hs = slice(h * HG_D, (h + 1) * HG_D)
            qh, kh, bh, fh, vh, vfh = q[:, hs], kk[:, hs], b[:, hs], f[:, hs], v[:, hs], vf[:, hs]
            st = st_ref[h]
            qe = (qh * jnp.exp(bh)).astype(BF16)
            o = lax.dot_general(qe, st.astype(BF16), (((1,), (1,)), ((), ())),
                                preferred_element_type=F32)
            a = jnp.zeros((c, c), F32)
            for i in range(nlev):
                e = jnp.exp(-jnp.abs(bh - brefs[i][:, hs]))
                sn = lax.dot_general((qh * e).astype(BF16), (kh * e).astype(BF16),
                                     (((1,), (1,)), ((), ())), preferred_element_type=F32)
                a = jnp.where(lv == i, sn, a)
            o = o + jnp.dot(a.astype(BF16), vh, preferred_element_type=F32)
            q3 = qh.reshape(c // SUBLANES, SUBLANES, HG_D)
            k3 = kh.reshape(c // SUBLANES, SUBLANES, HG_D)
            f3 = fh.reshape(c // SUBLANES, SUBLANES, HG_D)
            v3 = vfh.reshape(c // SUBLANES, SUBLANES, HG_D)
            od = jnp.sum(q3 * k3, axis=-1, keepdims=True) * v3
            dec = None
            for d in range(1, HG_BASE):
                sh = (SUBLANES - d) if rev else d
                shf = (SUBLANES - (d - 1)) % SUBLANES if rev else d - 1
                fs = f3 if d == 1 else pltpu.roll(f3, shf, axis=1)
                dec = fs if dec is None else dec * fs
                ok = (row8 + d < SUBLANES) if rev else (row8 >= d)
                w = jnp.where(ok, q3 * pltpu.roll(k3, sh, axis=1) * dec, 0.0)
                od = od + jnp.sum(w, axis=-1, keepdims=True) * pltpu.roll(v3, sh, axis=1)
            outs.append(o + od.reshape(c, HG_D))
            blh = blast[:, hs]
            ke = (kh * jnp.exp(blh - bh)).astype(BF16)
            kv = lax.dot_general(vh, ke, (((0,), (0,)), ((), ())), preferred_element_type=F32)
            st_ref[h] = st * jnp.exp(blh) + kv
        o_ref[rows, :] = jnp.concatenate(outs, axis=1)

    if not has_s0:
        @pl.when(j == pl.num_programs(1) - 1)
        def _():
            for h in range(HG_HEADS):
                so_ref[0, 0, 0, h] = st_ref[h].T


def _hgrn(zb, zf, lbp, s0, state_out, *, nb, seq, rev, layer, depth, c, lb_rows):
    t = zb.shape[0]
    nblk = seq // lb_rows
    nch = lb_rows // c
    tri, lv = _hgrn_tables(c, rev)
    d = int(rev)

    def rowmap(b, j):
        return b * nblk + (nblk - 1 - j if rev else j)

    in_specs = [
        pl.BlockSpec((lb_rows, CH), lambda b, j: (rowmap(b, j), ZB_HQ)),
        pl.BlockSpec((lb_rows, CH), lambda b, j: (rowmap(b, j), d)),
        pl.BlockSpec((lb_rows, CH), lambda b, j: (rowmap(b, j), ZB_HI)),
        _const_spec((2, CH)),
        _const_spec((c, c)),
        _const_spec((c, c)),
    ]
    args = [zb, zf, zb, lbp, jnp.asarray(tri, BF16), jnp.asarray(lv)]
    has_s0 = s0 is not None
    if has_s0:
        in_specs.append(pl.BlockSpec((1, HG_HEADS, HG_D, HG_D), lambda b, j: (b, 0, 0, 0)))
        args.append(s0)
    sshape = (nb, depth, 2, HG_HEADS, HG_D, HG_D)
    aliases = {}
    if state_out is not None:
        in_specs.append(pl.BlockSpec(memory_space=pl.ANY))
        args.append(state_out)
        aliases = {len(args) - 1: 1}
    kern = functools.partial(_hgrn_kernel, c=c, nch=nch, rev=rev, has_s0=has_s0)
    if state_out is not None:
        kern = functools.partial(_drop_ref, kern, len(args) - 1)
    out_specs = [pl.BlockSpec((lb_rows, CH), lambda b, j: (rowmap(b, j), 0))]
    out_shape = [jax.ShapeDtypeStruct((t, CH), F32)]
    if not has_s0:
        out_specs.append(pl.BlockSpec((1, 1, 1, HG_HEADS, HG_D, HG_D), lambda b, j: (b, layer, d, 0, 0, 0)))
        out_shape.append(jax.ShapeDtypeStruct(sshape, F32))
    outs = pl.pallas_call(
        kern,
        grid=(nb, nblk),
        in_specs=in_specs,
        out_specs=out_specs,
        out_shape=out_shape,
        scratch_shapes=[pltpu.VMEM((HG_HEADS, HG_D, HG_D), F32), pltpu.VMEM((c, CH), F32)],
        input_output_aliases=aliases,
        compiler_params=_cparams(("parallel", "arbitrary")),
        name="hgrn_bwd" if rev else "hgrn_fwd",
    )(*args)
    return (outs[0], None) if has_s0 else tuple(outs)


def _drop_ref(kern, pos, *refs):
    return kern(*refs[:pos], *refs[pos + 1:])


def _head_mask(a):
    lane = lax.broadcasted_iota(jnp.int32, (1, 2 * NA_HD), 1)
    return (lane // NA_HD) == a


def _qk(q, k):
    return lax.dot_general(q, k, (((1,), (1,)), ((), ())), preferred_element_type=F32)


def _attn_ctx_kernel(q_ref, k_ref, v_ref, o_ref):
    q = q_ref[...] * (NA_HD ** -0.5)
    k = k_ref[...]
    v = v_ref[...]
    outs = []
    for a in range(2):
        s = _qk(jnp.where(_head_mask(a), q, jnp.zeros_like(q)), k)
        p = jnp.exp(s - jnp.max(s, axis=-1, keepdims=True))
        l = jnp.sum(p, axis=-1, keepdims=True)
        outs.append(jnp.dot(p.astype(BF16), v, preferred_element_type=F32) / l)
    o_ref[...] = jnp.where(_head_mask(0), outs[0], outs[1]).astype(BF16)


def _attn_ctx(zb, nb, seq):
    hp = NA_HEADS // 2
    w = 2 * NA_HD
    return pl.pallas_call(
        _attn_ctx_kernel,
        grid=(nb, hp),
        in_specs=[
            pl.BlockSpec((seq, w), lambda b, p: (b, ZB_NQ * hp + p)),
            pl.BlockSpec((seq, w), lambda b, p: (b, ZB_NK * hp + p)),
            pl.BlockSpec((seq, w), lambda b, p: (b, ZB_NV * hp + p)),
        ],
        out_specs=pl.BlockSpec((seq, w), lambda b, p: (b, p)),
        out_shape=jax.ShapeDtypeStruct((nb * seq, CH), BF16),
        compiler_params=_cparams(("parallel", "parallel")),
        name="attn_ctx",
    )(zb, zb, zb)


NA_TR = 8


def _na_geometry(rows):
    kr = min(WIN_R, rows)
    span = min(NA_TR + WIN_R, rows)
    ntile = rows // NA_TR
    pats, var, starts = [], [], []
    for t in range(ntile):
        r0 = t * NA_TR
        start = int(np.clip(r0 - kr // 2, 0, rows - span))
        qr = r0 + np.arange(NA_TR)
        row_start = np.clip(qr - kr // 2, 0, rows - kr)
        krow = start + np.arange(span)
        rv = (krow[None, :] >= row_start[:, None]) & (krow[None, :] < row_start[:, None] + kr)
        d_row = np.where(rv, krow[None, :] - qr[:, None] + WIN_R - 1, -1)
        for vi, p in enumerate(pats):
            if np.array_equal(p, d_row):
                var.append(vi)
                break
        else:
            var.append(len(pats))
            pats.append(d_row)
        starts.append(start)
    return pats, np.asarray(var, np.int32), np.asarray(starts, np.int32), span


def _na_col_tables(rpb):
    qc = np.arange(GRID_W)
    col_start = np.clip(qc - WIN_C // 2, 0, GRID_W - WIN_C)
    cv = (qc[None, :] >= col_start[:, None]) & (qc[None, :] < col_start[:, None] + WIN_C)
    d_col = qc[None, :] - qc[:, None] + WIN_C - 1
    onehot = (d_col[None] == np.arange(2 * WIN_C - 1)[:, None, None]) & cv[None]
    t = jnp.einsum("hrd,dqk->hrqk", rpb, jnp.asarray(onehot, F32), precision=lax.Precision.HIGHEST)
    t = jnp.where(jnp.asarray(cv)[None, None], t, NEG)
    return jnp.concatenate([t, t], axis=-1)


def _na_kernel(var_ref, start_ref, q_ref, k_ref, v_ref, kc_ref, vc_ref, tab_ref, o_ref, bias_ref,
               *, nkey, pats):
    t = pl.program_id(2)
    var = var_ref[t]
    changed = jnp.logical_or(t == 0, var != var_ref[jnp.maximum(t - 1, 0)])
    left = lax.broadcasted_iota(jnp.int32, (GRID_W, 2 * GRID_W), 1) < GRID_W
    neg = jnp.full((GRID_W, 2 * GRID_W), NEG, F32)

    for vi, d_row in enumerate(pats):
        @pl.when(jnp.logical_and(changed, var == vi))
        def _(d_row=d_row):
            for a in range(2):
                for i in range(d_row.shape[0]):
                    for j in range(0, d_row.shape[1], 2):
                        r0, r1 = int(d_row[i, j]), int(d_row[i, j + 1])
                        b0 = tab_ref[a, r0] if r0 >= 0 else neg
                        b1 = tab_ref[a, r1] if r1 >= 0 else neg
                        blk = neg if (r0 < 0 and r1 < 0) else jnp.where(left, b0, b1)
                        bias_ref[a, i * GRID_W:(i + 1) * GRID_W, j * GRID_W:(j + 2) * GRID_W] = blk

    st = pl.multiple_of(start_ref[t] * GRID_W, GRID_W)
    q = q_ref[...] * (NA_HD ** -0.5)
    kw = k_ref[pl.ds(st, nkey), :]
    vw = v_ref[pl.ds(st, nkey), :]
    kc = kc_ref[0]
    vc = vc_ref[0]
    outs = []
    for a in range(2):
        qa = jnp.where(_head_mask(a), q, jnp.zeros_like(q))
        s1 = _qk(qa, kw) + bias_ref[a]
        s2 = _qk(qa, kc)
        m = jnp.maximum(jnp.max(s1, axis=-1, keepdims=True), jnp.max(s2, axis=-1, keepdims=True))
        p1 = jnp.exp(s1 - m)
        p2 = jnp.exp(s2 - m)
        l = jnp.sum(p1, axis=-1, keepdims=True) + jnp.sum(p2, axis=-1, keepdims=True)
        o = (jnp.dot(p1.astype(BF16), vw, preferred_element_type=F32)
             + jnp.dot(p2.astype(BF16), vc, preferred_element_type=F32))
        outs.append(o / l)
    o_ref[...] = jnp.where(_head_mask(0), outs[0], outs[1]).astype(BF16)


def _na_latent(zb, kctx, vctx, rpb, nb, seq):
    rows = seq // GRID_W
    pats, var, starts, span = _na_geometry(rows)
    nkey = span * GRID_W
    tq = NA_TR * GRID_W
    ntile = rows // NA_TR
    hp = NA_HEADS // 2
    w = 2 * NA_HD
    past = kctx.shape[1]
    assert span % 2 == 0
    tabs = _na_col_tables(rpb)
    nrow = tabs.shape[1]
    grid_spec = pltpu.PrefetchScalarGridSpec(
        num_scalar_prefetch=2,
        grid=(nb, hp, ntile),
        in_specs=[
            pl.BlockSpec((tq, w), lambda b, p, t, var, st: (b * ntile + t, ZB_NQ * hp + p)),
            pl.BlockSpec((seq, w), lambda b, p, t, var, st: (b, ZB_NK * hp + p)),
            pl.BlockSpec((seq, w), lambda b, p, t, var, st: (b, ZB_NV * hp + p)),
            pl.BlockSpec((1, past, w), lambda b, p, t, var, st: (b, 0, p)),
            pl.BlockSpec((1, past, w), lambda b, p, t, var, st: (b, 0, p)),
            pl.BlockSpec((2, nrow, GRID_W, 2 * GRID_W), lambda b, p, t, var, st: (p, 0, 0, 0)),
        ],
        out_specs=pl.BlockSpec((tq, w), lambda b, p, t, var, st: (b * ntile + t, p)),
        scratch_shapes=[pltpu.VMEM((2, tq, nkey), F32)],
    )
    return pl.pallas_call(
        functools.partial(_na_kernel, nkey=nkey, pats=pats),
        grid_spec=grid_spec,
        out_shape=jax.ShapeDtypeStruct((nb * seq, CH), BF16),
        compiler_params=_cparams(("parallel", "parallel", "arbitrary")),
        name="na_latent",
    )(jnp.asarray(var), jnp.asarray(starts), zb, zb, zb, kctx, vctx, tabs)


def _merge_kernel(x_ref, mod_ref, gate_ref, yft_ref, of_ref, ob_ref, hg_ref, on_ref, hn_ref,
                  wfo_ref, who_ref, wno_ref, wout_ref, o_ref):
    d = x_ref.shape[1]
    yft = jnp.concatenate([yft_ref[g] for g in range(FT_GROUPS)], axis=1)
    y_ft = jnp.dot(yft.astype(BF16), wfo_ref[...], preferred_element_type=F32)
    o = of_ref[...] + ob_ref[...]
    parts = []
    for h in range(HG_HEADS):
        oh = o[:, h * HG_D:(h + 1) * HG_D]
        ms = jnp.mean(oh * oh, axis=-1, keepdims=True)
        parts.append(oh * lax.rsqrt(ms + EPS))
    oh = jnp.concatenate(parts, axis=1) * hn_ref[...]
    oh = (oh * _silu(hg_ref[...].astype(F32))).astype(BF16)
    y_hg = jnp.dot(oh, who_ref[...], preferred_element_type=F32)
    y_na = jnp.dot(on_ref[...], wno_ref[...], preferred_element_type=F32)
    g = gate_ref[...].astype(F32)
    merged = (_sigmoid(g[:, :d]) * y_ft + _sigmoid(g[:, d:2 * d]) * y_hg
              + _sigmoid(g[:, 2 * d:]) * y_na)
    y = jnp.dot(merged.astype(BF16), wout_ref[...], preferred_element_type=F32)
    o_ref[...] = x_ref[...] + mod_ref[0, 2:3, :] * y


def _merge(x, mod, zb, yft, o_f, o_b, o_n, hg_norm, w_fo, w_ho, w_no, w_out, *, rows_per_seg):
    t, d = x.shape
    tm = 256
    spt = rows_per_seg // tm
    assert 3 * d == 6 * CH, "the three merge gates fill the first six bf16 splits"
    row = lambda i: (i, 0)
    return pl.pallas_call(
        _merge_kernel,
        grid=(t // tm,),
        in_specs=[
            pl.BlockSpec((tm, d), row),
            pl.BlockSpec((1, 6, d), lambda i: (i // spt, 0, 0)),
            pl.BlockSpec((tm, 3 * d), row),
            pl.BlockSpec((FT_GROUPS, tm, FT_GD), lambda i: (0, i, 0)),
            pl.BlockSpec((tm, CH), row),
            pl.BlockSpec((tm, CH), row),
            pl.BlockSpec((tm, CH), lambda i: (i, ZB_HG)),
            pl.BlockSpec((tm, CH), row),
            _const_spec((1, CH)),
            _const_spec((CH, d)),
            _const_spec((CH, d)),
            _const_spec((CH, d)),
            _const_spec((d, d)),
        ],
        out_specs=pl.BlockSpec((tm, d), row),
        out_shape=jax.ShapeDtypeStruct((t, d), F32),
        compiler_params=_cparams(("parallel",)),
        name="merge",
    )(x, mod, zb, yft, o_f, o_b, zb, o_n, hg_norm.reshape(1, CH), w_fo, w_ho, w_no, w_out)


FF_CHUNK = 256


def _swiglu_acc(h, w1_ref, w3_ref, w2_ref, lead=()):
    dff = w1_ref.shape[-1]
    acc = None
    for c0 in range(0, dff, FF_CHUNK):
        cs = slice(c0, c0 + FF_CHUNK)
        g = jnp.dot(h, w1_ref[lead + (slice(None), cs)], preferred_element_type=F32)
        u = jnp.dot(h, w3_ref[lead + (slice(None), cs)], preferred_element_type=F32)
        a = (_silu(g) * u).astype(BF16)
        y = jnp.dot(a, w2_ref[lead + (cs, slice(None))], preferred_element_type=F32)
        acc = y if acc is None else acc + y
    return acc


def _ffn_kernel(x_ref, g_ref, mod_ref, w1_ref, w3_ref, w2_ref, *refs, final):
    o_ref = refs[-1]
    x = x_ref[...]
    h = _norm_mod(x, g_ref[...], mod_ref[0, 3:4, :], mod_ref[0, 4:5, :]).astype(BF16)
    y = x + mod_ref[0, 5:6, :] * _swiglu_acc(h, w1_ref, w3_ref, w2_ref)
    if final:
        ms = jnp.mean(y * y, axis=-1, keepdims=True)
        y = y * lax.rsqrt(ms + EPS) * refs[0][...]
    o_ref[...] = y


def _ffn_dense(x, g, mod, w1, w3, w2, norm_final, *, rows_per_seg):
    t, d = x.shape
    dff = w1.shape[1]
    tm = 256
    spt = rows_per_seg // tm
    final = norm_final is not None
    in_specs = [
        pl.BlockSpec((tm, d), lambda i: (i, 0)),
        _const_spec((1, d)),
        pl.BlockSpec((1, 6, d), lambda i: (i // spt, 0, 0)),
        _const_spec((d, dff)),
        _const_spec((d, dff)),
        _const_spec((dff, d)),
    ]
    args = [x, g.reshape(1, d), mod, w1, w3, w2]
    if final:
        in_specs.append(_const_spec((1, d)))
        args.append(norm_final.reshape(1, d))
    return pl.pallas_call(
        functools.partial(_ffn_kernel, final=final),
        grid=(t // tm,),
        in_specs=in_specs,
        out_specs=pl.BlockSpec((tm, d), lambda i: (i, 0)),
        out_shape=jax.ShapeDtypeStruct((t, d), F32),
        compiler_params=_cparams(("parallel",)),
        name="ffn_dense",
    )(*args)


def _router_kernel(x_ref, g_ref, mod_ref, wr_ref, h_ref, r_ref):
    h = _norm_mod(x_ref[...], g_ref[...], mod_ref[0, 3:4, :], mod_ref[0, 4:5, :])
    h_ref[...] = h
    logits = jnp.dot(h, wr_ref[...], precision=lax.Precision.HIGHEST,
                     preferred_element_type=F32)
    lane = lax.broadcasted_iota(jnp.int32, logits.shape, 1)
    logits = jnp.where(lane < N_EXPERTS, logits, -jnp.inf)
    m1 = jnp.max(logits, axis=-1, keepdims=True)
    i1 = jnp.min(jnp.where(logits == m1, lane, LANES), axis=-1, keepdims=True)
    rest = jnp.where(lane == i1, -jnp.inf, logits)
    m2 = jnp.max(rest, axis=-1, keepdims=True)
    i2 = jnp.min(jnp.where(rest == m2, lane, LANES), axis=-1, keepdims=True)
    e2 = jnp.exp(m2 - m1)
    w1 = 1.0 / (1.0 + e2)
    w2 = e2 / (1.0 + e2)
    r = jnp.where(lane == 0, i1.astype(F32), jnp.where(lane == 1, i2.astype(F32),
                  jnp.where(lane == 2, w1, jnp.where(lane == 3, w2, 0.0))))
    r_ref[...] = r


def _router(x, g, mod, w_router, *, rows_per_seg):
    t, d = x.shape
    tm = 256
    spt = rows_per_seg // tm
    wr = jnp.zeros((d, LANES), F32).at[:, :N_EXPERTS].set(w_router)
    return pl.pallas_call(
        _router_kernel,
        grid=(t // tm,),
        in_specs=[
            pl.BlockSpec((tm, d), lambda i: (i, 0)),
            _const_spec((1, d)),
            pl.BlockSpec((1, 6, d), lambda i: (i // spt, 0, 0)),
            _const_spec((d, LANES)),
        ],
        out_specs=[pl.BlockSpec((tm, d), lambda i: (i, 0)), pl.BlockSpec((tm, LANES), lambda i: (i, 0))],
        out_shape=[jax.ShapeDtypeStruct((t, d), F32), jax.ShapeDtypeStruct((t, LANES), F32)],
        compiler_params=_cparams(("parallel",)),
        name="router",
    )(x, g.reshape(1, d), mod, wr)


MOE_TM = 512


def _moe_ffn_kernel(te_ref, nt_ref, src_ref, h_hbm, w1_ref, w3_ref, w2_ref, y_ref, hbuf, sem):
    i = pl.program_id(0)

    @pl.when(i < nt_ref[0])
    def _():
        base = i * MOE_TM

        def row_copy(r):
            return pltpu.make_async_copy(h_hbm.at[pl.ds(src_ref[base + r], 1)],
                                         hbuf.at[pl.ds(r, 1)], sem.at[0])

        def issue(r, carry):
            row_copy(r).start()
            return carry

        lax.fori_loop(0, MOE_TM, issue, 0)
        pltpu.make_async_copy(h_hbm.at[pl.ds(0, MOE_TM)], hbuf, sem.at[0]).wait()
        y_ref[...] = _swiglu_acc(hbuf[...].astype(BF16), w1_ref, w3_ref, w2_ref, lead=(0,))

    @pl.when(i >= nt_ref[0])
    def _():
        y_ref[...] = jnp.zeros_like(y_ref)


def _moe_ffn(h, tile_expert, n_tiles, src, w1, w3, w2, ntile_max):
    t, d = h.shape
    dff = w1.shape[-1]
    grid_spec = pltpu.PrefetchScalarGridSpec(
        num_scalar_prefetch=3,
        grid=(ntile_max,),
        in_specs=[
            pl.BlockSpec(memory_space=pl.ANY),
            pl.BlockSpec((1, d, dff), lambda i, te, nt, src: (te[i], 0, 0)),
            pl.BlockSpec((1, d, dff), lambda i, te, nt, src: (te[i], 0, 0)),
            pl.BlockSpec((1, dff, d), lambda i, te, nt, src: (te[i], 0, 0)),
        ],
        out_specs=pl.BlockSpec((MOE_TM, d), lambda i, te, nt, src: (i, 0)),
        scratch_shapes=[pltpu.VMEM((MOE_TM, d), F32), pltpu.SemaphoreType.DMA((1,))],
    )
    return pl.pallas_call(
        _moe_ffn_kernel,
        grid_spec=grid_spec,
        out_shape=jax.ShapeDtypeStruct((ntile_max * MOE_TM, d), F32),
        compiler_params=_cparams(("arbitrary",)),
        name="moe_ffn",
    )(tile_expert, n_tiles, src, h, w1, w3, w2)


def _combine_kernel(pos_ref, x_ref, mod_ref, r_ref, y_hbm, *refs, final, tm):
    o_ref, ybuf, sem = refs[-3:]
    i = pl.program_id(0)
    base = i * tm

    def row_copy(r, k):
        return pltpu.make_async_copy(y_hbm.at[pl.ds(pos_ref[2 * (base + r) + k], 1)],
                                     ybuf.at[k, pl.ds(r, 1)], sem.at[k])

    def issue(r, carry):
        row_copy(r, 0).start()
        row_copy(r, 1).start()
        return carry

    lax.fori_loop(0, tm, issue, 0)
    for k in range(2):
        pltpu.make_async_copy(y_hbm.at[pl.ds(0, tm)], ybuf.at[k], sem.at[k]).wait()
    r = r_ref[...]
    f = r[:, 2:3] * ybuf[0] + r[:, 3:4] * ybuf[1]
    y = x_ref[...] + mod_ref[0, 5:6, :] * f
    if final:
        ms = jnp.mean(y * y, axis=-1, keepdims=True)
        y = y * lax.rsqrt(ms + EPS) * refs[0][...]
    o_ref[...] = y


def _moe_combine(x, mod, route, pos, y_sorted, norm_final, *, rows_per_seg):
    t, d = x.shape
    tm = 256
    spt = rows_per_seg // tm
    final = norm_final is not None
    in_specs = [
        pl.BlockSpec((tm, d), lambda i, pos: (i, 0)),
        pl.BlockSpec((1, 6, d), lambda i, pos: (i // spt, 0, 0)),
        pl.BlockSpec((tm, LANES), lambda i, pos: (i, 0)),
        pl.BlockSpec(memory_space=pl.ANY),
    ]
    args = [x, mod, route, y_sorted]
    if final:
        in_specs.append(pl.BlockSpec((1, d), lambda i, pos: (0, 0)))
        args.append(norm_final.reshape(1, d))
    grid_spec = pltpu.PrefetchScalarGridSpec(
        num_scalar_prefetch=1,
        grid=(t // tm,),
        in_specs=in_specs,
        out_specs=pl.BlockSpec((tm, d), lambda i, pos: (i, 0)),
        scratch_shapes=[pltpu.VMEM((2, tm, d), F32), pltpu.SemaphoreType.DMA((2,))],
    )
    return pl.pallas_call(
        functools.partial(_combine_kernel, final=final, tm=tm),
        grid_spec=grid_spec,
        out_shape=jax.ShapeDtypeStruct((t, d), F32),
        compiler_params=_cparams(("arbitrary",)),
        name="moe_combine",
    )(pos.reshape(-1), *args)


def _moe(x, g, mod, w_router, w1, w3, w2, norm_final, *, rows_per_seg):
    t, d = x.shape
    h, route = _router(x, g, mod, w_router, rows_per_seg=rows_per_seg)
    ids = route[:, :2].astype(jnp.int32)
    onehot = (ids[:, :, None] == jnp.arange(N_EXPERTS)[None, None, :]).astype(jnp.int32)
    sel = onehot.sum(axis=1)
    counts = sel.sum(axis=0)
    rank = jnp.cumsum(sel, axis=0) - sel
    padded = ((counts + MOE_TM - 1) // MOE_TM) * MOE_TM
    pstart = jnp.cumsum(padded) - padded
    cstart = jnp.cumsum(counts) - counts
    pos_e = pstart[None, :] + rank
    pos = jnp.take_along_axis(pos_e, ids, axis=1).astype(jnp.int32)
    ntile_max = (2 * t) // MOE_TM + N_EXPERTS
    tok = jnp.arange(t, dtype=jnp.int32)
    key = jnp.sort((ids * t + tok[:, None]).reshape(-1))
    sorted_tok = key % t
    tile_start = jnp.arange(ntile_max, dtype=jnp.int32) * MOE_TM
    pend = pstart + padded
    tile_expert = jnp.minimum(jnp.sum(tile_start[:, None] >= pend[None, :], axis=1), N_EXPERTS - 1)
    n_tiles = (jnp.sum(padded) // MOE_TM).astype(jnp.int32).reshape(1)
    p = jnp.arange(ntile_max * MOE_TM, dtype=jnp.int32)
    e_p = jnp.repeat(tile_expert, MOE_TM)
    r_p = jnp.minimum(p - pstart[e_p], jnp.maximum(counts[e_p] - 1, 0))
    src = sorted_tok[jnp.clip(cstart[e_p] + r_p, 0, 2 * t - 1)].astype(jnp.int32)
    y_sorted = _moe_ffn(h, tile_expert.astype(jnp.int32), n_tiles, src, w1, w3, w2, ntile_max)
    return _moe_combine(x, mod, route, pos, y_sorted, norm_final, rows_per_seg=rows_per_seg)


def _layer(x, mod, l, depth, p, *, nb, seq, ctx, caches, state_out, norm_final):
    is_ctx = ctx is None
    seg = nb * seq if is_ctx else seq
    outs = _in_proj(x, p["norm_mix"][l], mod, p["w_in"][l], rows_per_seg=seg, seq=seq, layer=l,
                    depth=depth, caches=caches if is_ctx else None)
    zb, zf = outs[0], outs[1]
    new_caches = tuple(outs[2:]) if is_ctx else None
    if is_ctx:
        yft = _fourier_ctx(zb, nb, seq)
        o_n = _attn_ctx(zb, nb, seq)
    else:
        yft = _fourier_lat(zb, nb, seq)
        o_n = _na_latent(zb, ctx[0], ctx[1], p["na_rpb"][l], nb, seq)
    hg_c = 128
    lbr = 256 if is_ctx else 512
    lbr = min(lbr, seq)
    o_f, state_out = _hgrn(zb, zf, p["lbp"][l][0], None if is_ctx else ctx[2],
                           state_out if is_ctx else None, nb=nb, seq=seq, rev=False, layer=l,
                           depth=depth, c=hg_c, lb_rows=lbr)
    o_b, state_out = _hgrn(zb, zf, p["lbp"][l][1], None if is_ctx else ctx[3],
                           state_out if is_ctx else None, nb=nb, seq=seq, rev=True, layer=l,
                           depth=depth, c=hg_c, lb_rows=lbr)
    x = _merge(x, mod, zb, yft, o_f, o_b, o_n, p["hg_norm"][l], p["w_fo"][l], p["w_ho"][l],
               p["w_no"][l], p["w_out"][l], rows_per_seg=seg)
    i = l // 2
    if l % 2 == 0:
        x = _ffn_dense(x, p["norm_ffn"][l], mod, p["w1_d"][i], p["w3_d"][i], p["w2_d"][i], norm_final,
                       rows_per_seg=seg)
    else:
        x = _moe(x, p["norm_ffn"][l], mod, p["w_router"][i], p["w1_e"][i], p["w3_e"][i], p["w2_e"][i],
                 norm_final, rows_per_seg=seg)
    return x, new_caches, state_out


def kernel(x_prompt, x_sample, cache_k, cache_v, state_hgrn, c, c_ctx, w_mod, b_mod, norm_mix,
           norm_ffn, w_in, hg_lb, hg_norm, na_rpb, w_fo, w_ho, w_no, w_out, w1_d, w3_d, w2_d,
           w_router, w1_e, w3_e, w2_e, norm_final):
    nb_c, seq_c, d = x_prompt.shape
    nb_l, seq_l, _ = x_sample.shape
    depth = w_in.shape[0]
    past = cache_k.shape[2]

    lbs = jnp.cumsum(jax.nn.softmax(hg_lb.astype(F32), axis=0), axis=0)
    lbs = lbs - lbs[0:1]
    lbp = jnp.stack([jnp.log(lbs), jnp.log1p(-lbs)], axis=2)

    bf = lambda w: w.astype(BF16)
    p = dict(norm_mix=norm_mix, norm_ffn=norm_ffn, w_in=bf(w_in), hg_norm=hg_norm, na_rpb=na_rpb,
             w_fo=bf(w_fo), w_ho=bf(w_ho), w_no=bf(w_no), w_out=bf(w_out), w1_d=bf(w1_d),
             w3_d=bf(w3_d), w2_d=bf(w2_d), w_router=w_router, w1_e=bf(w1_e), w3_e=bf(w3_e),
             w2_e=bf(w2_e), lbp=lbp)

    mods = _mod_vectors(jnp.concatenate([c_ctx[None], c], axis=0), w_mod, b_mod)

    x = x_prompt.reshape(nb_c * seq_c, d)
    caches, state_out = (None, None), None
    for l in range(depth):
        x, caches, state_out = _layer(x, mods[l, 0:1], l, depth, p, nb=nb_c, seq=seq_c, ctx=None,
                                      caches=caches, state_out=state_out,
                                      norm_final=norm_final if l == depth - 1 else None)
    y_prompt = x.reshape(nb_c, seq_c, d)

    xs = x_sample.reshape(nb_l * seq_l, d)
    for l in range(depth):
        ctx = (bf(cache_k[:, l].reshape(nb_l, past, CH)), bf(cache_v[:, l].reshape(nb_l, past, CH)),
               state_hgrn[:, l, 0], state_hgrn[:, l, 1])
        xs, _, _ = _layer(xs, mods[l, 1:], l, depth, p, nb=nb_l, seq=seq_l, ctx=ctx, caches=None,
                          state_out=None, norm_final=norm_final if l == depth - 1 else None)
    y_sample = xs.reshape(nb_l, seq_l, d)

    new_cache_k = caches[0].reshape(nb_c, depth, seq_c, NA_HEADS, NA_HD)
    new_cache_v = caches[1].reshape(nb_c, depth, seq_c, NA_HEADS, NA_HD)
    return (y_prompt, y_sample, new_cache_k, new_cache_v, state_out)
```

```python
import functools
import math

import numpy as np
import jax
import jax.numpy as jnp
from jax import lax
from jax.experimental import pallas as pl
from jax.experimental.pallas import tpu as pltpu

F32 = jnp.float32
BF16 = jnp.bfloat16

LANES = 128
SUBLANES = 8
VMEM_LIMIT = 56 * 1024 * 1024

FT_GROUPS = 4
FT_GD = 128
HG_HEADS = 4
HG_D = 128
NA_HEADS = 8
NA_HD = 64
GRID_W = 64
WIN_R = 8
WIN_C = 16
N_EXPERTS = 8
EPS = 1e-6
CH = 512
NEG = -1e30

ZB_SRC = (9, 10, 11, 12, 13, 14, 0, 1, 4, 5, 6, 7, 8)
ZB_UFT, ZB_HQ, ZB_HI, ZB_HG, ZB_NQ, ZB_NK, ZB_NV = 6, 7, 8, 9, 10, 11, 12
ZF_SRC = (2, 3)
SRC_NK, SRC_NV = 7, 8


def _cparams(sem):
    return pltpu.CompilerParams(dimension_semantics=sem, vmem_limit_bytes=VMEM_LIMIT)


def _const_spec(shape):
    nd = len(shape)
    return pl.BlockSpec(shape, lambda *_: (0,) * nd, pipeline_mode=pl.Buffered(1))


def _silu(x):
    return x * (1.0 / (1.0 + jnp.exp(-x)))


def _sigmoid(x):
    return 1.0 / (1.0 + jnp.exp(-x))


def _norm_mod(x, g, shift, scale):
    ms = jnp.mean(x * x, axis=-1, keepdims=True)
    return (x * lax.rsqrt(ms + EPS) * g) * (1.0 + scale) + shift


def _mod_kernel(c_ref, w_ref, b_ref, o_ref, *, nv, tn):
    w = w_ref[0]
    for v in range(nv):
        s = _silu(c_ref[v])
        s = jnp.concatenate([s] * (tn // LANES), axis=1)
        o_ref[0, v:v + 1, :] = jnp.sum(w * s, axis=0, keepdims=True) + b_ref[0]


def _mod_vectors(cvecs, w_mod, b_mod):
    nv, d = cvecs.shape
    depth, _, n6 = w_mod.shape
    tn = 512
    cb = jnp.broadcast_to(cvecs[:, :, None], (nv, d, LANES))
    out = pl.pallas_call(
        functools.partial(_mod_kernel, nv=nv, tn=tn),
        grid=(depth, n6 // tn),
        in_specs=[
            pl.BlockSpec((nv, d, LANES), lambda l, j: (0, 0, 0)),
            pl.BlockSpec((1, d, tn), lambda l, j: (l, 0, j)),
            pl.BlockSpec((1, 1, tn), lambda l, j: (l, 0, j)),
        ],
        out_specs=pl.BlockSpec((1, nv, tn), lambda l, j: (l, 0, j)),
        out_shape=jax.ShapeDtypeStruct((depth, nv, n6), F32),
        compiler_params=_cparams(("parallel", "parallel")),
        name="mod_vectors",
    )(cb, w_mod, b_mod.reshape(depth, 1, n6))
    return out.reshape(depth, nv, 6, d)


def _in_kernel(x_ref, g_ref, mod_ref, w_ref, *refs, with_cache):
    if with_cache:
        zb_ref, zf_ref, ck_ref, cv_ref = refs[-4:]
    else:
        zb_ref, zf_ref = refs[-2:]
    h = _norm_mod(x_ref[...], g_ref[...], mod_ref[0, 0:1, :], mod_ref[0, 1:2, :]).astype(BF16)
    for src in range(15):
        z = jnp.dot(h, w_ref[:, src * CH:(src + 1) * CH], preferred_element_type=F32)
        if src in ZB_SRC:
            j = ZB_SRC.index(src)
            zb_ref[:, j * CH:(j + 1) * CH] = z.astype(BF16)
        if src in ZF_SRC:
            j = ZF_SRC.index(src)
            zf_ref[:, j * CH:(j + 1) * CH] = z
        if with_cache and src == SRC_NK:
            ck_ref[...] = z
        if with_cache and src == SRC_NV:
            cv_ref[...] = z


def _in_proj(x, g, mod, w_bf, *, rows_per_seg, seq, layer, depth, caches):
    t, d = x.shape
    tm = 256
    with_cache = caches is not None
    spt = rows_per_seg // tm
    in_specs = [
        pl.BlockSpec((tm, d), lambda i: (i, 0)),
        _const_spec((1, d)),
        pl.BlockSpec((1, 6, d), lambda i: (i // spt, 0, 0)),
        _const_spec((d, 15 * CH)),
    ]
    args = [x, g.reshape(1, d), mod, w_bf]
    out_specs = [
        pl.BlockSpec((tm, len(ZB_SRC) * CH), lambda i: (i, 0)),
        pl.BlockSpec((tm, len(ZF_SRC) * CH), lambda i: (i, 0)),
    ]
    out_shape = [
        jax.ShapeDtypeStruct((t, len(ZB_SRC) * CH), BF16),
        jax.ShapeDtypeStruct((t, len(ZF_SRC) * CH), F32),
    ]
    aliases = {}
    if with_cache:
        assert seq == tm, "one row tile per context sequence"
        cspec = pl.BlockSpec((tm, CH), lambda i: (i * depth + layer, 0))
        out_specs += [cspec, cspec]
        cshape = jax.ShapeDtypeStruct((t * depth, CH), F32)
        out_shape += [cshape, cshape]
        if caches[0] is not None:
            in_specs += [pl.BlockSpec(memory_space=pl.ANY)] * 2
            args += list(caches)
            aliases = {4: 2, 5: 3}
    return pl.pallas_call(
        functools.partial(_in_kernel, with_cache=with_cache),
        grid=(t // tm,),
        in_specs=in_specs,
        out_specs=out_specs,
        out_shape=out_shape,
        input_output_aliases=aliases,
        compiler_params=_cparams(("parallel",)),
        name="in_proj",
    )(*args)


def _dft_tables(n, scale):
    k = np.arange(n)
    ang = 2.0 * np.pi * ((k[:, None] * k[None, :]) % n) / n
    return np.cos(ang) * scale, np.sin(ang) * scale


def _ft_ctx_kernel(u_ref, cs_ref, lm_ref, y_ref):
    u = u_ref[...]
    cs = cs_ref[...].astype(BF16)
    zc, zs = [], []
    for g in range(FT_GROUPS):
        z = jnp.dot(u[:, g * FT_GD:(g + 1) * FT_GD], cs, preferred_element_type=F32)
        zc.append(z[:, :FT_GD])
        zs.append(z[:, FT_GD:])
    zz = jnp.concatenate([jnp.concatenate(zc, axis=1), jnp.concatenate(zs, axis=1)], axis=0)
    y = jnp.dot(lm_ref[...].astype(BF16), zz.astype(BF16), preferred_element_type=F32)
    for g in range(FT_GROUPS):
        y_ref[g] = y[:, g * FT_GD:(g + 1) * FT_GD]


def _fourier_ctx(zb, nb, seq):
    cc, sc = _dft_tables(FT_GD, FT_GD ** -0.5)
    cl, sl = _dft_tables(seq, seq ** -0.5)
    cs = jnp.asarray(np.concatenate([cc, sc], axis=1), F32)
    lm = jnp.asarray(np.concatenate([cl, -sl], axis=1), F32)
    return pl.pallas_call(
        _ft_ctx_kernel,
        grid=(nb,),
        in_specs=[
            pl.BlockSpec((seq, CH), lambda b: (b, ZB_UFT)),
            _const_spec(cs.shape),
            _const_spec(lm.shape),
        ],
        out_specs=pl.BlockSpec((FT_GROUPS, seq, FT_GD), lambda b: (0, b, 0)),
        out_shape=jax.ShapeDtypeStruct((FT_GROUPS, nb * seq, FT_GD), F32),
        compiler_params=_cparams(("parallel",)),
        name="fourier_ctx",
    )(zb, cs, lm)


def _ft_lat_kernel(u_ref, cs_ref, w1_ref, twc_ref, tws_ref, w2_ref, y_ref, zr_ref, zi_ref,
                   ar_ref, ai_ref, *, n, lw):
    rows = n * n
    rc = min(rows, 512)
    ng = lw // FT_GD
    cs = cs_ref[...].astype(BF16)
    w1 = w1_ref[...].astype(BF16)
    w2 = w2_ref[...].astype(BF16)
    for r0 in range(0, rows, rc):
        for g in range(ng):
            z = jnp.dot(u_ref[r0:r0 + rc, g * FT_GD:(g + 1) * FT_GD], cs,
                        preferred_element_type=F32)
            zr_ref[g, r0:r0 + rc, :] = z[:, :FT_GD]
            zi_ref[g, r0:r0 + rc, :] = -z[:, FT_GD:]

    def gather(re_ref, im_ref, idx):
        xr = jnp.concatenate([re_ref[g, idx, :] for g in range(ng)], axis=1)
        xi = jnp.concatenate([im_ref[g, idx, :] for g in range(ng)], axis=1)
        return jnp.concatenate([xr, xi], axis=0).astype(BF16)

    def stage1(a, carry):
        idx = pl.ds(a, n, stride=n)
        y = jnp.dot(w1, gather(zr_ref, zi_ref, idx), preferred_element_type=F32)
        yr, yi = y[:n], y[n:]
        tc = jnp.concatenate([twc_ref[a]] * ng, axis=1)
        ts = jnp.concatenate([tws_ref[a]] * ng, axis=1)
        zr = yr * tc - yi * ts
        zi = yr * ts + yi * tc
        for g in range(ng):
            ar_ref[g, idx, :] = zr[:, g * FT_GD:(g + 1) * FT_GD]
            ai_ref[g, idx, :] = zi[:, g * FT_GD:(g + 1) * FT_GD]
        return carry

    lax.fori_loop(0, n, stage1, 0, unroll=4)

    def stage2(q, carry):
        r0 = pl.multiple_of(q * n, n)
        y = jnp.dot(w2, gather(ar_ref, ai_ref, pl.ds(r0, n)), preferred_element_type=F32)
        for g in range(ng):
            y_ref[g, pl.ds(q, n, stride=n), :] = y[:, g * FT_GD:(g + 1) * FT_GD]
        return carry

    lax.fori_loop(0, n, stage2, 0, unroll=4)


def _fourier_lat(zb, nb, seq):
    n = math.isqrt(seq)
    assert n * n == seq and n % SUBLANES == 0
    lw = 256
    cc, sc = _dft_tables(FT_GD, FT_GD ** -0.5)
    cs = jnp.asarray(np.concatenate([cc, sc], axis=1), F32)
    pr, ps = _dft_tables(n, 1.0 / n)
    w1 = jnp.asarray(np.block([[pr, ps], [-ps, pr]]), F32)
    k = np.arange(n)
    ang = 2.0 * np.pi * (k[:, None] * k[None, :]) / seq
    twc = jnp.asarray(np.broadcast_to(np.cos(ang)[:, :, None], (n, n, LANES)), F32)
    tws = jnp.asarray(np.broadcast_to(-np.sin(ang)[:, :, None], (n, n, LANES)), F32)
    wr, ws = _dft_tables(n, 1.0)
    w2 = jnp.asarray(np.concatenate([wr, ws], axis=1), F32)
    nh = CH // lw
    return pl.pallas_call(
        functools.partial(_ft_lat_kernel, n=n, lw=lw),
        grid=(nb, nh),
        in_specs=[
            pl.BlockSpec((seq, lw), lambda b, h: (b, ZB_UFT * nh + h)),
            _const_spec(cs.shape),
            _const_spec(w1.shape),
            _const_spec(twc.shape),
            _const_spec(tws.shape),
            _const_spec(w2.shape),
        ],
        out_specs=pl.BlockSpec((lw // FT_GD, seq, FT_GD), lambda b, h: (h, b, 0)),
        out_shape=jax.ShapeDtypeStruct((FT_GROUPS, nb * seq, FT_GD), F32),
        scratch_shapes=[pltpu.VMEM((lw // FT_GD, seq, FT_GD), F32)] * 4,
        compiler_params=_cparams(("parallel", "parallel")),
        name="fourier_lat",
    )(zb, cs, w1, twc, tws, w2)


def _hgrn_tables(c, rev):
    t = np.arange(c)
    tri = (t[None, :] >= t[:, None]) if rev else (t[None, :] <= t[:, None])
    lv = np.full((c, c), -1, np.int32)
    n, i = 1, 0
    while n < c:
        same = (t[:, None] // (2 * n)) == (t[None, :] // (2 * n))
        hi_t, hi_s = (t[:, None] % (2 * n)) >= n, (t[None, :] % (2 * n)) >= n
        pair = same & (~hi_t & hi_s if rev else hi_t & ~hi_s)
        lv[pair] = i
        n, i = 2 * n, i + 1
    return tri.astype(np.float32), lv


def _split3(x):
    hi = x.astype(BF16)
    r = x - hi.astype(F32)
    mid = r.astype(BF16)
    lo = (r - mid.astype(F32)).astype(BF16)
    return hi, mid, lo


def _hgrn_kernel(hq_ref, hf_ref, hi_ref, lb_ref, tri_ref, lv_ref, *refs, c, nch, rev, has_s0):
    if has_s0:
        s0_ref, o_ref, st_ref, b_ref = refs
    else:
        o_ref, so_ref, st_ref, b_ref = refs
    j = pl.program_id(1)

    @pl.when(j == 0)
    def _():
        for h in range(HG_HEADS):
            st_ref[h] = s0_ref[0, h].T if has_s0 else jnp.zeros((HG_D, HG_D), F32)

    la = lb_ref[0:1, :]
    l1 = lb_ref[1:2, :]
    lv = lv_ref[...]
    row8 = lax.broadcasted_iota(jnp.int32, (c // SUBLANES, SUBLANES, HG_D), 1)

    for ci in range(nch):
        cc = nch - 1 - ci if rev else ci
        rows = slice(cc * c, (cc + 1) * c)
        z = hf_ref[rows, :]
        ls = jnp.minimum(z, 0.0) - jnp.log1p(jnp.exp(-jnp.abs(z)))
        cc_ = l1 + ls
        logf = jnp.maximum(la, cc_) + jnp.log1p(jnp.exp(-jnp.abs(la - cc_)))
        f = jnp.exp(logf)
        kk = 1.0 - f
        b = sum(jnp.dot(tri_ref[...], p, preferred_element_type=F32) for p in _split3(logf))
        b_ref[...] = b
        bl_row = c - 1 if not rev else 0
        blast = b_ref[bl_row:bl_row + 1, :]
        q = _silu(hq_ref[rows, :].astype(F32))
        v = hi_ref[rows, :]
        vf = v.astype(F32)
        brefs = {}
        n = SUBLANES
        while n < c:
            parts = []
            for m in range(c // (2 * n)):
                r = 2 * n * m + (n if rev else n - 1)
                parts.append(jnp.broadcast_to(b_ref[r:r + 1, :], (2 * n, HG_HEADS * HG_D)))
            brefs[n] = jnp.concatenate(parts, axis=0) if len(parts) > 1 else parts[0]
            n *= 2
        outs = []
        for h in range(HG_HEADS):
            hs = slice(h * HG_D, (h + 1) * HG_D)
            qh, kh, bh, vh, vfh = q[:, hs], kk[:, hs], b[:, hs], v[:, hs], vf[:, hs]
            st = st_ref[h]
            qe = (qh * jnp.exp(bh)).astype(BF16)
            o = lax.dot_general(qe, st.astype(BF16), (((1,), (1,)), ((), ())),
                                preferred_element_type=F32)
            b3 = bh.reshape(c // SUBLANES, SUBLANES, HG_D)
            a = jnp.zeros((c, c), F32)
            n, i = 1, 0
            while n < c:
                if n < SUBLANES:
                    bref = None
                    for g in range(SUBLANES // (2 * n)):
                        r = 2 * n * g + (n if rev else n - 1)
                        bc = jnp.broadcast_to(b3[:, r:r + 1, :], b3.shape)
                        bref = bc if bref is None else jnp.where(row8 >= 2 * n * g, bc, bref)
                    bref = bref.reshape(c, HG_D)
                else:
                    bref = brefs[n][:, hs]
                e = jnp.exp(-jnp.abs(bh - bref))
                sn = lax.dot_general((qh * e).astype(BF16), (kh * e).astype(BF16),
                                     (((1,), (1,)), ((), ())), preferred_element_type=F32)
                a = jnp.where(lv == i, sn, a)
                n, i = 2 * n, i + 1
            o = o + jnp.dot(a.astype(BF16), vh, preferred_element_type=F32)
            o = o + jnp.sum(qh * kh, axis=-1, keepdims=True) * vfh
            outs.append(o)
            blh = blast[:, hs]
            ke = (kh * jnp.exp(blh - bh)).astype(BF16)
            kv = lax.dot_general(vh, ke, (((0,), (0,)), ((), ())), preferred_element_type=F32)
            st_ref[h] = st * jnp.exp(blh) + kv
        o_ref[rows, :] = jnp.concatenate(outs, axis=1)

    if not has_s0:
        @pl.when(j == pl.num_programs(1) - 1)
        def _():
            for h in range(HG_HEADS):
                so_ref[0, 0, 0, h] = st_ref[h].T


def _hgrn(zb, zf, lbp, s0, state_out, *, nb, seq, rev, layer, depth, c, lb_rows):
    t = zb.shape[0]
    nblk = seq // lb_rows
    nch = lb_rows // c
    tri, lv = _hgrn_tables(c, rev)
    d = int(rev)

    def rowmap(b, j):
        return b * nblk + (nblk - 1 - j if rev else j)

    in_specs = [
        pl.BlockSpec((lb_rows, CH), lambda b, j: (rowmap(b, j), ZB_HQ)),
        pl.BlockSpec((lb_rows, CH), lambda b, j: (rowmap(b, j), d)),
        pl.BlockSpec((lb_rows, CH), lambda b, j: (rowmap(b, j), ZB_HI)),
        _const_spec((2, CH)),
        _const_spec((c, c)),
        _const_spec((c, c)),
    ]
    args = [zb, zf, zb, lbp, jnp.asarray(tri, BF16), jnp.asarray(lv)]
    has_s0 = s0 is not None
    if has_s0:
        in_specs.append(pl.BlockSpec((1, HG_HEADS, HG_D, HG_D), lambda b, j: (b, 0, 0, 0)))
        args.append(s0)
    sshape = (nb, depth, 2, HG_HEADS, HG_D, HG_D)
    aliases = {}
    if state_out is not None:
        in_specs.append(pl.BlockSpec(memory_space=pl.ANY))
        args.append(state_out)
        aliases = {len(args) - 1: 1}
    kern = functools.partial(_hgrn_kernel, c=c, nch=nch, rev=rev, has_s0=has_s0)
    if state_out is not None:
        kern = functools.partial(_drop_ref, kern, len(args) - 1)
    out_specs = [pl.BlockSpec((lb_rows, CH), lambda b, j: (rowmap(b, j), 0))]
    out_shape = [jax.ShapeDtypeStruct((t, CH), F32)]
    if not has_s0:
        out_specs.append(pl.BlockSpec((1, 1, 1, HG_HEADS, HG_D, HG_D), lambda b, j: (b, layer, d, 0, 0, 0)))
        out_shape.append(jax.ShapeDtypeStruct(sshape, F32))
    outs = pl.pallas_call(
        kern,
        grid=(nb, nblk),
        in_specs=in_specs,
        out_specs=out_specs,
        out_shape=out_shape,
        scratch_shapes=[pltpu.VMEM((HG_HEADS, HG_D, HG_D), F32), pltpu.VMEM((c, CH), F32)],
        input_output_aliases=aliases,
        compiler_params=_cparams(("parallel", "arbitrary")),
        name="hgrn_bwd" if rev else "hgrn_fwd",
    )(*args)
    return (outs[0], None) if has_s0 else tuple(outs)


def _drop_ref(kern, pos, *refs):
    return kern(*refs[:pos], *refs[pos + 1:])


def _head_mask(a):
    lane = lax.broadcasted_iota(jnp.int32, (1, 2 * NA_HD), 1)
    return (lane // NA_HD) == a


def _qk(q, k):
    return lax.dot_general(q, k, (((1,), (1,)), ((), ())), preferred_element_type=F32)


def _attn_ctx_kernel(q_ref, k_ref, v_ref, o_ref):
    w = 2 * NA_HD
    for pr in range(NA_HEADS // 2):
        ps = slice(pr * w, (pr + 1) * w)
        q = q_ref[:, ps] * (NA_HD ** -0.5)
        k = k_ref[:, ps]
        v = v_ref[:, ps]
        outs = []
        for a in range(2):
            s = _qk(jnp.where(_head_mask(a), q, jnp.zeros_like(q)), k)
            p = jnp.exp(s - jnp.max(s, axis=-1, keepdims=True))
            l = jnp.sum(p, axis=-1, keepdims=True)
            outs.append(jnp.dot(p.astype(BF16), v, preferred_element_type=F32) / l)
        o_ref[:, ps] = jnp.where(_head_mask(0), outs[0], outs[1]).astype(BF16)


def _attn_ctx(zb, nb, seq):
    return pl.pallas_call(
        _attn_ctx_kernel,
        grid=(nb,),
        in_specs=[
            pl.BlockSpec((seq, CH), lambda b: (b, ZB_NQ)),
            pl.BlockSpec((seq, CH), lambda b: (b, ZB_NK)),
            pl.BlockSpec((seq, CH), lambda b: (b, ZB_NV)),
        ],
        out_specs=pl.BlockSpec((seq, CH), lambda b: (b, 0)),
        out_shape=jax.ShapeDtypeStruct((nb * seq, CH), BF16),
        compiler_params=_cparams(("parallel",)),
        name="attn_ctx",
    )(zb, zb, zb)


NA_TR = 8


def _na_geometry(rows):
    kr = min(WIN_R, rows)
    span = min(NA_TR + WIN_R, rows)
    ntile = rows // NA_TR
    pats, var, starts = [], [], []
    for t in range(ntile):
        r0 = t * NA_TR
        start = int(np.clip(r0 - kr // 2, 0, rows - span))
        qr = r0 + np.arange(NA_TR)
        row_start = np.clip(qr - kr // 2, 0, rows - kr)
        krow = start + np.arange(span)
        rv = (krow[None, :] >= row_start[:, None]) & (krow[None, :] < row_start[:, None] + kr)
        d_row = np.where(rv, krow[None, :] - qr[:, None] + WIN_R - 1, -1)
        for vi, p in enumerate(pats):
            if np.array_equal(p, d_row):
                var.append(vi)
                break
        else:
            var.append(len(pats))
            pats.append(d_row)
        starts.append(start)
    return pats, np.asarray(var, np.int32), np.asarray(starts, np.int32), span


def _na_col_tables(rpb):
    qc = np.arange(GRID_W)
    col_start = np.clip(qc - WIN_C // 2, 0, GRID_W - WIN_C)
    cv = (qc[None, :] >= col_start[:, None]) & (qc[None, :] < col_start[:, None] + WIN_C)
    d_col = qc[None, :] - qc[:, None] + WIN_C - 1
    onehot = (d_col[None] == np.arange(2 * WIN_C - 1)[:, None, None]) & cv[None]
    t = jnp.einsum("hrd,dqk->hrqk", rpb, jnp.asarray(onehot, F32), precision=lax.Precision.HIGHEST)
    t = jnp.where(jnp.asarray(cv)[None, None], t, NEG)
    return jnp.concatenate([t, t], axis=-1)


def _na_kernel(var_ref, start_ref, q_ref, k_ref, v_ref, kc_ref, vc_ref, tab_ref, o_ref, bias_ref,
               *, nkey, pats):
    t = pl.program_id(2)
    var = var_ref[t]
    changed = jnp.logical_or(t == 0, var != var_ref[jnp.maximum(t - 1, 0)])
    left = lax.broadcasted_iota(jnp.int32, (GRID_W, 2 * GRID_W), 1) < GRID_W
    neg = jnp.full((GRID_W, 2 * GRID_W), NEG, F32)

    for vi, d_row in enumerate(pats):
        @pl.when(jnp.logical_and(changed, var == vi))
        def _(d_row=d_row):
            for a in range(2):
                for i in range(d_row.shape[0]):
                    for j in range(0, d_row.shape[1], 2):
                        r0, r1 = int(d_row[i, j]), int(d_row[i, j + 1])
                        b0 = tab_ref[a, r0] if r0 >= 0 else neg
                        b1 = tab_ref[a, r1] if r1 >= 0 else neg
                        blk = neg if (r0 < 0 and r1 < 0) else jnp.where(left, b0, b1)
                        bias_ref[a, i * GRID_W:(i + 1) * GRID_W, j * GRID_W:(j + 2) * GRID_W] = blk

    st = pl.multiple_of(start_ref[t] * GRID_W, GRID_W)
    q = q_ref[...] * (NA_HD ** -0.5)
    kw = k_ref[pl.ds(st, nkey), :]
    vw = v_ref[pl.ds(st, nkey), :]
    kc = kc_ref[0]
    vc = vc_ref[0]
    outs = []
    for a in range(2):
        qa = jnp.where(_head_mask(a), q, jnp.zeros_like(q))
        s1 = _qk(qa, kw) + bias_ref[a]
        s2 = _qk(qa, kc)
        m = jnp.maximum(jnp.max(s1, axis=-1, keepdims=True), jnp.max(s2, axis=-1, keepdims=True))
        p1 = jnp.exp(s1 - m)
        p2 = jnp.exp(s2 - m)
        l = jnp.sum(p1, axis=-1, keepdims=True) + jnp.sum(p2, axis=-1, keepdims=True)
        o = (jnp.dot(p1.astype(BF16), vw, preferred_element_type=F32)
             + jnp.dot(p2.astype(BF16), vc, preferred_element_type=F32))
        outs.append(o / l)
    o_ref[...] = jnp.where(_head_mask(0), outs[0], outs[1]).astype(BF16)


def _na_latent(zb, kctx, vctx, rpb, nb, seq):
    rows = seq // GRID_W
    pats, var, starts, span = _na_geometry(rows)
    nkey = span * GRID_W
    tq = NA_TR * GRID_W
    ntile = rows // NA_TR
    hp = NA_HEADS // 2
    w = 2 * NA_HD
    past = kctx.shape[1]
    assert span % 2 == 0
    tabs = _na_col_tables(rpb)
    nrow = tabs.shape[1]
    grid_spec = pltpu.PrefetchScalarGridSpec(
        num_scalar_prefetch=2,
        grid=(nb, hp, ntile),
        in_specs=[
            pl.BlockSpec((tq, w), lambda b, p, t, var, st: (b * ntile + t, ZB_NQ * hp + p)),
            pl.BlockSpec((seq, w), lambda b, p, t, var, st: (b, ZB_NK * hp + p)),
            pl.BlockSpec((seq, w), lambda b, p, t, var, st: (b, ZB_NV * hp + p)),
            pl.BlockSpec((1, past, w), lambda b, p, t, var, st: (b, 0, p)),
            pl.BlockSpec((1, past, w), lambda b, p, t, var, st: (b, 0, p)),
            pl.BlockSpec((2, nrow, GRID_W, 2 * GRID_W), lambda b, p, t, var, st: (p, 0, 0, 0)),
        ],
        out_specs=pl.BlockSpec((tq, w), lambda b, p, t, var, st: (b * ntile + t, p)),
        scratch_shapes=[pltpu.VMEM((2, tq, nkey), F32)],
    )
    return pl.pallas_call(
        functools.partial(_na_kernel, nkey=nkey, pats=pats),
        grid_spec=grid_spec,
        out_shape=jax.ShapeDtypeStruct((nb * seq, CH), BF16),
        compiler_params=_cparams(("parallel", "parallel", "arbitrary")),
        name="na_latent",
    )(jnp.asarray(var), jnp.asarray(starts), zb, zb, zb, kctx, vctx, tabs)


def _merge_kernel(x_ref, mod_ref, gate_ref, yft_ref, of_ref, ob_ref, hg_ref, on_ref, hn_ref,
                  wfo_ref, who_ref, wno_ref, wout_ref, o_ref):
    d = x_ref.shape[1]
    yft = jnp.concatenate([yft_ref[g] for g in range(FT_GROUPS)], axis=1)
    y_ft = jnp.dot(yft.astype(BF16), wfo_ref[...], preferred_element_type=F32)
    o = of_ref[...] + ob_ref[...]
    parts = []
    for h in range(HG_HEADS):
        oh = o[:, h * HG_D:(h + 1) * HG_D]
        ms = jnp.mean(oh * oh, axis=-1, keepdims=True)
        parts.append(oh * lax.rsqrt(ms + EPS))
    oh = jnp.concatenate(parts, axis=1) * hn_ref[...]
    oh = (oh * _silu(hg_ref[...].astype(F32))).astype(BF16)
    y_hg = jnp.dot(oh, who_ref[...], preferred_element_type=F32)
    y_na = jnp.dot(on_ref[...], wno_ref[...], preferred_element_type=F32)
    g = gate_ref[...].astype(F32)
    merged = (_sigmoid(g[:, :d]) * y_ft + _sigmoid(g[:, d:2 * d]) * y_hg
              + _sigmoid(g[:, 2 * d:]) * y_na)
    y = jnp.dot(merged.astype(BF16), wout_ref[...], preferred_element_type=F32)
    o_ref[...] = x_ref[...] + mod_ref[0, 2:3, :] * y


def _merge(x, mod, zb, yft, o_f, o_b, o_n, hg_norm, w_fo, w_ho, w_no, w_out, *, rows_per_seg):
    t, d = x.shape
    tm = 256
    spt = rows_per_seg // tm
    assert 3 * d == 6 * CH, "the three merge gates fill the first six bf16 splits"
    row = lambda i: (i, 0)
    return pl.pallas_call(
        _merge_kernel,
        grid=(t // tm,),
        in_specs=[
            pl.BlockSpec((tm, d), row),
            pl.BlockSpec((1, 6, d), lambda i: (i // spt, 0, 0)),
            pl.BlockSpec((tm, 3 * d), row),
            pl.BlockSpec((FT_GROUPS, tm, FT_GD), lambda i: (0, i, 0)),
            pl.BlockSpec((tm, CH), row),
            pl.BlockSpec((tm, CH), row),
            pl.BlockSpec((tm, CH), lambda i: (i, ZB_HG)),
            pl.BlockSpec((tm, CH), row),
            _const_spec((1, CH)),
            _const_spec((CH, d)),
            _const_spec((CH, d)),
            _const_spec((CH, d)),
            _const_spec((d, d)),
        ],
        out_specs=pl.BlockSpec((tm, d), row),
        out_shape=jax.ShapeDtypeStruct((t, d), F32),
        compiler_params=_cparams(("parallel",)),
        name="merge",
    )(x, mod, zb, yft, o_f, o_b, zb, o_n, hg_norm.reshape(1, CH), w_fo, w_ho, w_no, w_out)


FF_CHUNK = 256


def _swiglu_acc(h, w1_ref, w3_ref, w2_ref, lead=()):
    dff = w1_ref.shape[-1]
    acc = None
    for c0 in range(0, dff, FF_CHUNK):
        cs = slice(c0, c0 + FF_CHUNK)
        g = jnp.dot(h, w1_ref[lead + (slice(None), cs)], preferred_element_type=F32)
        u = jnp.dot(h, w3_ref[lead + (slice(None), cs)], preferred_element_type=F32)
        a = (_silu(g) * u).astype(BF16)
        y = jnp.dot(a, w2_ref[lead + (cs, slice(None))], preferred_element_type=F32)
        acc = y if acc is None else acc + y
    return acc


def _ffn_kernel(x_ref, g_ref, mod_ref, w1_ref, w3_ref, w2_ref, *refs, final):
    o_ref = refs[-1]
    x = x_ref[...]
    h = _norm_mod(x, g_ref[...], mod_ref[0, 3:4, :], mod_ref[0, 4:5, :]).astype(BF16)
    y = x + mod_ref[0, 5:6, :] * _swiglu_acc(h, w1_ref, w3_ref, w2_ref)
    if final:
        ms = jnp.mean(y * y, axis=-1, keepdims=True)
        y = y * lax.rsqrt(ms + EPS) * refs[0][...]
    o_ref[...] = y


def _ffn_dense(x, g, mod, w1, w3, w2, norm_final, *, rows_per_seg):
    t, d = x.shape
    dff = w1.shape[1]
    tm = 256
    spt = rows_per_seg // tm
    final = norm_final is not None
    in_specs = [
        pl.BlockSpec((tm, d), lambda i: (i, 0)),
        _const_spec((1, d)),
        pl.BlockSpec((1, 6, d), lambda i: (i // spt, 0, 0)),
        _const_spec((d, dff)),
        _const_spec((d, dff)),
        _const_spec((dff, d)),
    ]
    args = [x, g.reshape(1, d), mod, w1, w3, w2]
    if final:
        in_specs.append(_const_spec((1, d)))
        args.append(norm_final.reshape(1, d))
    return pl.pallas_call(
        functools.partial(_ffn_kernel, final=final),
        grid=(t // tm,),
        in_specs=in_specs,
        out_specs=pl.BlockSpec((tm, d), lambda i: (i, 0)),
        out_shape=jax.ShapeDtypeStruct((t, d), F32),
        compiler_params=_cparams(("parallel",)),
        name="ffn_dense",
    )(*args)


def _router_kernel(x_ref, g_ref, mod_ref, wr_ref, h_ref, r_ref):
    h = _norm_mod(x_ref[...], g_ref[...], mod_ref[0, 3:4, :], mod_ref[0, 4:5, :])
    h_ref[...] = h
    logits = jnp.dot(h, wr_ref[...], precision=lax.Precision.HIGHEST,
                     preferred_element_type=F32)
    lane = lax.broadcasted_iota(jnp.int32, logits.shape, 1)
    logits = jnp.where(lane < N_EXPERTS, logits, -jnp.inf)
    m1 = jnp.max(logits, axis=-1, keepdims=True)
    i1 = jnp.min(jnp.where(logits == m1, lane, LANES), axis=-1, keepdims=True)
    rest = jnp.where(lane == i1, -jnp.inf, logits)
    m2 = jnp.max(rest, axis=-1, keepdims=True)
    i2 = jnp.min(jnp.where(rest == m2, lane, LANES), axis=-1, keepdims=True)
    e2 = jnp.exp(m2 - m1)
    w1 = 1.0 / (1.0 + e2)
    w2 = e2 / (1.0 + e2)
    r = jnp.where(lane == 0, i1.astype(F32), jnp.where(lane == 1, i2.astype(F32),
                  jnp.where(lane == 2, w1, jnp.where(lane == 3, w2, 0.0))))
    r_ref[...] = r


def _router(x, g, mod, w_router, *, rows_per_seg):
    t, d = x.shape
    tm = 256
    spt = rows_per_seg // tm
    wr = jnp.zeros((d, LANES), F32).at[:, :N_EXPERTS].set(w_router)
    return pl.pallas_call(
        _router_kernel,
        grid=(t // tm,),
        in_specs=[
            pl.BlockSpec((tm, d), lambda i: (i, 0)),
            _const_spec((1, d)),
            pl.BlockSpec((1, 6, d), lambda i: (i // spt, 0, 0)),
            _const_spec((d, LANES)),
        ],
        out_specs=[pl.BlockSpec((tm, d), lambda i: (i, 0)), pl.BlockSpec((tm, LANES), lambda i: (i, 0))],
        out_shape=[jax.ShapeDtypeStruct((t, d), F32), jax.ShapeDtypeStruct((t, LANES), F32)],
        compiler_params=_cparams(("parallel",)),
        name="router",
    )(x, g.reshape(1, d), mod, wr)


MOE_TM = 512


def _moe_ffn_kernel(te_ref, nt_ref, src_ref, h_hbm, w1_ref, w3_ref, w2_ref, y_ref, hbuf, sem):
    i = pl.program_id(0)
    nt = nt_ref[0]
    slot = i % 2

    def gather(tile, s):
        base = tile * MOE_TM

        def issue(r, carry):
            pltpu.make_async_copy(h_hbm.at[pl.ds(src_ref[base + r], 1)],
                                  hbuf.at[s, pl.ds(r, 1)], sem.at[s]).start()
            return carry

        lax.fori_loop(0, MOE_TM, issue, 0, unroll=8)

    @pl.when(jnp.logical_and(i == 0, nt > 0))
    def _():
        gather(0, 0)

    @pl.when(i < nt)
    def _():
        @pl.when(i + 1 < nt)
        def _():
            gather(i + 1, 1 - slot)

        pltpu.make_async_copy(h_hbm.at[pl.ds(0, MOE_TM)], hbuf.at[slot], sem.at[slot]).wait()
        y_ref[...] = _swiglu_acc(hbuf[slot].astype(BF16), w1_ref, w3_ref, w2_ref, lead=(0,))

    @pl.when(i >= nt)
    def _():
        y_ref[...] = jnp.zeros_like(y_ref)


def _moe_ffn(h, tile_expert, n_tiles, src, w1, w3, w2, ntile_max):
    t, d = h.shape
    dff = w1.shape[-1]
    grid_spec = pltpu.PrefetchScalarGridSpec(
        num_scalar_prefetch=3,
        grid=(ntile_max,),
        in_specs=[
            pl.BlockSpec(memory_space=pl.ANY),
            pl.BlockSpec((1, d, dff), lambda i, te, nt, src: (te[i], 0, 0)),
            pl.BlockSpec((1, d, dff), lambda i, te, nt, src: (te[i], 0, 0)),
            pl.BlockSpec((1, dff, d), lambda i, te, nt, src: (te[i], 0, 0)),
        ],
        out_specs=pl.BlockSpec((MOE_TM, d), lambda i, te, nt, src: (i, 0)),
        scratch_shapes=[pltpu.VMEM((2, MOE_TM, d), F32), pltpu.SemaphoreType.DMA((2,))],
    )
    return pl.pallas_call(
        _moe_ffn_kernel,
        grid_spec=grid_spec,
        out_shape=jax.ShapeDtypeStruct((ntile_max * MOE_TM, d), F32),
        compiler_params=_cparams(("arbitrary",)),
        name="moe_ffn",
    )(tile_expert, n_tiles, src, h, w1, w3, w2)


def _combine_kernel(pos_ref, x_ref, mod_ref, r_ref, y_hbm, *refs, final, tm):
    o_ref, ybuf, sem = refs[-3:]
    i = pl.program_id(0)
    slot = i % 2

    def gather(tile, s):
        base = 2 * tile * tm

        def issue(r, carry):
            for k in range(2):
                pltpu.make_async_copy(y_hbm.at[pl.ds(pos_ref[base + 2 * r + k], 1)],
                                      ybuf.at[s, k, pl.ds(r, 1)], sem.at[s]).start()
            return carry

        lax.fori_loop(0, tm, issue, 0, unroll=4)

    @pl.when(i == 0)
    def _():
        gather(0, 0)

    @pl.when(i + 1 < pl.num_programs(0))
    def _():
        gather(i + 1, 1 - slot)

    for k in range(2):
        pltpu.make_async_copy(y_hbm.at[pl.ds(0, tm)], ybuf.at[slot, k], sem.at[slot]).wait()
    r = r_ref[...]
    f = r[:, 2:3] * ybuf[slot, 0] + r[:, 3:4] * ybuf[slot, 1]
    y = x_ref[...] + mod_ref[0, 5:6, :] * f
    if final:
        ms = jnp.mean(y * y, axis=-1, keepdims=True)
        y = y * lax.rsqrt(ms + EPS) * refs[0][...]
    o_ref[...] = y


def _moe_combine(x, mod, route, pos, y_sorted, norm_final, *, rows_per_seg):
    t, d = x.shape
    tm = 256
    spt = rows_per_seg // tm
    final = norm_final is not None
    in_specs = [
        pl.BlockSpec((tm, d), lambda i, pos: (i, 0)),
        pl.BlockSpec((1, 6, d), lambda i, pos: (i // spt, 0, 0)),
        pl.BlockSpec((tm, LANES), lambda i, pos: (i, 0)),
        pl.BlockSpec(memory_space=pl.ANY),
    ]
    args = [x, mod, route, y_sorted]
    if final:
        in_specs.append(pl.BlockSpec((1, d), lambda i, pos: (0, 0)))
        args.append(norm_final.reshape(1, d))
    grid_spec = pltpu.PrefetchScalarGridSpec(
        num_scalar_prefetch=1,
        grid=(t // tm,),
        in_specs=in_specs,
        out_specs=pl.BlockSpec((tm, d), lambda i, pos: (i, 0)),
        scratch_shapes=[pltpu.VMEM((2, 2, tm, d), F32), pltpu.SemaphoreType.DMA((2,))],
    )
    return pl.pallas_call(
        functools.partial(_combine_kernel, final=final, tm=tm),
        grid_spec=grid_spec,
        out_shape=jax.ShapeDtypeStruct((t, d), F32),
        compiler_params=_cparams(("arbitrary",)),
        name="moe_combine",
    )(pos.reshape(-1), *args)


def _moe(x, g, mod, w_router, w1, w3, w2, norm_final, *, rows_per_seg):
    t, d = x.shape
    h, route = _router(x, g, mod, w_router, rows_per_seg=rows_per_seg)
    ids = route[:, :2].astype(jnp.int32)
    onehot = (ids[:, :, None] == jnp.arange(N_EXPERTS)[None, None, :]).astype(jnp.int32)
    sel = onehot.sum(axis=1)
    counts = sel.sum(axis=0)
    rank = jnp.cumsum(sel, axis=0) - sel
    padded = ((counts + MOE_TM - 1) // MOE_TM) * MOE_TM
    pstart = jnp.cumsum(padded) - padded
    cstart = jnp.cumsum(counts) - counts
    pos_e = pstart[None, :] + rank
    pos = jnp.take_along_axis(pos_e, ids, axis=1).astype(jnp.int32)
    ntile_max = (2 * t) // MOE_TM + N_EXPERTS
    tok = jnp.arange(t, dtype=jnp.int32)
    key = jnp.sort((ids * t + tok[:, None]).reshape(-1))
    sorted_tok = key % t
    tile_start = jnp.arange(ntile_max, dtype=jnp.int32) * MOE_TM
    pend = pstart + padded
    tile_expert = jnp.minimum(jnp.sum(tile_start[:, None] >= pend[None, :], axis=1), N_EXPERTS - 1)
    n_tiles = (jnp.sum(padded) // MOE_TM).astype(jnp.int32).reshape(1)
    p = jnp.arange(ntile_max * MOE_TM, dtype=jnp.int32)
    e_p = jnp.repeat(tile_expert, MOE_TM)
    r_p = jnp.minimum(p - pstart[e_p], jnp.maximum(counts[e_p] - 1, 0))
    src = sorted_tok[jnp.clip(cstart[e_p] + r_p, 0, 2 * t - 1)].astype(jnp.int32)
    y_sorted = _moe_ffn(h, tile_expert.astype(jnp.int32), n_tiles, src, w1, w3, w2, ntile_max)
    return _moe_combine(x, mod, route, pos, y_sorted, norm_final, rows_per_seg=rows_per_seg)


def _layer(x, mod, l, depth, p, *, nb, seq, ctx, caches, state_out, norm_final):
    is_ctx = ctx is None
    seg = nb * seq if is_ctx else seq
    outs = _in_proj(x, p["norm_mix"][l], mod, p["w_in"][l], rows_per_seg=seg, seq=seq, layer=l,
                    depth=depth, caches=caches if is_ctx else None)
    zb, zf = outs[0], outs[1]
    new_caches = tuple(outs[2:]) if is_ctx else None
    if is_ctx:
        yft = _fourier_ctx(zb, nb, seq)
        o_n = _attn_ctx(zb, nb, seq)
    else:
        yft = _fourier_lat(zb, nb, seq)
        o_n = _na_latent(zb, ctx[0], ctx[1], p["na_rpb"][l], nb, seq)
    hg_c = 128
    lbr = 256 if is_ctx else 512
    lbr = min(lbr, seq)
    o_f, state_out = _hgrn(zb, zf, p["lbp"][l][0], None if is_ctx else ctx[2],
                           state_out if is_ctx else None, nb=nb, seq=seq, rev=False, layer=l,
                           depth=depth, c=hg_c, lb_rows=lbr)
    o_b, state_out = _hgrn(zb, zf, p["lbp"][l][1], None if is_ctx else ctx[3],
                           state_out if is_ctx else None, nb=nb, seq=seq, rev=True, layer=l,
                           depth=depth, c=hg_c, lb_rows=lbr)
    x = _merge(x, mod, zb, yft, o_f, o_b, o_n, p["hg_norm"][l], p["w_fo"][l], p["w_ho"][l],
               p["w_no"][l], p["w_out"][l], rows_per_seg=seg)
    i = l // 2
    if l % 2 == 0:
        x = _ffn_dense(x, p["norm_ffn"][l], mod, p["w1_d"][i], p["w3_d"][i], p["w2_d"][i], norm_final,
                       rows_per_seg=seg)
    else:
        x = _moe(x, p["norm_ffn"][l], mod, p["w_router"][i], p["w1_e"][i], p["w3_e"][i], p["w2_e"][i],
                 norm_final, rows_per_seg=seg)
    return x, new_caches, state_out


def kernel(x_prompt, x_sample, cache_k, cache_v, state_hgrn, c, c_ctx, w_mod, b_mod, norm_mix,
           norm_ffn, w_in, hg_lb, hg_norm, na_rpb, w_fo, w_ho, w_no, w_out, w1_d, w3_d, w2_d,
           w_router, w1_e, w3_e, w2_e, norm_final):
    nb_c, seq_c, d = x_prompt.shape
    nb_l, seq_l, _ = x_sample.shape
    depth = w_in.shape[0]
    past = cache_k.shape[2]

    lbs = jnp.cumsum(jax.nn.softmax(hg_lb.astype(F32), axis=0), axis=0)
    lbs = lbs - lbs[0:1]
    lbp = jnp.stack([jnp.log(lbs), jnp.log1p(-lbs)], axis=2)

    bf = lambda w: w.astype(BF16)
    p = dict(norm_mix=norm_mix, norm_ffn=norm_ffn, w_in=bf(w_in), hg_norm=hg_norm, na_rpb=na_rpb,
             w_fo=bf(w_fo), w_ho=bf(w_ho), w_no=bf(w_no), w_out=bf(w_out), w1_d=bf(w1_d),
             w3_d=bf(w3_d), w2_d=bf(w2_d), w_router=w_router, w1_e=bf(w1_e), w3_e=bf(w3_e),
             w2_e=bf(w2_e), lbp=lbp)

    mods = _mod_vectors(jnp.concatenate([c_ctx[None], c], axis=0), w_mod, b_mod)

    x = x_prompt.reshape(nb_c * seq_c, d)
    caches, state_out = (None, None), None
    for l in range(depth):
        x, caches, state_out = _layer(x, mods[l, 0:1], l, depth, p, nb=nb_c, seq=seq_c, ctx=None,
                                      caches=caches, state_out=state_out,
                                      norm_final=norm_final if l == depth - 1 else None)
    y_prompt = x.reshape(nb_c, seq_c, d)

    xs = x_sample.reshape(nb_l * seq_l, d)
    for l in range(depth):
        ctx = (bf(cache_k[:, l].reshape(nb_l, past, CH)), bf(cache_v[:, l].reshape(nb_l, past, CH)),
               state_hgrn[:, l, 0], state_hgrn[:, l, 1])
        xs, _, _ = _layer(xs, mods[l, 1:], l, depth, p, nb=nb_l, seq=seq_l, ctx=ctx, caches=None,
                          state_out=None, norm_final=norm_final if l == depth - 1 else None)
    y_sample = xs.reshape(nb_l, seq_l, d)

    new_cache_k = caches[0].reshape(nb_c, depth, seq_c, NA_HEADS, NA_HD)
    new_cache_v = caches[1].reshape(nb_c, depth, seq_c, NA_HEADS, NA_HD)
    return (y_prompt, y_sample, new_cache_k, new_cache_v, state_out)
```

```python
import functools
import math

import numpy as np
import jax
import jax.numpy as jnp
from jax import lax
from jax.experimental import pallas as pl
from jax.experimental.pallas import tpu as pltpu

F32 = jnp.float32
BF16 = jnp.bfloat16

LANES = 128
SUBLANES = 8
VMEM_LIMIT = 56 * 1024 * 1024

FT_GROUPS = 4
FT_GD = 128
HG_HEADS = 4
HG_D = 128
NA_HEADS = 8
NA_HD = 64
GRID_W = 64
WIN_R = 8
WIN_C = 16
N_EXPERTS = 8
EPS = 1e-6
CH = 512
NEG = -1e30

ZB_SRC = (9, 10, 11, 12, 13, 14, 0, 1, 4, 5, 6, 7, 8)
ZB_UFT, ZB_HQ, ZB_HI, ZB_HG, ZB_NQ, ZB_NK, ZB_NV = 6, 7, 8, 9, 10, 11, 12
ZF_SRC = (2, 3)
SRC_NK, SRC_NV = 7, 8


def _cparams(sem):
    return pltpu.CompilerParams(dimension_semantics=sem, vmem_limit_bytes=VMEM_LIMIT)


def _const_spec(shape):
    nd = len(shape)
    return pl.BlockSpec(shape, lambda *_: (0,) * nd, pipeline_mode=pl.Buffered(1))


def _silu(x):
    return x * (1.0 / (1.0 + jnp.exp(-x)))


def _sigmoid(x):
    return 1.0 / (1.0 + jnp.exp(-x))


def _norm_mod(x, g, shift, scale):
    ms = jnp.mean(x * x, axis=-1, keepdims=True)
    return (x * lax.rsqrt(ms + EPS) * g) * (1.0 + scale) + shift


def _mod_kernel(c_ref, w_ref, b_ref, o_ref, *, nv, tn):
    w = w_ref[0]
    for v in range(nv):
        s = _silu(c_ref[v])
        s = jnp.concatenate([s] * (tn // LANES), axis=1)
        o_ref[0, v:v + 1, :] = jnp.sum(w * s, axis=0, keepdims=True) + b_ref[0]


def _mod_vectors(cvecs, w_mod, b_mod):
    nv, d = cvecs.shape
    depth, _, n6 = w_mod.shape
    tn = 512
    cb = jnp.broadcast_to(cvecs[:, :, None], (nv, d, LANES))
    out = pl.pallas_call(
        functools.partial(_mod_kernel, nv=nv, tn=tn),
        grid=(depth, n6 // tn),
        in_specs=[
            pl.BlockSpec((nv, d, LANES), lambda l, j: (0, 0, 0)),
            pl.BlockSpec((1, d, tn), lambda l, j: (l, 0, j)),
            pl.BlockSpec((1, 1, tn), lambda l, j: (l, 0, j)),
        ],
        out_specs=pl.BlockSpec((1, nv, tn), lambda l, j: (l, 0, j)),
        out_shape=jax.ShapeDtypeStruct((depth, nv, n6), F32),
        compiler_params=_cparams(("parallel", "parallel")),
        name="mod_vectors",
    )(cb, w_mod, b_mod.reshape(depth, 1, n6))
    return out.reshape(depth, nv, 6, d)


def _in_kernel(x_ref, g_ref, mod_ref, w_ref, *refs, with_cache):
    if with_cache:
        zb_ref, zf_ref, ck_ref, cv_ref = refs[-4:]
    else:
        zb_ref, zf_ref = refs[-2:]
    h = _norm_mod(x_ref[...], g_ref[...], mod_ref[0, 0:1, :], mod_ref[0, 1:2, :]).astype(BF16)
    for src in range(15):
        z = jnp.dot(h, w_ref[:, src * CH:(src + 1) * CH], preferred_element_type=F32)
        if src in ZB_SRC:
            j = ZB_SRC.index(src)
            zb_ref[:, j * CH:(j + 1) * CH] = z.astype(BF16)
        if src in ZF_SRC:
            j = ZF_SRC.index(src)
            zf_ref[:, j * CH:(j + 1) * CH] = z
        if with_cache and src == SRC_NK:
            ck_ref[...] = z
        if with_cache and src == SRC_NV:
            cv_ref[...] = z


def _in_proj(x, g, mod, w_bf, *, rows_per_seg, seq, layer, depth, caches):
    t, d = x.shape
    tm = 256
    with_cache = caches is not None
    spt = rows_per_seg // tm
    in_specs = [
        pl.BlockSpec((tm, d), lambda i: (i, 0)),
        _const_spec((1, d)),
        pl.BlockSpec((1, 6, d), lambda i: (i // spt, 0, 0)),
        _const_spec((d, 15 * CH)),
    ]
    args = [x, g.reshape(1, d), mod, w_bf]
    out_specs = [
        pl.BlockSpec((tm, len(ZB_SRC) * CH), lambda i: (i, 0)),
        pl.BlockSpec((tm, len(ZF_SRC) * CH), lambda i: (i, 0)),
    ]
    out_shape = [
        jax.ShapeDtypeStruct((t, len(ZB_SRC) * CH), BF16),
        jax.ShapeDtypeStruct((t, len(ZF_SRC) * CH), F32),
    ]
    aliases = {}
    if with_cache:
        assert seq == tm, "one row tile per context sequence"
        cspec = pl.BlockSpec((tm, CH), lambda i: (i * depth + layer, 0))
        out_specs += [cspec, cspec]
        cshape = jax.ShapeDtypeStruct((t * depth, CH), F32)
        out_shape += [cshape, cshape]
        if caches[0] is not None:
            in_specs += [pl.BlockSpec(memory_space=pl.ANY)] * 2
            args += list(caches)
            aliases = {4: 2, 5: 3}
    return pl.pallas_call(
        functools.partial(_in_kernel, with_cache=with_cache),
        grid=(t // tm,),
        in_specs=in_specs,
        out_specs=out_specs,
        out_shape=out_shape,
        input_output_aliases=aliases,
        compiler_params=_cparams(("parallel",)),
        name="in_proj",
    )(*args)


def _dft_tables(n, scale):
    k = np.arange(n)
    ang = 2.0 * np.pi * ((k[:, None] * k[None, :]) % n) / n
    return np.cos(ang) * scale, np.sin(ang) * scale


def _ft_ctx_kernel(u_ref, cs_ref, lm_ref, y_ref):
    u = u_ref[...]
    cs = cs_ref[...].astype(BF16)
    zc, zs = [], []
    for g in range(FT_GROUPS):
        z = jnp.dot(u[:, g * FT_GD:(g + 1) * FT_GD], cs, preferred_element_type=F32)
        zc.append(z[:, :FT_GD])
        zs.append(z[:, FT_GD:])
    zz = jnp.concatenate([jnp.concatenate(zc, axis=1), jnp.concatenate(zs, axis=1)], axis=0)
    y = jnp.dot(lm_ref[...].astype(BF16), zz.astype(BF16), preferred_element_type=F32)
    for g in range(FT_GROUPS):
        y_ref[g] = y[:, g * FT_GD:(g + 1) * FT_GD]


def _fourier_ctx(zb, nb, seq):
    cc, sc = _dft_tables(FT_GD, FT_GD ** -0.5)
    cl, sl = _dft_tables(seq, seq ** -0.5)
    cs = jnp.asarray(np.concatenate([cc, sc], axis=1), F32)
    lm = jnp.asarray(np.concatenate([cl, -sl], axis=1), F32)
    return pl.pallas_call(
        _ft_ctx_kernel,
        grid=(nb,),
        in_specs=[
            pl.BlockSpec((seq, CH), lambda b: (b, ZB_UFT)),
            _const_spec(cs.shape),
            _const_spec(lm.shape),
        ],
        out_specs=pl.BlockSpec((FT_GROUPS, seq, FT_GD), lambda b: (0, b, 0)),
        out_shape=jax.ShapeDtypeStruct((FT_GROUPS, nb * seq, FT_GD), F32),
        compiler_params=_cparams(("parallel",)),
        name="fourier_ctx",
    )(zb, cs, lm)


def _ft_lat_kernel(u_ref, cs_ref, w1_ref, twc_ref, tws_ref, w2_ref, y_ref, zr_ref, zi_ref,
                   ar_ref, ai_ref, *, n, lw):
    rows = n * n
    rc = min(rows, 512)
    ng = lw // FT_GD
    cs = cs_ref[...].astype(BF16)
    w1 = w1_ref[...].astype(BF16)
    w2 = w2_ref[...].astype(BF16)
    for r0 in range(0, rows, rc):
        for g in range(ng):
            z = jnp.dot(u_ref[r0:r0 + rc, g * FT_GD:(g + 1) * FT_GD], cs,
                        preferred_element_type=F32)
            zr_ref[g, r0:r0 + rc, :] = z[:, :FT_GD]
            zi_ref[g, r0:r0 + rc, :] = -z[:, FT_GD:]

    def gather(re_ref, im_ref, idx):
        xr = jnp.concatenate([re_ref[g, idx, :] for g in range(ng)], axis=1)
        xi = jnp.concatenate([im_ref[g, idx, :] for g in range(ng)], axis=1)
        return jnp.concatenate([xr, xi], axis=0).astype(BF16)

    def stage1(a, carry):
        idx = pl.ds(a, n, stride=n)
        y = jnp.dot(w1, gather(zr_ref, zi_ref, idx), preferred_element_type=F32)
        yr, yi = y[:n], y[n:]
        tc = jnp.concatenate([twc_ref[a]] * ng, axis=1)
        ts = jnp.concatenate([tws_ref[a]] * ng, axis=1)
        zr = yr * tc - yi * ts
        zi = yr * ts + yi * tc
        for g in range(ng):
            ar_ref[g, idx, :] = zr[:, g * FT_GD:(g + 1) * FT_GD]
            ai_ref[g, idx, :] = zi[:, g * FT_GD:(g + 1) * FT_GD]
        return carry

    lax.fori_loop(0, n, stage1, 0, unroll=4)

    def stage2(q, carry):
        r0 = pl.multiple_of(q * n, n)
        y = jnp.dot(w2, gather(ar_ref, ai_ref, pl.ds(r0, n)), preferred_element_type=F32)
        for g in range(ng):
            y_ref[g, pl.ds(q, n, stride=n), :] = y[:, g * FT_GD:(g + 1) * FT_GD]
        return carry

    lax.fori_loop(0, n, stage2, 0, unroll=4)


def _fourier_lat(zb, nb, seq):
    n = math.isqrt(seq)
    assert n * n == seq and n % SUBLANES == 0
    lw = 256
    cc, sc = _dft_tables(FT_GD, FT_GD ** -0.5)
    cs = jnp.asarray(np.concatenate([cc, sc], axis=1), F32)
    pr, ps = _dft_tables(n, 1.0 / n)
    w1 = jnp.asarray(np.block([[pr, ps], [-ps, pr]]), F32)
    k = np.arange(n)
    ang = 2.0 * np.pi * (k[:, None] * k[None, :]) / seq
    twc = jnp.asarray(np.broadcast_to(np.cos(ang)[:, :, None], (n, n, LANES)), F32)
    tws = jnp.asarray(np.broadcast_to(-np.sin(ang)[:, :, None], (n, n, LANES)), F32)
    wr, ws = _dft_tables(n, 1.0)
    w2 = jnp.asarray(np.concatenate([wr, ws], axis=1), F32)
    nh = CH // lw
    return pl.pallas_call(
        functools.partial(_ft_lat_kernel, n=n, lw=lw),
        grid=(nb, nh),
        in_specs=[
            pl.BlockSpec((seq, lw), lambda b, h: (b, ZB_UFT * nh + h)),
            _const_spec(cs.shape),
            _const_spec(w1.shape),
            _const_spec(twc.shape),
            _const_spec(tws.shape),
            _const_spec(w2.shape),
        ],
        out_specs=pl.BlockSpec((lw // FT_GD, seq, FT_GD), lambda b, h: (h, b, 0)),
        out_shape=jax.ShapeDtypeStruct((FT_GROUPS, nb * seq, FT_GD), F32),
        scratch_shapes=[pltpu.VMEM((lw // FT_GD, seq, FT_GD), F32)] * 4,
        compiler_params=_cparams(("parallel", "parallel")),
        name="fourier_lat",
    )(zb, cs, w1, twc, tws, w2)


HG_SAFE = 160.0


def _hgrn_tables(c, rev):
    t = np.arange(c)
    tri = (t[None, :] >= t[:, None]) if rev else (t[None, :] <= t[:, None])
    lv = np.full((c, c), -1, np.int32)
    n, i = 1, 0
    while n < c:
        same = (t[:, None] // (2 * n)) == (t[None, :] // (2 * n))
        hi_t, hi_s = (t[:, None] % (2 * n)) >= n, (t[None, :] % (2 * n)) >= n
        pair = same & (~hi_t & hi_s if rev else hi_t & ~hi_s)
        lv[pair] = i
        n, i = 2 * n, i + 1
    return tri.astype(np.float32), lv


def _split3(x):
    hi = x.astype(BF16)
    r = x - hi.astype(F32)
    mid = r.astype(BF16)
    lo = (r - mid.astype(F32)).astype(BF16)
    return hi, mid, lo


def _hgrn_kernel(hq_ref, hf_ref, hi_ref, lb_ref, tri_ref, lv_ref, *refs, c, nch, rev, has_s0):
    if has_s0:
        s0_ref, o_ref, st_ref, b_ref, k_ref, q_ref = refs
    else:
        o_ref, so_ref, st_ref, b_ref, k_ref, q_ref = refs
    j = pl.program_id(1)

    @pl.when(j == 0)
    def _():
        for h in range(HG_HEADS):
            st_ref[h] = s0_ref[0, h].T if has_s0 else jnp.zeros((HG_D, HG_D), F32)

    la = lb_ref[0:1, :]
    l1 = lb_ref[1:2, :]
    lv = lv_ref[...]
    row8 = lax.broadcasted_iota(jnp.int32, (c // SUBLANES, SUBLANES, HG_D), 1)
    ti = lax.broadcasted_iota(jnp.int32, (c, c), 0)
    si = lax.broadcasted_iota(jnp.int32, (c, c), 1)
    causal = (si >= ti) if rev else (si <= ti)
    bl_row = c - 1 if not rev else 0

    tmax = jnp.float32(0.0)
    for cc in range(nch):
        rows = slice(cc * c, (cc + 1) * c)
        z = hf_ref[rows, :]
        ls = jnp.minimum(z, 0.0) - jnp.log1p(jnp.exp(-jnp.abs(z)))
        cc_ = l1 + ls
        logf = jnp.maximum(la, cc_) + jnp.log1p(jnp.exp(-jnp.abs(la - cc_)))
        k_ref[rows, :] = 1.0 - jnp.exp(logf)
        b = sum(jnp.dot(tri_ref[...], p, preferred_element_type=F32) for p in _split3(logf))
        b_ref[rows, :] = b
        q_ref[rows, :] = _silu(hq_ref[rows, :].astype(F32))
        tmax = jnp.maximum(tmax, jnp.max(jnp.abs(b[bl_row:bl_row + 1, :])))
    safe = tmax <= HG_SAFE

    def chunk(cc, fast):
        rows = slice(cc * c, (cc + 1) * c)
        b = b_ref[rows, :]
        kk = k_ref[rows, :]
        q = q_ref[rows, :]
        v = hi_ref[rows, :]
        blast = b_ref[cc * c + bl_row:cc * c + bl_row + 1, :]
        brefs = {}
        n = SUBLANES
        while not fast and n < c:
            parts = []
            for m in range(c // (2 * n)):
                r = cc * c + 2 * n * m + (n if rev else n - 1)
                parts.append(jnp.broadcast_to(b_ref[r:r + 1, :], (2 * n, HG_HEADS * HG_D)))
            brefs[n] = jnp.concatenate(parts, axis=0) if len(parts) > 1 else parts[0]
            n *= 2
        outs = []
        for h in range(HG_HEADS):
            hs = slice(h * HG_D, (h + 1) * HG_D)
            qh, kh, bh, vh, blh = q[:, hs], kk[:, hs], b[:, hs], v[:, hs], blast[:, hs]
            st = st_ref[h]
            if fast:
                cm = 0.5 * blh
                qn = qh * jnp.exp(bh - cm)
                kn = kh * jnp.exp(cm - bh)
                sn = lax.dot_general(qn.astype(BF16), kn.astype(BF16), (((1,), (1,)), ((), ())),
                                     preferred_element_type=F32)
                a = jnp.where(causal, sn, 0.0)
                qe = (qn * jnp.exp(cm)).astype(BF16)
                ke = (kn * jnp.exp(blh - cm)).astype(BF16)
                o = lax.dot_general(qe, st.astype(BF16), (((1,), (1,)), ((), ())),
                                    preferred_element_type=F32)
                o = o + jnp.dot(a.astype(BF16), vh, preferred_element_type=F32)
            else:
                o, ke = _hgrn_levels(qh, kh, bh, vh, blh, st, brefs, hs, lv, row8, c, rev)
            outs.append(o)
            kv = lax.dot_general(vh, ke, (((0,), (0,)), ((), ())), preferred_element_type=F32)
            st_ref[h] = st * jnp.exp(blh) + kv
        o_ref[rows, :] = jnp.concatenate(outs, axis=1)

    order = [nch - 1 - ci if rev else ci for ci in range(nch)]

    @pl.when(safe)
    def _():
        for cc in order:
            chunk(cc, True)

    @pl.when(jnp.logical_not(safe))
    def _():
        for cc in order:
            chunk(cc, False)

    if not has_s0:
        @pl.when(j == pl.num_programs(1) - 1)
        def _():
            for h in range(HG_HEADS):
                so_ref[0, 0, 0, h] = st_ref[h].T


def _hgrn_levels(qh, kh, bh, vh, blh, st, brefs, hs, lv, row8, c, rev):
    qe = (qh * jnp.exp(bh)).astype(BF16)
    o = lax.dot_general(qe, st.astype(BF16), (((1,), (1,)), ((), ())), preferred_element_type=F32)
    b3 = bh.reshape(c // SUBLANES, SUBLANES, HG_D)
    a = jnp.zeros((c, c), F32)
    n, i = 1, 0
    while n < c:
        if n < SUBLANES:
            bref = None
            for g in range(SUBLANES // (2 * n)):
                r = 2 * n * g + (n if rev else n - 1)
                bc = jnp.broadcast_to(b3[:, r:r + 1, :], b3.shape)
                bref = bc if bref is None else jnp.where(row8 >= 2 * n * g, bc, bref)
            bref = bref.reshape(c, HG_D)
        else:
            bref = brefs[n][:, hs]
        e = jnp.exp(-jnp.abs(bh - bref))
        sn = lax.dot_general((qh * e).astype(BF16), (kh * e).astype(BF16),
                             (((1,), (1,)), ((), ())), preferred_element_type=F32)
        a = jnp.where(lv == i, sn, a)
        n, i = 2 * n, i + 1
    o = o + jnp.dot(a.astype(BF16), vh, preferred_element_type=F32)
    o = o + jnp.sum(qh * kh, axis=-1, keepdims=True) * vh.astype(F32)
    return o, (kh * jnp.exp(blh - bh)).astype(BF16)


def _hgrn(zb, zf, lbp, s0, state_out, *, nb, seq, rev, layer, depth, c, lb_rows):
    t = zb.shape[0]
    nblk = seq // lb_rows
    nch = lb_rows // c
    tri, lv = _hgrn_tables(c, rev)
    d = int(rev)

    def rowmap(b, j):
        return b * nblk + (nblk - 1 - j if rev else j)

    in_specs = [
        pl.BlockSpec((lb_rows, CH), lambda b, j: (rowmap(b, j), ZB_HQ)),
        pl.BlockSpec((lb_rows, CH), lambda b, j: (rowmap(b, j), d)),
        pl.BlockSpec((lb_rows, CH), lambda b, j: (rowmap(b, j), ZB_HI)),
        _const_spec((2, CH)),
        _const_spec((c, c)),
        _const_spec((c, c)),
    ]
    args = [zb, zf, zb, lbp, jnp.asarray(tri, BF16), jnp.asarray(lv)]
    has_s0 = s0 is not None
    if has_s0:
        in_specs.append(pl.BlockSpec((1, HG_HEADS, HG_D, HG_D), lambda b, j: (b, 0, 0, 0)))
        args.append(s0)
    sshape = (nb, depth, 2, HG_HEADS, HG_D, HG_D)
    aliases = {}
    if state_out is not None:
        in_specs.append(pl.BlockSpec(memory_space=pl.ANY))
        args.append(state_out)
        aliases = {len(args) - 1: 1}
    kern = functools.partial(_hgrn_kernel, c=c, nch=nch, rev=rev, has_s0=has_s0)
    if state_out is not None:
        kern = functools.partial(_drop_ref, kern, len(args) - 1)
    out_specs = [pl.BlockSpec((lb_rows, CH), lambda b, j: (rowmap(b, j), 0))]
    out_shape = [jax.ShapeDtypeStruct((t, CH), F32)]
    if not has_s0:
        out_specs.append(pl.BlockSpec((1, 1, 1, HG_HEADS, HG_D, HG_D), lambda b, j: (b, layer, d, 0, 0, 0)))
        out_shape.append(jax.ShapeDtypeStruct(sshape, F32))
    outs = pl.pallas_call(
        kern,
        grid=(nb, nblk),
        in_specs=in_specs,
        out_specs=out_specs,
        out_shape=out_shape,
        scratch_shapes=[pltpu.VMEM((HG_HEADS, HG_D, HG_D), F32)] + [pltpu.VMEM((lb_rows, CH), F32)] * 3,
        input_output_aliases=aliases,
        compiler_params=_cparams(("parallel", "arbitrary")),
        name="hgrn_bwd" if rev else "hgrn_fwd",
    )(*args)
    return (outs[0], None) if has_s0 else tuple(outs)


def _drop_ref(kern, pos, *refs):
    return kern(*refs[:pos], *refs[pos + 1:])


def _head_mask(a):
    lane = lax.broadcasted_iota(jnp.int32, (1, 2 * NA_HD), 1)
    return (lane // NA_HD) == a


def _qk(q, k):
    return lax.dot_general(q, k, (((1,), (1,)), ((), ())), preferred_element_type=F32)


def _attn_ctx_kernel(q_ref, k_ref, v_ref, o_ref):
    w = 2 * NA_HD
    for pr in range(NA_HEADS // 2):
        ps = slice(pr * w, (pr + 1) * w)
        q = q_ref[:, ps] * (NA_HD ** -0.5)
        k = k_ref[:, ps]
        v = v_ref[:, ps]
        outs = []
        for a in range(2):
            s = _qk(jnp.where(_head_mask(a), q, jnp.zeros_like(q)), k)
            p = jnp.exp(s - jnp.max(s, axis=-1, keepdims=True))
            l = jnp.sum(p, axis=-1, keepdims=True)
            outs.append(jnp.dot(p.astype(BF16), v, preferred_element_type=F32) / l)
        o_ref[:, ps] = jnp.where(_head_mask(0), outs[0], outs[1]).astype(BF16)


def _attn_ctx(zb, nb, seq):
    return pl.pallas_call(
        _attn_ctx_kernel,
        grid=(nb,),
        in_specs=[
            pl.BlockSpec((seq, CH), lambda b: (b, ZB_NQ)),
            pl.BlockSpec((seq, CH), lambda b: (b, ZB_NK)),
            pl.BlockSpec((seq, CH), lambda b: (b, ZB_NV)),
        ],
        out_specs=pl.BlockSpec((seq, CH), lambda b: (b, 0)),
        out_shape=jax.ShapeDtypeStruct((nb * seq, CH), BF16),
        compiler_params=_cparams(("parallel",)),
        name="attn_ctx",
    )(zb, zb, zb)


NA_TR = 8


def _na_geometry(rows):
    kr = min(WIN_R, rows)
    span = min(NA_TR + WIN_R, rows)
    ntile = rows // NA_TR
    pats, var, starts = [], [], []
    for t in range(ntile):
        r0 = t * NA_TR
        start = int(np.clip(r0 - kr // 2, 0, rows - span))
        qr = r0 + np.arange(NA_TR)
        row_start = np.clip(qr - kr // 2, 0, rows - kr)
        krow = start + np.arange(span)
        rv = (krow[None, :] >= row_start[:, None]) & (krow[None, :] < row_start[:, None] + kr)
        d_row = np.where(rv, krow[None, :] - qr[:, None] + WIN_R - 1, -1)
        for vi, p in enumerate(pats):
            if np.array_equal(p, d_row):
                var.append(vi)
                break
        else:
            var.append(len(pats))
            pats.append(d_row)
        starts.append(start)
    return pats, np.asarray(var, np.int32), np.asarray(starts, np.int32), span


def _na_col_tables(rpb):
    qc = np.arange(GRID_W)
    col_start = np.clip(qc - WIN_C // 2, 0, GRID_W - WIN_C)
    cv = (qc[None, :] >= col_start[:, None]) & (qc[None, :] < col_start[:, None] + WIN_C)
    d_col = qc[None, :] - qc[:, None] + WIN_C - 1
    onehot = (d_col[None] == np.arange(2 * WIN_C - 1)[:, None, None]) & cv[None]
    t = jnp.einsum("hrd,dqk->hrqk", rpb, jnp.asarray(onehot, F32), precision=lax.Precision.HIGHEST)
    t = jnp.where(jnp.asarray(cv)[None, None], t, NEG)
    return jnp.concatenate([t, t], axis=-1)


def _na_kernel(var_ref, start_ref, q_ref, k_ref, v_ref, kc_ref, vc_ref, tab_ref, o_ref, bias_ref,
               *, nkey, pats):
    t = pl.program_id(2)
    var = var_ref[t]
    changed = jnp.logical_or(t == 0, var != var_ref[jnp.maximum(t - 1, 0)])
    left = lax.broadcasted_iota(jnp.int32, (GRID_W, 2 * GRID_W), 1) < GRID_W
    neg = jnp.full((GRID_W, 2 * GRID_W), NEG, F32)

    for vi, d_row in enumerate(pats):
        @pl.when(jnp.logical_and(changed, var == vi))
        def _(d_row=d_row):
            for a in range(2):
                for i in range(d_row.shape[0]):
                    for j in range(0, d_row.shape[1], 2):
                        r0, r1 = int(d_row[i, j]), int(d_row[i, j + 1])
                        b0 = tab_ref[a, r0] if r0 >= 0 else neg
                        b1 = tab_ref[a, r1] if r1 >= 0 else neg
                        blk = neg if (r0 < 0 and r1 < 0) else jnp.where(left, b0, b1)
                        bias_ref[a, i * GRID_W:(i + 1) * GRID_W, j * GRID_W:(j + 2) * GRID_W] = blk

    st = pl.multiple_of(start_ref[t] * GRID_W, GRID_W)
    q = q_ref[...] * (NA_HD ** -0.5)
    kw = k_ref[pl.ds(st, nkey), :]
    vw = v_ref[pl.ds(st, nkey), :]
    kc = kc_ref[0]
    vc = vc_ref[0]
    outs = []
    for a in range(2):
        qa = jnp.where(_head_mask(a), q, jnp.zeros_like(q))
        s1 = _qk(qa, kw) + bias_ref[a]
        s2 = _qk(qa, kc)
        m = jnp.maximum(jnp.max(s1, axis=-1, keepdims=True), jnp.max(s2, axis=-1, keepdims=True))
        p1 = jnp.exp(s1 - m)
        p2 = jnp.exp(s2 - m)
        l = jnp.sum(p1, axis=-1, keepdims=True) + jnp.sum(p2, axis=-1, keepdims=True)
        o = (jnp.dot(p1.astype(BF16), vw, preferred_element_type=F32)
             + jnp.dot(p2.astype(BF16), vc, preferred_element_type=F32))
        outs.append(o / l)
    o_ref[...] = jnp.where(_head_mask(0), outs[0], outs[1]).astype(BF16)


def _na_latent(zb, kctx, vctx, rpb, nb, seq):
    rows = seq // GRID_W
    pats, var, starts, span = _na_geometry(rows)
    nkey = span * GRID_W
    tq = NA_TR * GRID_W
    ntile = rows // NA_TR
    hp = NA_HEADS // 2
    w = 2 * NA_HD
    past = kctx.shape[1]
    assert span % 2 == 0
    tabs = _na_col_tables(rpb)
    nrow = tabs.shape[1]
    grid_spec = pltpu.PrefetchScalarGridSpec(
        num_scalar_prefetch=2,
        grid=(nb, hp, ntile),
        in_specs=[
            pl.BlockSpec((tq, w), lambda b, p, t, var, st: (b * ntile + t, ZB_NQ * hp + p)),
            pl.BlockSpec((seq, w), lambda b, p, t, var, st: (b, ZB_NK * hp + p)),
            pl.BlockSpec((seq, w), lambda b, p, t, var, st: (b, ZB_NV * hp + p)),
            pl.BlockSpec((1, past, w), lambda b, p, t, var, st: (b, 0, p)),
            pl.BlockSpec((1, past, w), lambda b, p, t, var, st: (b, 0, p)),
            pl.BlockSpec((2, nrow, GRID_W, 2 * GRID_W), lambda b, p, t, var, st: (p, 0, 0, 0)),
        ],
        out_specs=pl.BlockSpec((tq, w), lambda b, p, t, var, st: (b * ntile + t, p)),
        scratch_shapes=[pltpu.VMEM((2, tq, nkey), F32)],
    )
    return pl.pallas_call(
        functools.partial(_na_kernel, nkey=nkey, pats=pats),
        grid_spec=grid_spec,
        out_shape=jax.ShapeDtypeStruct((nb * seq, CH), BF16),
        compiler_params=_cparams(("parallel", "parallel", "arbitrary")),
        name="na_latent",
    )(jnp.asarray(var), jnp.asarray(starts), zb, zb, zb, kctx, vctx, tabs)


def _merge_kernel(x_ref, mod_ref, gate_ref, yft_ref, of_ref, ob_ref, hg_ref, on_ref, hn_ref,
                  wfo_ref, who_ref, wno_ref, wout_ref, o_ref):
    d = x_ref.shape[1]
    yft = jnp.concatenate([yft_ref[g] for g in range(FT_GROUPS)], axis=1)
    y_ft = jnp.dot(yft.astype(BF16), wfo_ref[...], preferred_element_type=F32)
    o = of_ref[...] + ob_ref[...]
    parts = []
    for h in range(HG_HEADS):
        oh = o[:, h * HG_D:(h + 1) * HG_D]
        ms = jnp.mean(oh * oh, axis=-1, keepdims=True)
        parts.append(oh * lax.rsqrt(ms + EPS))
    oh = jnp.concatenate(parts, axis=1) * hn_ref[...]
    oh = (oh * _silu(hg_ref[...].astype(F32))).astype(BF16)
    y_hg = jnp.dot(oh, who_ref[...], preferred_element_type=F32)
    y_na = jnp.dot(on_ref[...], wno_ref[...], preferred_element_type=F32)
    g = gate_ref[...].astype(F32)
    merged = (_sigmoid(g[:, :d]) * y_ft + _sigmoid(g[:, d:2 * d]) * y_hg
              + _sigmoid(g[:, 2 * d:]) * y_na)
    y = jnp.dot(merged.astype(BF16), wout_ref[...], preferred_element_type=F32)
    o_ref[...] = x_ref[...] + mod_ref[0, 2:3, :] * y


def _merge(x, mod, zb, yft, o_f, o_b, o_n, hg_norm, w_fo, w_ho, w_no, w_out, *, rows_per_seg):
    t, d = x.shape
    tm = 256
    spt = rows_per_seg // tm
    assert 3 * d == 6 * CH, "the three merge gates fill the first six bf16 splits"
    row = lambda i: (i, 0)
    return pl.pallas_call(
        _merge_kernel,
        grid=(t // tm,),
        in_specs=[
            pl.BlockSpec((tm, d), row),
            pl.BlockSpec((1, 6, d), lambda i: (i // spt, 0, 0)),
            pl.BlockSpec((tm, 3 * d), row),
            pl.BlockSpec((FT_GROUPS, tm, FT_GD), lambda i: (0, i, 0)),
            pl.BlockSpec((tm, CH), row),
            pl.BlockSpec((tm, CH), row),
            pl.BlockSpec((tm, CH), lambda i: (i, ZB_HG)),
            pl.BlockSpec((tm, CH), row),
            _const_spec((1, CH)),
            _const_spec((CH, d)),
            _const_spec((CH, d)),
            _const_spec((CH, d)),
            _const_spec((d, d)),
        ],
        out_specs=pl.BlockSpec((tm, d), row),
        out_shape=jax.ShapeDtypeStruct((t, d), F32),
        compiler_params=_cparams(("parallel",)),
        name="merge",
    )(x, mod, zb, yft, o_f, o_b, zb, o_n, hg_norm.reshape(1, CH), w_fo, w_ho, w_no, w_out)


FF_CHUNK = 256


def _swiglu_acc(h, w1_ref, w3_ref, w2_ref, lead=(), before_chunk=None):
    dff = w1_ref.shape[-1]
    acc = None
    for c0 in range(0, dff, FF_CHUNK):
        if before_chunk is not None:
            before_chunk(c0 // FF_CHUNK, dff // FF_CHUNK)
        cs = slice(c0, c0 + FF_CHUNK)
        g = jnp.dot(h, w1_ref[lead + (slice(None), cs)], preferred_element_type=F32)
        u = jnp.dot(h, w3_ref[lead + (slice(None), cs)], preferred_element_type=F32)
        a = (_silu(g) * u).astype(BF16)
        y = jnp.dot(a, w2_ref[lead + (cs, slice(None))], preferred_element_type=F32)
        acc = y if acc is None else acc + y
    return acc


def _ffn_kernel(x_ref, g_ref, mod_ref, w1_ref, w3_ref, w2_ref, *refs, final):
    o_ref = refs[-1]
    x = x_ref[...]
    h = _norm_mod(x, g_ref[...], mod_ref[0, 3:4, :], mod_ref[0, 4:5, :]).astype(BF16)
    y = x + mod_ref[0, 5:6, :] * _swiglu_acc(h, w1_ref, w3_ref, w2_ref)
    if final:
        ms = jnp.mean(y * y, axis=-1, keepdims=True)
        y = y * lax.rsqrt(ms + EPS) * refs[0][...]
    o_ref[...] = y


def _ffn_dense(x, g, mod, w1, w3, w2, norm_final, *, rows_per_seg):
    t, d = x.shape
    dff = w1.shape[1]
    tm = 256
    spt = rows_per_seg // tm
    final = norm_final is not None
    in_specs = [
        pl.BlockSpec((tm, d), lambda i: (i, 0)),
        _const_spec((1, d)),
        pl.BlockSpec((1, 6, d), lambda i: (i // spt, 0, 0)),
        _const_spec((d, dff)),
        _const_spec((d, dff)),
        _const_spec((dff, d)),
    ]
    args = [x, g.reshape(1, d), mod, w1, w3, w2]
    if final:
        in_specs.append(_const_spec((1, d)))
        args.append(norm_final.reshape(1, d))
    return pl.pallas_call(
        functools.partial(_ffn_kernel, final=final),
        grid=(t // tm,),
        in_specs=in_specs,
        out_specs=pl.BlockSpec((tm, d), lambda i: (i, 0)),
        out_shape=jax.ShapeDtypeStruct((t, d), F32),
        compiler_params=_cparams(("parallel",)),
        name="ffn_dense",
    )(*args)


def _router_kernel(x_ref, g_ref, mod_ref, wr_ref, h_ref, r_ref):
    h = _norm_mod(x_ref[...], g_ref[...], mod_ref[0, 3:4, :], mod_ref[0, 4:5, :])
    h_ref[...] = h
    logits = jnp.dot(h, wr_ref[...], precision=lax.Precision.HIGHEST,
                     preferred_element_type=F32)
    lane = lax.broadcasted_iota(jnp.int32, logits.shape, 1)
    logits = jnp.where(lane < N_EXPERTS, logits, -jnp.inf)
    m1 = jnp.max(logits, axis=-1, keepdims=True)
    i1 = jnp.min(jnp.where(logits == m1, lane, LANES), axis=-1, keepdims=True)
    rest = jnp.where(lane == i1, -jnp.inf, logits)
    m2 = jnp.max(rest, axis=-1, keepdims=True)
    i2 = jnp.min(jnp.where(rest == m2, lane, LANES), axis=-1, keepdims=True)
    e2 = jnp.exp(m2 - m1)
    w1 = 1.0 / (1.0 + e2)
    w2 = e2 / (1.0 + e2)
    r = jnp.where(lane == 0, i1.astype(F32), jnp.where(lane == 1, i2.astype(F32),
                  jnp.where(lane == 2, w1, jnp.where(lane == 3, w2, 0.0))))
    r_ref[...] = r


def _router(x, g, mod, w_router, *, rows_per_seg):
    t, d = x.shape
    tm = 256
    spt = rows_per_seg // tm
    wr = jnp.zeros((d, LANES), F32).at[:, :N_EXPERTS].set(w_router)
    return pl.pallas_call(
        _router_kernel,
        grid=(t // tm,),
        in_specs=[
            pl.BlockSpec((tm, d), lambda i: (i, 0)),
            _const_spec((1, d)),
            pl.BlockSpec((1, 6, d), lambda i: (i // spt, 0, 0)),
            _const_spec((d, LANES)),
        ],
        out_specs=[pl.BlockSpec((tm, d), lambda i: (i, 0)), pl.BlockSpec((tm, LANES), lambda i: (i, 0))],
        out_shape=[jax.ShapeDtypeStruct((t, d), F32), jax.ShapeDtypeStruct((t, LANES), F32)],
        compiler_params=_cparams(("parallel",)),
        name="router",
    )(x, g.reshape(1, d), mod, wr)


MOE_TM = 512


def _moe_ffn_kernel(te_ref, nt_ref, src_ref, h_hbm, w1_ref, w3_ref, w2_ref, y_ref, hbuf, sem):
    i = pl.program_id(0)
    nt = nt_ref[0]
    slot = i % 2

    def start_row(tile, r, s):
        pltpu.make_async_copy(h_hbm.at[pl.ds(src_ref[tile * MOE_TM + r], 1)],
                              hbuf.at[s, pl.ds(r, 1)], sem.at[s]).start()

    def gather(tile, s):
        def issue(r, carry):
            start_row(tile, r, s)
            return carry

        lax.fori_loop(0, MOE_TM, issue, 0, unroll=8)

    @pl.when(i == 0)
    def _():
        gather(0, 0)

    pltpu.make_async_copy(h_hbm.at[pl.ds(0, MOE_TM)], hbuf.at[slot], sem.at[slot]).wait()

    @pl.when(i < nt)
    def _():
        def next_rows(ci, n):
            per = -(-MOE_TM // n)
            for r in range(ci * per, min((ci + 1) * per, MOE_TM)):
                start_row(i + 1, r, 1 - slot)

        y_ref[...] = _swiglu_acc(hbuf[slot].astype(BF16), w1_ref, w3_ref, w2_ref, lead=(0,),
                                 before_chunk=next_rows)

    @pl.when(i >= nt)
    def _():
        y_ref[...] = jnp.zeros_like(y_ref)

        @pl.when(i + 1 < pl.num_programs(0))
        def _():
            gather(i + 1, 1 - slot)


def _moe_ffn(h, tile_expert, n_tiles, src, w1, w3, w2, ntile_max):
    t, d = h.shape
    dff = w1.shape[-1]
    grid_spec = pltpu.PrefetchScalarGridSpec(
        num_scalar_prefetch=3,
        grid=(ntile_max,),
        in_specs=[
            pl.BlockSpec(memory_space=pl.ANY),
            pl.BlockSpec((1, d, dff), lambda i, te, nt, src: (te[i], 0, 0)),
            pl.BlockSpec((1, d, dff), lambda i, te, nt, src: (te[i], 0, 0)),
            pl.BlockSpec((1, dff, d), lambda i, te, nt, src: (te[i], 0, 0)),
        ],
        out_specs=pl.BlockSpec((MOE_TM, d), lambda i, te, nt, src: (i, 0)),
        scratch_shapes=[pltpu.VMEM((2, MOE_TM, d), F32), pltpu.SemaphoreType.DMA((2,))],
    )
    return pl.pallas_call(
        _moe_ffn_kernel,
        grid_spec=grid_spec,
        out_shape=jax.ShapeDtypeStruct((ntile_max * MOE_TM, d), F32),
        compiler_params=_cparams(("arbitrary",)),
        name="moe_ffn",
    )(tile_expert, n_tiles, src, h, w1, w3, w2)


def _combine_kernel(pos_ref, x_ref, mod_ref, r_ref, y_hbm, *refs, final, tm):
    o_ref, ybuf, sem = refs[-3:]
    i = pl.program_id(0)
    slot = i % 2

    def gather(tile, s):
        base = 2 * tile * tm

        def issue(r, carry):
            for k in range(2):
                pltpu.make_async_copy(y_hbm.at[pl.ds(pos_ref[base + 2 * r + k], 1)],
                                      ybuf.at[s, k, pl.ds(r, 1)], sem.at[s]).start()
            return carry

        lax.fori_loop(0, tm, issue, 0, unroll=4)

    @pl.when(i == 0)
    def _():
        gather(0, 0)

    @pl.when(i + 1 < pl.num_programs(0))
    def _():
        gather(i + 1, 1 - slot)

    for k in range(2):
        pltpu.make_async_copy(y_hbm.at[pl.ds(0, tm)], ybuf.at[slot, k], sem.at[slot]).wait()
    r = r_ref[...]
    f = r[:, 2:3] * ybuf[slot, 0] + r[:, 3:4] * ybuf[slot, 1]
    y = x_ref[...] + mod_ref[0, 5:6, :] * f
    if final:
        ms = jnp.mean(y * y, axis=-1, keepdims=True)
        y = y * lax.rsqrt(ms + EPS) * refs[0][...]
    o_ref[...] = y


def _moe_combine(x, mod, route, pos, y_sorted, norm_final, *, rows_per_seg):
    t, d = x.shape
    tm = 256
    spt = rows_per_seg // tm
    final = norm_final is not None
    in_specs = [
        pl.BlockSpec((tm, d), lambda i, pos: (i, 0)),
        pl.BlockSpec((1, 6, d), lambda i, pos: (i // spt, 0, 0)),
        pl.BlockSpec((tm, LANES), lambda i, pos: (i, 0)),
        pl.BlockSpec(memory_space=pl.ANY),
    ]
    args = [x, mod, route, y_sorted]
    if final:
        in_specs.append(pl.BlockSpec((1, d), lambda i, pos: (0, 0)))
        args.append(norm_final.reshape(1, d))
    grid_spec = pltpu.PrefetchScalarGridSpec(
        num_scalar_prefetch=1,
        grid=(t // tm,),
        in_specs=in_specs,
        out_specs=pl.BlockSpec((tm, d), lambda i, pos: (i, 0)),
        scratch_shapes=[pltpu.VMEM((2, 2, tm, d), F32), pltpu.SemaphoreType.DMA((2,))],
    )
    return pl.pallas_call(
        functools.partial(_combine_kernel, final=final, tm=tm),
        grid_spec=grid_spec,
        out_shape=jax.ShapeDtypeStruct((t, d), F32),
        compiler_params=_cparams(("arbitrary",)),
        name="moe_combine",
    )(pos.reshape(-1), *args)


def _moe(x, g, mod, w_router, w1, w3, w2, norm_final, *, rows_per_seg):
    t, d = x.shape
    h, route = _router(x, g, mod, w_router, rows_per_seg=rows_per_seg)
    ids = route[:, :2].astype(jnp.int32)
    onehot = (ids[:, :, None] == jnp.arange(N_EXPERTS)[None, None, :]).astype(jnp.int32)
    sel = onehot.sum(axis=1)
    counts = sel.sum(axis=0)
    rank = jnp.cumsum(sel, axis=0) - sel
    padded = ((counts + MOE_TM - 1) // MOE_TM) * MOE_TM
    pstart = jnp.cumsum(padded) - padded
    cstart = jnp.cumsum(counts) - counts
    pos_e = pstart[None, :] + rank
    pos = jnp.take_along_axis(pos_e, ids, axis=1).astype(jnp.int32)
    ntile_max = (2 * t) // MOE_TM + N_EXPERTS + 1
    tok = jnp.arange(t, dtype=jnp.int32)
    key = jnp.sort((ids * t + tok[:, None]).reshape(-1))
    sorted_tok = key % t
    tile_start = jnp.arange(ntile_max, dtype=jnp.int32) * MOE_TM
    pend = pstart + padded
    tile_expert = jnp.minimum(jnp.sum(tile_start[:, None] >= pend[None, :], axis=1), N_EXPERTS - 1)
    n_tiles = (jnp.sum(padded) // MOE_TM).astype(jnp.int32).reshape(1)
    p = jnp.arange(ntile_max * MOE_TM, dtype=jnp.int32)
    e_p = jnp.repeat(tile_expert, MOE_TM)
    r_p = jnp.minimum(p - pstart[e_p], jnp.maximum(counts[e_p] - 1, 0))
    src = sorted_tok[jnp.clip(cstart[e_p] + r_p, 0, 2 * t - 1)].astype(jnp.int32)
    y_sorted = _moe_ffn(h, tile_expert.astype(jnp.int32), n_tiles, src, w1, w3, w2, ntile_max)
    return _moe_combine(x, mod, route, pos, y_sorted, norm_final, rows_per_seg=rows_per_seg)


def _layer(x, mod, l, depth, p, *, nb, seq, ctx, caches, state_out, norm_final):
    is_ctx = ctx is None
    seg = nb * seq if is_ctx else seq
    outs = _in_proj(x, p["norm_mix"][l], mod, p["w_in"][l], rows_per_seg=seg, seq=seq, layer=l,
                    depth=depth, caches=caches if is_ctx else None)
    zb, zf = outs[0], outs[1]
    new_caches = tuple(outs[2:]) if is_ctx else None
    if is_ctx:
        yft = _fourier_ctx(zb, nb, seq)
        o_n = _attn_ctx(zb, nb, seq)
    else:
        yft = _fourier_lat(zb, nb, seq)
        o_n = _na_latent(zb, ctx[0], ctx[1], p["na_rpb"][l], nb, seq)
    hg_c = 128
    lbr = 256 if is_ctx else 512
    lbr = min(lbr, seq)
    o_f, state_out = _hgrn(zb, zf, p["lbp"][l][0], None if is_ctx else ctx[2],
                           state_out if is_ctx else None, nb=nb, seq=seq, rev=False, layer=l,
                           depth=depth, c=hg_c, lb_rows=lbr)
    o_b, state_out = _hgrn(zb, zf, p["lbp"][l][1], None if is_ctx else ctx[3],
                           state_out if is_ctx else None, nb=nb, seq=seq, rev=True, layer=l,
                           depth=depth, c=hg_c, lb_rows=lbr)
    x = _merge(x, mod, zb, yft, o_f, o_b, o_n, p["hg_norm"][l], p["w_fo"][l], p["w_ho"][l],
               p["w_no"][l], p["w_out"][l], rows_per_seg=seg)
    i = l // 2
    if l % 2 == 0:
        x = _ffn_dense(x, p["norm_ffn"][l], mod, p["w1_d"][i], p["w3_d"][i], p["w2_d"][i], norm_final,
                       rows_per_seg=seg)
    else:
        x = _moe(x, p["norm_ffn"][l], mod, p["w_router"][i], p["w1_e"][i], p["w3_e"][i], p["w2_e"][i],
                 norm_final, rows_per_seg=seg)
    return x, new_caches, state_out


def kernel(x_prompt, x_sample, cache_k, cache_v, state_hgrn, c, c_ctx, w_mod, b_mod, norm_mix,
           norm_ffn, w_in, hg_lb, hg_norm, na_rpb, w_fo, w_ho, w_no, w_out, w1_d, w3_d, w2_d,
           w_router, w1_e, w3_e, w2_e, norm_final):
    nb_c, seq_c, d = x_prompt.shape
    nb_l, seq_l, _ = x_sample.shape
    depth = w_in.shape[0]
    past = cache_k.shape[2]

    lbs = jnp.cumsum(jax.nn.softmax(hg_lb.astype(F32), axis=0), axis=0)
    lbs = lbs - lbs[0:1]
    lbp = jnp.stack([jnp.log(lbs), jnp.log1p(-lbs)], axis=2)

    bf = lambda w: w.astype(BF16)
    p = dict(norm_mix=norm_mix, norm_ffn=norm_ffn, w_in=bf(w_in), hg_norm=hg_norm, na_rpb=na_rpb,
             w_fo=bf(w_fo), w_ho=bf(w_ho), w_no=bf(w_no), w_out=bf(w_out), w1_d=bf(w1_d),
             w3_d=bf(w3_d), w2_d=bf(w2_d), w_router=w_router, w1_e=bf(w1_e), w3_e=bf(w3_e),
             w2_e=bf(w2_e), lbp=lbp)

    mods = _mod_vectors(jnp.concatenate([c_ctx[None], c], axis=0), w_mod, b_mod)

    x = x_prompt.reshape(nb_c * seq_c, d)
    caches, state_out = (None, None), None
    for l in range(depth):
        x, caches, state_out = _layer(x, mods[l, 0:1], l, depth, p, nb=nb_c, seq=seq_c, ctx=None,
                                      caches=caches, state_out=state_out,
                                      norm_final=norm_final if l == depth - 1 else None)
    y_prompt = x.reshape(nb_c, seq_c, d)

    xs = x_sample.reshape(nb_l * seq_l, d)
    for l in range(depth):
        ctx = (bf(cache_k[:, l].reshape(nb_l, past, CH)), bf(cache_v[:, l].reshape(nb_l, past, CH)),
               state_hgrn[:, l, 0], state_hgrn[:, l, 1])
        xs, _, _ = _layer(xs, mods[l, 1:], l, depth, p, nb=nb_l, seq=seq_l, ctx=ctx, caches=None,
                          state_out=None, norm_final=norm_final if l == depth - 1 else None)
    y_sample = xs.reshape(nb_l, seq_l, d)

    new_cache_k = caches[0].reshape(nb_c, depth, seq_c, NA_HEADS, NA_HD)
    new_cache_v = caches[1].reshape(nb_c, depth, seq_c, NA_HEADS, NA_HD)
    return (y_prompt, y_sample, new_cache_k, new_cache_v, state_out)
```

```python
import functools
import math

import numpy as np
import jax
import jax.numpy as jnp
from jax import lax
from jax.experimental import pallas as pl
from jax.experimental.pallas import tpu as pltpu

F32 = jnp.float32
BF16 = jnp.bfloat16

LANES = 128
SUBLANES = 8
VMEM_LIMIT = 56 * 1024 * 1024
ROW_TILE = 512

FT_GROUPS = 4
FT_GD = 128
HG_HEADS = 4
HG_D = 128
NA_HEADS = 8
NA_HD = 64
GRID_W = 64
WIN_R = 8
WIN_C = 16
N_EXPERTS = 8
EPS = 1e-6
CH = 512
NEG = -1e30

ZB_SRC = (9, 10, 11, 12, 13, 14, 0, 1, 4, 5, 6, 7, 8)
ZB_UFT, ZB_HQ, ZB_HI, ZB_HG, ZB_NQ, ZB_NK, ZB_NV = 6, 7, 8, 9, 10, 11, 12
ZF_SRC = (2, 3)
SRC_NK, SRC_NV = 7, 8


def _cparams(sem):
    return pltpu.CompilerParams(dimension_semantics=sem, vmem_limit_bytes=VMEM_LIMIT)


def _const_spec(shape):
    nd = len(shape)
    return pl.BlockSpec(shape, lambda *_: (0,) * nd, pipeline_mode=pl.Buffered(1))


def _silu(x):
    return x * (1.0 / (1.0 + jnp.exp(-x)))


def _sigmoid(x):
    return 1.0 / (1.0 + jnp.exp(-x))


def _norm_mod(x, g, shift, scale):
    ms = jnp.mean(x * x, axis=-1, keepdims=True)
    return (x * lax.rsqrt(ms + EPS) * g) * (1.0 + scale) + shift


def _mod_kernel(c_ref, w_ref, b_ref, o_ref, *, nv, tn):
    w = w_ref[0]
    for v in range(nv):
        s = _silu(c_ref[v])
        s = jnp.concatenate([s] * (tn // LANES), axis=1)
        o_ref[0, v:v + 1, :] = jnp.sum(w * s, axis=0, keepdims=True) + b_ref[0]


def _mod_vectors(cvecs, w_mod, b_mod):
    nv, d = cvecs.shape
    depth, _, n6 = w_mod.shape
    tn = 512
    cb = jnp.broadcast_to(cvecs[:, :, None], (nv, d, LANES))
    out = pl.pallas_call(
        functools.partial(_mod_kernel, nv=nv, tn=tn),
        grid=(depth, n6 // tn),
        in_specs=[
            pl.BlockSpec((nv, d, LANES), lambda l, j: (0, 0, 0)),
            pl.BlockSpec((1, d, tn), lambda l, j: (l, 0, j)),
            pl.BlockSpec((1, 1, tn), lambda l, j: (l, 0, j)),
        ],
        out_specs=pl.BlockSpec((1, nv, tn), lambda l, j: (l, 0, j)),
        out_shape=jax.ShapeDtypeStruct((depth, nv, n6), F32),
        compiler_params=_cparams(("parallel", "parallel")),
        name="mod_vectors",
    )(cb, w_mod, b_mod.reshape(depth, 1, n6))
    return out.reshape(depth, nv, 6, d)


def _in_kernel(x_ref, g_ref, mod_ref, w_ref, *refs, with_cache):
    if with_cache:
        zb_ref, zf_ref, ck_ref, cv_ref = refs[-4:]
    else:
        zb_ref, zf_ref = refs[-2:]
    h = _norm_mod(x_ref[...], g_ref[...], mod_ref[0, 0:1, :], mod_ref[0, 1:2, :]).astype(BF16)
    for src in range(15):
        z = jnp.dot(h, w_ref[:, src * CH:(src + 1) * CH], preferred_element_type=F32)
        if src in ZB_SRC:
            j = ZB_SRC.index(src)
            zb_ref[:, j * CH:(j + 1) * CH] = z.astype(BF16)
        if src in ZF_SRC:
            j = ZF_SRC.index(src)
            zf_ref[:, j * CH:(j + 1) * CH] = z
        if with_cache and src in (SRC_NK, SRC_NV):
            c_ref = ck_ref if src == SRC_NK else cv_ref
            nseq, _, seq, _ = c_ref.shape
            for s in range(nseq):
                c_ref[s, 0] = z[s * seq:(s + 1) * seq]


def _in_proj(x, g, mod, w_bf, *, rows_per_seg, seq, layer, depth, caches):
    t, d = x.shape
    tm = ROW_TILE
    with_cache = caches is not None
    spt = rows_per_seg // tm
    in_specs = [
        pl.BlockSpec((tm, d), lambda i: (i, 0)),
        _const_spec((1, d)),
        pl.BlockSpec((1, 6, d), lambda i: (i // spt, 0, 0)),
        _const_spec((d, 15 * CH)),
    ]
    args = [x, g.reshape(1, d), mod, w_bf]
    out_specs = [
        pl.BlockSpec((tm, len(ZB_SRC) * CH), lambda i: (i, 0)),
        pl.BlockSpec((tm, len(ZF_SRC) * CH), lambda i: (i, 0)),
    ]
    out_shape = [
        jax.ShapeDtypeStruct((t, len(ZB_SRC) * CH), BF16),
        jax.ShapeDtypeStruct((t, len(ZF_SRC) * CH), F32),
    ]
    aliases = {}
    if with_cache:
        assert tm % seq == 0, "a row tile holds whole context sequences"
        cspec = pl.BlockSpec((tm // seq, 1, seq, CH), lambda i: (i, layer, 0, 0))
        out_specs += [cspec, cspec]
        cshape = jax.ShapeDtypeStruct((t // seq, depth, seq, CH), F32)
        out_shape += [cshape, cshape]
        if caches[0] is not None:
            in_specs += [pl.BlockSpec(memory_space=pl.ANY)] * 2
            args += list(caches)
            aliases = {4: 2, 5: 3}
    return pl.pallas_call(
        functools.partial(_in_kernel, with_cache=with_cache),
        grid=(t // tm,),
        in_specs=in_specs,
        out_specs=out_specs,
        out_shape=out_shape,
        input_output_aliases=aliases,
        compiler_params=_cparams(("parallel",)),
        name="in_proj",
    )(*args)


def _dft_tables(n, scale):
    k = np.arange(n)
    ang = 2.0 * np.pi * ((k[:, None] * k[None, :]) % n) / n
    return np.cos(ang) * scale, np.sin(ang) * scale


def _ft_ctx_kernel(u_ref, cs_ref, lm_ref, y_ref):
    u = u_ref[...]
    cs = cs_ref[...].astype(BF16)
    zc, zs = [], []
    for g in range(FT_GROUPS):
        z = jnp.dot(u[:, g * FT_GD:(g + 1) * FT_GD], cs, preferred_element_type=F32)
        zc.append(z[:, :FT_GD])
        zs.append(z[:, FT_GD:])
    zz = jnp.concatenate([jnp.concatenate(zc, axis=1), jnp.concatenate(zs, axis=1)], axis=0)
    y = jnp.dot(lm_ref[...].astype(BF16), zz.astype(BF16), preferred_element_type=F32)
    for g in range(FT_GROUPS):
        y_ref[g] = y[:, g * FT_GD:(g + 1) * FT_GD]


def _fourier_ctx(zb, nb, seq):
    cc, sc = _dft_tables(FT_GD, FT_GD ** -0.5)
    cl, sl = _dft_tables(seq, seq ** -0.5)
    cs = jnp.asarray(np.concatenate([cc, sc], axis=1), F32)
    lm = jnp.asarray(np.concatenate([cl, -sl], axis=1), F32)
    return pl.pallas_call(
        _ft_ctx_kernel,
        grid=(nb,),
        in_specs=[
            pl.BlockSpec((seq, CH), lambda b: (b, ZB_UFT)),
            _const_spec(cs.shape),
            _const_spec(lm.shape),
        ],
        out_specs=pl.BlockSpec((FT_GROUPS, seq, FT_GD), lambda b: (0, b, 0)),
        out_shape=jax.ShapeDtypeStruct((FT_GROUPS, nb * seq, FT_GD), F32),
        compiler_params=_cparams(("parallel",)),
        name="fourier_ctx",
    )(zb, cs, lm)


def _ft_lat_kernel(u_ref, cs_ref, w1_ref, twc_ref, tws_ref, w2_ref, y_ref, zr_ref, zi_ref,
                   ar_ref, ai_ref, *, n, lw):
    rows = n * n
    rc = min(rows, 512)
    ng = lw // FT_GD
    cs = cs_ref[...].astype(BF16)
    w1 = w1_ref[...].astype(BF16)
    w2 = w2_ref[...].astype(BF16)
    for r0 in range(0, rows, rc):
        for g in range(ng):
            z = jnp.dot(u_ref[r0:r0 + rc, g * FT_GD:(g + 1) * FT_GD], cs,
                        preferred_element_type=F32)
            zr_ref[g, r0:r0 + rc, :] = z[:, :FT_GD]
            zi_ref[g, r0:r0 + rc, :] = -z[:, FT_GD:]

    def gather(re_ref, im_ref, idx):
        xr = jnp.concatenate([re_ref[g, idx, :] for g in range(ng)], axis=1)
        xi = jnp.concatenate([im_ref[g, idx, :] for g in range(ng)], axis=1)
        return jnp.concatenate([xr, xi], axis=0).astype(BF16)

    def stage1(a, carry):
        idx = pl.ds(a, n, stride=n)
        y = jnp.dot(w1, gather(zr_ref, zi_ref, idx), preferred_element_type=F32)
        yr, yi = y[:n], y[n:]
        tc = jnp.concatenate([twc_ref[a]] * ng, axis=1)
        ts = jnp.concatenate([tws_ref[a]] * ng, axis=1)
        zr = yr * tc - yi * ts
        zi = yr * ts + yi * tc
        for g in range(ng):
            ar_ref[g, idx, :] = zr[:, g * FT_GD:(g + 1) * FT_GD]
            ai_ref[g, idx, :] = zi[:, g * FT_GD:(g + 1) * FT_GD]
        return carry

    lax.fori_loop(0, n, stage1, 0, unroll=4)

    def stage2(q, carry):
        r0 = pl.multiple_of(q * n, n)
        y = jnp.dot(w2, gather(ar_ref, ai_ref, pl.ds(r0, n)), preferred_element_type=F32)
        for g in range(ng):
            y_ref[g, pl.ds(q, n, stride=n), :] = y[:, g * FT_GD:(g + 1) * FT_GD]
        return carry

    lax.fori_loop(0, n, stage2, 0, unroll=4)


def _fourier_lat(zb, nb, seq):
    n = math.isqrt(seq)
    assert n * n == seq and n % SUBLANES == 0
    lw = 256
    cc, sc = _dft_tables(FT_GD, FT_GD ** -0.5)
    cs = jnp.asarray(np.concatenate([cc, sc], axis=1), F32)
    pr, ps = _dft_tables(n, 1.0 / n)
    w1 = jnp.asarray(np.block([[pr, ps], [-ps, pr]]), F32)
    k = np.arange(n)
    ang = 2.0 * np.pi * (k[:, None] * k[None, :]) / seq
    twc = jnp.asarray(np.broadcast_to(np.cos(ang)[:, :, None], (n, n, LANES)), F32)
    tws = jnp.asarray(np.broadcast_to(-np.sin(ang)[:, :, None], (n, n, LANES)), F32)
    wr, ws = _dft_tables(n, 1.0)
    w2 = jnp.asarray(np.concatenate([wr, ws], axis=1), F32)
    nh = CH // lw
    return pl.pallas_call(
        functools.partial(_ft_lat_kernel, n=n, lw=lw),
        grid=(nb, nh),
        in_specs=[
            pl.BlockSpec((seq, lw), lambda b, h: (b, ZB_UFT * nh + h)),
            _const_spec(cs.shape),
            _const_spec(w1.shape),
            _const_spec(twc.shape),
            _const_spec(tws.shape),
            _const_spec(w2.shape),
        ],
        out_specs=pl.BlockSpec((lw // FT_GD, seq, FT_GD), lambda b, h: (h, b, 0)),
        out_shape=jax.ShapeDtypeStruct((FT_GROUPS, nb * seq, FT_GD), F32),
        scratch_shapes=[pltpu.VMEM((lw // FT_GD, seq, FT_GD), F32)] * 4,
        compiler_params=_cparams(("parallel", "parallel")),
        name="fourier_lat",
    )(zb, cs, w1, twc, tws, w2)


HG_SAFE = 160.0


def _hgrn_tables(c, rev):
    t = np.arange(c)
    tri = (t[None, :] >= t[:, None]) if rev else (t[None, :] <= t[:, None])
    lv = np.full((c, c), -1, np.int32)
    n, i = 1, 0
    while n < c:
        same = (t[:, None] // (2 * n)) == (t[None, :] // (2 * n))
        hi_t, hi_s = (t[:, None] % (2 * n)) >= n, (t[None, :] % (2 * n)) >= n
        pair = same & (~hi_t & hi_s if rev else hi_t & ~hi_s)
        lv[pair] = i
        n, i = 2 * n, i + 1
    return tri.astype(np.float32), lv


def _split3(x):
    hi = x.astype(BF16)
    r = x - hi.astype(F32)
    mid = r.astype(BF16)
    lo = (r - mid.astype(F32)).astype(BF16)
    return hi, mid, lo


def _hgrn_kernel(hq_ref, hf_ref, hi_ref, lb_ref, tri_ref, lv_ref, *refs, c, nch, rev, has_s0):
    if has_s0:
        s0_ref, o_ref, st_ref, b_ref, k_ref, q_ref = refs
    else:
        o_ref, so_ref, st_ref, b_ref, k_ref, q_ref = refs
    j = pl.program_id(1)

    @pl.when(j == 0)
    def _():
        for h in range(HG_HEADS):
            st_ref[h] = s0_ref[0, h].T if has_s0 else jnp.zeros((HG_D, HG_D), F32)

    la = lb_ref[0:1, :]
    l1 = lb_ref[1:2, :]
    lv = lv_ref[...]
    row8 = lax.broadcasted_iota(jnp.int32, (c // SUBLANES, SUBLANES, HG_D), 1)
    ti = lax.broadcasted_iota(jnp.int32, (c, c), 0)
    si = lax.broadcasted_iota(jnp.int32, (c, c), 1)
    causal = (si >= ti) if rev else (si <= ti)
    bl_row = c - 1 if not rev else 0

    tmax = jnp.float32(0.0)
    for cc in range(nch):
        rows = slice(cc * c, (cc + 1) * c)
        z = hf_ref[rows, :]
        ls = jnp.minimum(z, 0.0) - jnp.log1p(jnp.exp(-jnp.abs(z)))
        cc_ = l1 + ls
        logf = jnp.maximum(la, cc_) + jnp.log1p(jnp.exp(-jnp.abs(la - cc_)))
        k_ref[rows, :] = 1.0 - jnp.exp(logf)
        b = sum(jnp.dot(tri_ref[...], p, preferred_element_type=F32) for p in _split3(logf))
        b_ref[rows, :] = b
        q_ref[rows, :] = _silu(hq_ref[rows, :].astype(F32))
        tmax = jnp.maximum(tmax, jnp.max(jnp.abs(b[bl_row:bl_row + 1, :])))
    safe = tmax <= HG_SAFE

    def chunk(cc, fast):
        rows = slice(cc * c, (cc + 1) * c)
        b = b_ref[rows, :]
        kk = k_ref[rows, :]
        q = q_ref[rows, :]
        v = hi_ref[rows, :]
        blast = b_ref[cc * c + bl_row:cc * c + bl_row + 1, :]
        brefs = {}
        n = SUBLANES
        while not fast and n < c:
            parts = []
            for m in range(c // (2 * n)):
                r = cc * c + 2 * n * m + (n if rev else n - 1)
                parts.append(jnp.broadcast_to(b_ref[r:r + 1, :], (2 * n, HG_HEADS * HG_D)))
            brefs[n] = jnp.concatenate(parts, axis=0) if len(parts) > 1 else parts[0]
            n *= 2
        outs = []
        for h in range(HG_HEADS):
            hs = slice(h * HG_D, (h + 1) * HG_D)
            qh, kh, bh, vh, blh = q[:, hs], kk[:, hs], b[:, hs], v[:, hs], blast[:, hs]
            st = st_ref[h]
            if fast:
                cm = 0.5 * blh
                qn = qh * jnp.exp(bh - cm)
                kn = kh * jnp.exp(cm - bh)
                sn = lax.dot_general(qn.astype(BF16), kn.astype(BF16), (((1,), (1,)), ((), ())),
                                     preferred_element_type=F32)
                a = jnp.where(causal, sn, 0.0)
                qe = (qn * jnp.exp(cm)).astype(BF16)
                ke = (kn * jnp.exp(blh - cm)).astype(BF16)
                o = lax.dot_general(qe, st.astype(BF16), (((1,), (1,)), ((), ())),
                                    preferred_element_type=F32)
                o = o + jnp.dot(a.astype(BF16), vh, preferred_element_type=F32)
            else:
                o, ke = _hgrn_levels(qh, kh, bh, vh, blh, st, brefs, hs, lv, row8, c, rev)
            outs.append(o)
            kv = lax.dot_general(vh, ke, (((0,), (0,)), ((), ())), preferred_element_type=F32)
            st_ref[h] = st * jnp.exp(blh) + kv
        o_ref[rows, :] = jnp.concatenate(outs, axis=1)

    order = [nch - 1 - ci if rev else ci for ci in range(nch)]

    @pl.when(safe)
    def _():
        for cc in order:
            chunk(cc, True)

    @pl.when(jnp.logical_not(safe))
    def _():
        for cc in order:
            chunk(cc, False)

    if not has_s0:
        @pl.when(j == pl.num_programs(1) - 1)
        def _():
            for h in range(HG_HEADS):
                so_ref[0, 0, 0, h] = st_ref[h].T


def _hgrn_levels(qh, kh, bh, vh, blh, st, brefs, hs, lv, row8, c, rev):
    qe = (qh * jnp.exp(bh)).astype(BF16)
    o = lax.dot_general(qe, st.astype(BF16), (((1,), (1,)), ((), ())), preferred_element_type=F32)
    b3 = bh.reshape(c // SUBLANES, SUBLANES, HG_D)
    a = jnp.zeros((c, c), F32)
    n, i = 1, 0
    while n < c:
        if n < SUBLANES:
            bref = None
            for g in range(SUBLANES // (2 * n)):
                r = 2 * n * g + (n if rev else n - 1)
                bc = jnp.broadcast_to(b3[:, r:r + 1, :], b3.shape)
                bref = bc if bref is None else jnp.where(row8 >= 2 * n * g, bc, bref)
            bref = bref.reshape(c, HG_D)
        else:
            bref = brefs[n][:, hs]
        e = jnp.exp(-jnp.abs(bh - bref))
        sn = lax.dot_general((qh * e).astype(BF16), (kh * e).astype(BF16),
                             (((1,), (1,)), ((), ())), preferred_element_type=F32)
        a = jnp.where(lv == i, sn, a)
        n, i = 2 * n, i + 1
    o = o + jnp.dot(a.astype(BF16), vh, preferred_element_type=F32)
    o = o + jnp.sum(qh * kh, axis=-1, keepdims=True) * vh.astype(F32)
    return o, (kh * jnp.exp(blh - bh)).astype(BF16)


def _hgrn(zb, zf, lbp, s0, state_out, *, nb, seq, rev, layer, depth, c, lb_rows):
    t = zb.shape[0]
    nblk = seq // lb_rows
    nch = lb_rows // c
    tri, lv = _hgrn_tables(c, rev)
    d = int(rev)

    def rowmap(b, j):
        return b * nblk + (nblk - 1 - j if rev else j)

    in_specs = [
        pl.BlockSpec((lb_rows, CH), lambda b, j: (rowmap(b, j), ZB_HQ)),
        pl.BlockSpec((lb_rows, CH), lambda b, j: (rowmap(b, j), d)),
        pl.BlockSpec((lb_rows, CH), lambda b, j: (rowmap(b, j), ZB_HI)),
        _const_spec((2, CH)),
        _const_spec((c, c)),
        _const_spec((c, c)),
    ]
    args = [zb, zf, zb, lbp, jnp.asarray(tri, BF16), jnp.asarray(lv)]
    has_s0 = s0 is not None
    if has_s0:
        in_specs.append(pl.BlockSpec((1, HG_HEADS, HG_D, HG_D), lambda b, j: (b, 0, 0, 0)))
        args.append(s0)
    sshape = (nb, depth, 2, HG_HEADS, HG_D, HG_D)
    aliases = {}
    if state_out is not None:
        in_specs.append(pl.BlockSpec(memory_space=pl.ANY))
        args.append(state_out)
        aliases = {len(args) - 1: 1}
    kern = functools.partial(_hgrn_kernel, c=c, nch=nch, rev=rev, has_s0=has_s0)
    if state_out is not None:
        kern = functools.partial(_drop_ref, kern, len(args) - 1)
    out_specs = [pl.BlockSpec((lb_rows, CH), lambda b, j: (rowmap(b, j), 0))]
    out_shape = [jax.ShapeDtypeStruct((t, CH), F32)]
    if not has_s0:
        out_specs.append(pl.BlockSpec((1, 1, 1, HG_HEADS, HG_D, HG_D), lambda b, j: (b, layer, d, 0, 0, 0)))
        out_shape.append(jax.ShapeDtypeStruct(sshape, F32))
    outs = pl.pallas_call(
        kern,
        grid=(nb, nblk),
        in_specs=in_specs,
        out_specs=out_specs,
        out_shape=out_shape,
        scratch_shapes=[pltpu.VMEM((HG_HEADS, HG_D, HG_D), F32)] + [pltpu.VMEM((lb_rows, CH), F32)] * 3,
        input_output_aliases=aliases,
        compiler_params=_cparams(("parallel", "arbitrary")),
        name="hgrn_bwd" if rev else "hgrn_fwd",
    )(*args)
    return (outs[0], None) if has_s0 else tuple(outs)


def _drop_ref(kern, pos, *refs):
    return kern(*refs[:pos], *refs[pos + 1:])


def _head_mask(a):
    lane = lax.broadcasted_iota(jnp.int32, (1, 2 * NA_HD), 1)
    return (lane // NA_HD) == a


def _qk(q, k):
    return lax.dot_general(q, k, (((1,), (1,)), ((), ())), preferred_element_type=F32)


def _attn_ctx_kernel(q_ref, k_ref, v_ref, o_ref):
    w = 2 * NA_HD
    for pr in range(NA_HEADS // 2):
        ps = slice(pr * w, (pr + 1) * w)
        q = q_ref[:, ps] * (NA_HD ** -0.5)
        k = k_ref[:, ps]
        v = v_ref[:, ps]
        outs = []
        for a in range(2):
            s = _qk(jnp.where(_head_mask(a), q, jnp.zeros_like(q)), k)
            p = jnp.exp(s - jnp.max(s, axis=-1, keepdims=True))
            l = jnp.sum(p, axis=-1, keepdims=True)
            outs.append(jnp.dot(p.astype(BF16), v, preferred_element_type=F32) / l)
        o_ref[:, ps] = jnp.where(_head_mask(0), outs[0], outs[1]).astype(BF16)


def _attn_ctx(zb, nb, seq):
    return pl.pallas_call(
        _attn_ctx_kernel,
        grid=(nb,),
        in_specs=[
            pl.BlockSpec((seq, CH), lambda b: (b, ZB_NQ)),
            pl.BlockSpec((seq, CH), lambda b: (b, ZB_NK)),
            pl.BlockSpec((seq, CH), lambda b: (b, ZB_NV)),
        ],
        out_specs=pl.BlockSpec((seq, CH), lambda b: (b, 0)),
        out_shape=jax.ShapeDtypeStruct((nb * seq, CH), BF16),
        compiler_params=_cparams(("parallel",)),
        name="attn_ctx",
    )(zb, zb, zb)


NA_TR = 8
NA_KW = 10


def _na_geometry(rows):
    kr = min(WIN_R, rows)
    span = min(NA_TR + WIN_R, rows)
    ntile = rows // NA_TR
    pats, var, starts = [], [], []
    for t in range(ntile):
        r0 = t * NA_TR
        start = int(np.clip(r0 - kr // 2, 0, rows - span))
        qr = r0 + np.arange(NA_TR)
        row_start = np.clip(qr - kr // 2, 0, rows - kr)
        krow = start + np.arange(span)
        rv = (krow[None, :] >= row_start[:, None]) & (krow[None, :] < row_start[:, None] + kr)
        d_row = np.where(rv, krow[None, :] - qr[:, None] + WIN_R - 1, -1)
        for vi, p in enumerate(pats):
            if np.array_equal(p, d_row):
                var.append(vi)
                break
        else:
            var.append(len(pats))
            pats.append(d_row)
        starts.append(start)
    kw = min(NA_KW, span)
    koff = []
    for d_row in pats:
        for m in range(NA_TR // 2):
            cols = np.nonzero((d_row[2 * m:2 * m + 2] >= 0).any(axis=0))[0]
            ks = min(int(cols.min()) // 2 * 2, span - kw)
            assert cols.max() < ks + kw
            koff.append(ks)
    return pats, np.asarray(var, np.int32), np.asarray(starts, np.int32), np.asarray(koff, np.int32), kw


def _na_col_tables(rpb):
    qc = np.arange(GRID_W)
    col_start = np.clip(qc - WIN_C // 2, 0, GRID_W - WIN_C)
    cv = (qc[None, :] >= col_start[:, None]) & (qc[None, :] < col_start[:, None] + WIN_C)
    d_col = qc[None, :] - qc[:, None] + WIN_C - 1
    onehot = (d_col[None] == np.arange(2 * WIN_C - 1)[:, None, None]) & cv[None]
    t = jnp.einsum("hrd,dqk->hrqk", rpb, jnp.asarray(onehot, F32), precision=lax.Precision.HIGHEST)
    t = jnp.where(jnp.asarray(cv)[None, None], t, NEG)
    return jnp.concatenate([t, t], axis=-1)


def _na_kernel(var_ref, start_ref, koff_ref, q_ref, k_ref, v_ref, kc_ref, vc_ref, tab_ref, o_ref,
               bias_ref, *, kw, pats, koff):
    t = pl.program_id(2)
    var = var_ref[t]
    changed = jnp.logical_or(t == 0, var != var_ref[jnp.maximum(t - 1, 0)])
    left = lax.broadcasted_iota(jnp.int32, (GRID_W, 2 * GRID_W), 1) < GRID_W
    neg = jnp.full((GRID_W, 2 * GRID_W), NEG, F32)
    npair = NA_TR // 2

    for vi, d_row in enumerate(pats):
        @pl.when(jnp.logical_and(changed, var == vi))
        def _(vi=vi, d_row=d_row):
            for a in range(2):
                for i in range(NA_TR):
                    m = i // 2
                    ks = int(koff[vi * npair + m])
                    for j in range(ks, ks + kw, 2):
                        r0, r1 = int(d_row[i, j]), int(d_row[i, j + 1])
                        b0 = tab_ref[a, r0] if r0 >= 0 else neg
                        b1 = tab_ref[a, r1] if r1 >= 0 else neg
                        blk = neg if (r0 < 0 and r1 < 0) else jnp.where(left, b0, b1)
                        bias_ref[a, m, (i % 2) * GRID_W:(i % 2 + 1) * GRID_W,
                                 (j - ks) * GRID_W:(j - ks + 2) * GRID_W] = blk

    kc = kc_ref[0]
    vc = vc_ref[0]
    for m in range(npair):
        qs = slice(2 * m * GRID_W, (2 * m + 2) * GRID_W)
        st = pl.multiple_of((start_ref[t] + koff_ref[var * npair + m]) * GRID_W, 2 * GRID_W)
        q = q_ref[qs, :] * (NA_HD ** -0.5)
        kwin = k_ref[pl.ds(st, kw * GRID_W), :]
        vwin = v_ref[pl.ds(st, kw * GRID_W), :]
        outs = []
        for a in range(2):
            qa = jnp.where(_head_mask(a), q, jnp.zeros_like(q))
            s1 = _qk(qa, kwin) + bias_ref[a, m]
            s2 = _qk(qa, kc)
            mx = jnp.maximum(jnp.max(s1, axis=-1, keepdims=True), jnp.max(s2, axis=-1, keepdims=True))
            p1 = jnp.exp(s1 - mx)
            p2 = jnp.exp(s2 - mx)
            l = jnp.sum(p1, axis=-1, keepdims=True) + jnp.sum(p2, axis=-1, keepdims=True)
            o = (jnp.dot(p1.astype(BF16), vwin, preferred_element_type=F32)
                 + jnp.dot(p2.astype(BF16), vc, preferred_element_type=F32))
            outs.append(o / l)
        o_ref[qs, :] = jnp.where(_head_mask(0), outs[0], outs[1]).astype(BF16)


def _na_latent(zb, kctx, vctx, rpb, nb, seq):
    rows = seq // GRID_W
    pats, var, starts, koff, kw = _na_geometry(rows)
    tq = NA_TR * GRID_W
    ntile = rows // NA_TR
    hp = NA_HEADS // 2
    w = 2 * NA_HD
    past = kctx.shape[1]
    assert kw % 2 == 0
    tabs = _na_col_tables(rpb)
    nrow = tabs.shape[1]
    grid_spec = pltpu.PrefetchScalarGridSpec(
        num_scalar_prefetch=3,
        grid=(nb, hp, ntile),
        in_specs=[
            pl.BlockSpec((tq, w), lambda b, p, t, *_: (b * ntile + t, ZB_NQ * hp + p)),
            pl.BlockSpec((seq, w), lambda b, p, t, *_: (b, ZB_NK * hp + p)),
            pl.BlockSpec((seq, w), lambda b, p, t, *_: (b, ZB_NV * hp + p)),
            pl.BlockSpec((1, past, w), lambda b, p, t, *_: (b, 0, p)),
            pl.BlockSpec((1, past, w), lambda b, p, t, *_: (b, 0, p)),
            pl.BlockSpec((2, nrow, GRID_W, 2 * GRID_W), lambda b, p, t, *_: (p, 0, 0, 0)),
        ],
        out_specs=pl.BlockSpec((tq, w), lambda b, p, t, *_: (b * ntile + t, p)),
        scratch_shapes=[pltpu.VMEM((2, NA_TR // 2, 2 * GRID_W, kw * GRID_W), F32)],
    )
    return pl.pallas_call(
        functools.partial(_na_kernel, kw=kw, pats=pats, koff=koff),
        grid_spec=grid_spec,
        out_shape=jax.ShapeDtypeStruct((nb * seq, CH), BF16),
        compiler_params=_cparams(("parallel", "parallel", "arbitrary")),
        name="na_latent",
    )(jnp.asarray(var), jnp.asarray(starts), jnp.asarray(koff), zb, zb, zb, kctx, vctx, tabs)


def _merge_kernel(x_ref, mod_ref, gate_ref, yft_ref, of_ref, ob_ref, hg_ref, on_ref, hn_ref,
                  wfo_ref, who_ref, wno_ref, wout_ref, o_ref):
    d = x_ref.shape[1]
    yft = jnp.concatenate([yft_ref[g] for g in range(FT_GROUPS)], axis=1)
    y_ft = jnp.dot(yft.astype(BF16), wfo_ref[...], preferred_element_type=F32)
    o = of_ref[...] + ob_ref[...]
    parts = []
    for h in range(HG_HEADS):
        oh = o[:, h * HG_D:(h + 1) * HG_D]
        ms = jnp.mean(oh * oh, axis=-1, keepdims=True)
        parts.append(oh * lax.rsqrt(ms + EPS))
    oh = jnp.concatenate(parts, axis=1) * hn_ref[...]
    oh = (oh * _silu(hg_ref[...].astype(F32))).astype(BF16)
    y_hg = jnp.dot(oh, who_ref[...], preferred_element_type=F32)
    y_na = jnp.dot(on_ref[...], wno_ref[...], preferred_element_type=F32)
    g = gate_ref[...].astype(F32)
    merged = (_sigmoid(g[:, :d]) * y_ft + _sigmoid(g[:, d:2 * d]) * y_hg
              + _sigmoid(g[:, 2 * d:]) * y_na)
    y = jnp.dot(merged.astype(BF16), wout_ref[...], preferred_element_type=F32)
    o_ref[...] = x_ref[...] + mod_ref[0, 2:3, :] * y


def _merge(x, mod, zb, yft, o_f, o_b, o_n, hg_norm, w_fo, w_ho, w_no, w_out, *, rows_per_seg):
    t, d = x.shape
    tm = ROW_TILE
    spt = rows_per_seg // tm
    assert 3 * d == 6 * CH, "the three merge gates fill the first six bf16 splits"
    row = lambda i: (i, 0)
    return pl.pallas_call(
        _merge_kernel,
        grid=(t // tm,),
        in_specs=[
            pl.BlockSpec((tm, d), row),
            pl.BlockSpec((1, 6, d), lambda i: (i // spt, 0, 0)),
            pl.BlockSpec((tm, 3 * d), row),
            pl.BlockSpec((FT_GROUPS, tm, FT_GD), lambda i: (0, i, 0)),
            pl.BlockSpec((tm, CH), row),
            pl.BlockSpec((tm, CH), row),
            pl.BlockSpec((tm, CH), lambda i: (i, ZB_HG)),
            pl.BlockSpec((tm, CH), row),
            _const_spec((1, CH)),
            _const_spec((CH, d)),
            _const_spec((CH, d)),
            _const_spec((CH, d)),
            _const_spec((d, d)),
        ],
        out_specs=pl.BlockSpec((tm, d), row),
        out_shape=jax.ShapeDtypeStruct((t, d), F32),
        compiler_params=_cparams(("parallel",)),
        name="merge",
    )(x, mod, zb, yft, o_f, o_b, zb, o_n, hg_norm.reshape(1, CH), w_fo, w_ho, w_no, w_out)


FF_CHUNK = 256


def _swiglu_acc(h, w1_ref, w3_ref, w2_ref, lead=()):
    dff = w1_ref.shape[-1]
    acc = None
    for c0 in range(0, dff, FF_CHUNK):
        cs = slice(c0, c0 + FF_CHUNK)
        g = jnp.dot(h, w1_ref[lead + (slice(None), cs)], preferred_element_type=F32)
        u = jnp.dot(h, w3_ref[lead + (slice(None), cs)], preferred_element_type=F32)
        a = (_silu(g) * u).astype(BF16)
        y = jnp.dot(a, w2_ref[lead + (cs, slice(None))], preferred_element_type=F32)
        acc = y if acc is None else acc + y
    return acc


def _ffn_kernel(x_ref, g_ref, mod_ref, w1_ref, w3_ref, w2_ref, *refs, final):
    o_ref = refs[-1]
    x = x_ref[...]
    h = _norm_mod(x, g_ref[...], mod_ref[0, 3:4, :], mod_ref[0, 4:5, :]).astype(BF16)
    y = x + mod_ref[0, 5:6, :] * _swiglu_acc(h, w1_ref, w3_ref, w2_ref)
    if final:
        ms = jnp.mean(y * y, axis=-1, keepdims=True)
        y = y * lax.rsqrt(ms + EPS) * refs[0][...]
    o_ref[...] = y


def _ffn_dense(x, g, mod, w1, w3, w2, norm_final, *, rows_per_seg):
    t, d = x.shape
    dff = w1.shape[1]
    tm = ROW_TILE
    spt = rows_per_seg // tm
    final = norm_final is not None
    in_specs = [
        pl.BlockSpec((tm, d), lambda i: (i, 0)),
        _const_spec((1, d)),
        pl.BlockSpec((1, 6, d), lambda i: (i // spt, 0, 0)),
        _const_spec((d, dff)),
        _const_spec((d, dff)),
        _const_spec((dff, d)),
    ]
    args = [x, g.reshape(1, d), mod, w1, w3, w2]
    if final:
        in_specs.append(_const_spec((1, d)))
        args.append(norm_final.reshape(1, d))
    return pl.pallas_call(
        functools.partial(_ffn_kernel, final=final),
        grid=(t // tm,),
        in_specs=in_specs,
        out_specs=pl.BlockSpec((tm, d), lambda i: (i, 0)),
        out_shape=jax.ShapeDtypeStruct((t, d), F32),
        compiler_params=_cparams(("parallel",)),
        name="ffn_dense",
    )(*args)


def _router_kernel(x_ref, g_ref, mod_ref, wr_ref, h_ref, r_ref):
    h = _norm_mod(x_ref[...], g_ref[...], mod_ref[0, 3:4, :], mod_ref[0, 4:5, :])
    h_ref[...] = h
    logits = jnp.dot(h, wr_ref[...], precision=lax.Precision.HIGHEST,
                     preferred_element_type=F32)
    lane = lax.broadcasted_iota(jnp.int32, logits.shape, 1)
    logits = jnp.where(lane < N_EXPERTS, logits, -jnp.inf)
    m1 = jnp.max(logits, axis=-1, keepdims=True)
    i1 = jnp.min(jnp.where(logits == m1, lane, LANES), axis=-1, keepdims=True)
    rest = jnp.where(lane == i1, -jnp.inf, logits)
    m2 = jnp.max(rest, axis=-1, keepdims=True)
    i2 = jnp.min(jnp.where(rest == m2, lane, LANES), axis=-1, keepdims=True)
    e2 = jnp.exp(m2 - m1)
    w1 = 1.0 / (1.0 + e2)
    w2 = e2 / (1.0 + e2)
    r = jnp.where(lane == 0, i1.astype(F32), jnp.where(lane == 1, i2.astype(F32),
                  jnp.where(lane == 2, w1, jnp.where(lane == 3, w2, 0.0))))
    r_ref[...] = r


def _router(x, g, mod, w_router, *, rows_per_seg):
    t, d = x.shape
    tm = ROW_TILE
    spt = rows_per_seg // tm
    wr =jnp.zeros((d, LANES), F32).at[:, :N_EXPERTS].set(w_router)
    return pl.pallas_call(
        _router_kernel,
        grid=(t // tm,),
        in_specs=[
            pl.BlockSpec((tm, d), lambda i: (i, 0)),
            _const_spec((1, d)),
            pl.BlockSpec((1, 6, d), lambda i: (i // spt, 0, 0)),
            _const_spec((d, LANES)),
        ],
        out_specs=[pl.BlockSpec((tm, d), lambda i: (i, 0)), pl.BlockSpec((tm, LANES), lambda i: (i, 0))],
        out_shape=[jax.ShapeDtypeStruct((t, d), F32), jax.ShapeDtypeStruct((t, LANES), F32)],
        compiler_params=_cparams(("parallel",)),
        name="router",
    )(x, g.reshape(1, d), mod, wr)


MOE_TM = 512


def _moe_ffn_kernel(te_ref, nt_ref, src_ref, h_hbm, w1_ref, w3_ref, w2_ref, y_ref, hbuf, sem):
    i = pl.program_id(0)
    nt = nt_ref[0]
    slot = i % 2

    def gather(tile, s):
        base = tile * MOE_TM

        def issue(r, carry):
            pltpu.make_async_copy(h_hbm.at[pl.ds(src_ref[base + r], 1)],
                                  hbuf.at[s, pl.ds(r, 1)], sem.at[s]).start()
            return carry

        lax.fori_loop(0, MOE_TM, issue, 0, unroll=8)

    @pl.when(jnp.logical_and(i == 0, nt > 0))
    def _():
        gather(0, 0)

    @pl.when(i < nt)
    def _():
        @pl.when(i + 1 < nt)
        def _():
            gather(i + 1, 1 - slot)

        pltpu.make_async_copy(h_hbm.at[pl.ds(0, MOE_TM)], hbuf.at[slot], sem.at[slot]).wait()
        y_ref[...] = _swiglu_acc(hbuf[slot].astype(BF16), w1_ref, w3_ref, w2_ref, lead=(0,))

    @pl.when(i >= nt)
    def _():
        y_ref[...] = jnp.zeros_like(y_ref)


def _moe_ffn(h, tile_expert, n_tiles, src, w1, w3, w2, ntile_max):
    t, d = h.shape
    dff = w1.shape[-1]
    grid_spec = pltpu.PrefetchScalarGridSpec(
        num_scalar_prefetch=3,
        grid=(ntile_max,),
        in_specs=[
            pl.BlockSpec(memory_space=pl.ANY),
            pl.BlockSpec((1, d, dff), lambda i, te, nt, src: (te[i], 0, 0)),
            pl.BlockSpec((1, d, dff), lambda i, te, nt, src: (te[i], 0, 0)),
            pl.BlockSpec((1, dff, d), lambda i, te, nt, src: (te[i], 0, 0)),
        ],
        out_specs=pl.BlockSpec((MOE_TM, d), lambda i, te, nt, src: (i, 0)),
        scratch_shapes=[pltpu.VMEM((2, MOE_TM, d), F32), pltpu.SemaphoreType.DMA((2,))],
    )
    return pl.pallas_call(
        _moe_ffn_kernel,
        grid_spec=grid_spec,
        out_shape=jax.ShapeDtypeStruct((ntile_max * MOE_TM, d), F32),
        compiler_params=_cparams(("arbitrary",)),
        name="moe_ffn",
    )(tile_expert, n_tiles, src, h, w1, w3, w2)


def _combine_kernel(pos_ref, x_ref, mod_ref, r_ref, y_hbm, *refs, final, tm):
    o_ref, ybuf, sem = refs[-3:]
    i = pl.program_id(0)
    slot = i % 2

    def gather(tile, s):
        base = 2 * tile * tm

        def issue(r, carry):
            for k in range(2):
                pltpu.make_async_copy(y_hbm.at[pl.ds(pos_ref[base + 2 * r + k], 1)],
                                      ybuf.at[s, k, pl.ds(r, 1)], sem.at[s]).start()
            return carry

        lax.fori_loop(0, tm, issue, 0, unroll=4)

    @pl.when(i == 0)
    def _():
        gather(0, 0)

    @pl.when(i + 1 < pl.num_programs(0))
    def _():
        gather(i + 1, 1 - slot)

    for k in range(2):
        pltpu.make_async_copy(y_hbm.at[pl.ds(0, tm)], ybuf.at[slot, k], sem.at[slot]).wait()
    r = r_ref[...]
    f = r[:, 2:3] * ybuf[slot, 0] + r[:, 3:4] * ybuf[slot, 1]
    y = x_ref[...] + mod_ref[0, 5:6, :] * f
    if final:
        ms = jnp.mean(y * y, axis=-1, keepdims=True)
        y = y * lax.rsqrt(ms + EPS) * refs[0][...]
    o_ref[...] = y


def _moe_combine(x, mod, route, pos, y_sorted, norm_final, *, rows_per_seg):
    t, d = x.shape
    tm = 256
    spt = rows_per_seg // tm
    final = norm_final is not None
    in_specs = [
        pl.BlockSpec((tm, d), lambda i, pos: (i, 0)),
        pl.BlockSpec((1, 6, d), lambda i, pos: (i // spt, 0, 0)),
        pl.BlockSpec((tm, LANES), lambda i, pos: (i, 0)),
        pl.BlockSpec(memory_space=pl.ANY),
    ]
    args = [x, mod, route, y_sorted]
    if final:
        in_specs.append(pl.BlockSpec((1, d), lambda i, pos: (0, 0)))
        args.append(norm_final.reshape(1, d))
    grid_spec = pltpu.PrefetchScalarGridSpec(
        num_scalar_prefetch=1,
        grid=(t // tm,),
        in_specs=in_specs,
        out_specs=pl.BlockSpec((tm, d), lambda i, pos: (i, 0)),
        scratch_shapes=[pltpu.VMEM((2, 2, tm, d), F32), pltpu.SemaphoreType.DMA((2,))],
    )
    return pl.pallas_call(
        functools.partial(_combine_kernel, final=final, tm=tm),
        grid_spec=grid_spec,
        out_shape=jax.ShapeDtypeStruct((t, d), F32),
        compiler_params=_cparams(("arbitrary",)),
        name="moe_combine",
    )(pos.reshape(-1), *args)


def _moe(x, g, mod, w_router, w1, w3, w2, norm_final, *, rows_per_seg):
    t, d = x.shape
    h, route = _router(x, g, mod, w_router, rows_per_seg=rows_per_seg)
    ids = route[:, :2].astype(jnp.int32)
    onehot = (ids[:, :, None] == jnp.arange(N_EXPERTS)[None, None, :]).astype(jnp.int32)
    sel = onehot.sum(axis=1)
    counts = sel.sum(axis=0)
    rank = jnp.cumsum(sel, axis=0) - sel
    padded = ((counts + MOE_TM - 1) // MOE_TM) * MOE_TM
    pstart = jnp.cumsum(padded) - padded
    cstart = jnp.cumsum(counts) - counts
    pos_e = pstart[None, :] + rank
    pos = jnp.take_along_axis(pos_e, ids, axis=1).astype(jnp.int32)
    ntile_max = (2 * t) // MOE_TM + N_EXPERTS
    tok = jnp.arange(t, dtype=jnp.int32)
    key = jnp.sort((ids * t + tok[:, None]).reshape(-1))
    sorted_tok = key % t
    tile_start = jnp.arange(ntile_max, dtype=jnp.int32) * MOE_TM
    pend = pstart + padded
    tile_expert = jnp.minimum(jnp.sum(tile_start[:, None] >= pend[None, :], axis=1), N_EXPERTS - 1)
    n_tiles = (jnp.sum(padded) // MOE_TM).astype(jnp.int32).reshape(1)
    p = jnp.arange(ntile_max * MOE_TM, dtype=jnp.int32)
    e_p = jnp.repeat(tile_expert, MOE_TM)
    r_p = jnp.minimum(p - pstart[e_p], jnp.maximum(counts[e_p] - 1, 0))
    src = sorted_tok[jnp.clip(cstart[e_p] + r_p, 0, 2 * t - 1)].astype(jnp.int32)
    y_sorted = _moe_ffn(h, tile_expert.astype(jnp.int32), n_tiles, src, w1, w3, w2, ntile_max)
    return _moe_combine(x, mod, route, pos, y_sorted, norm_final, rows_per_seg=rows_per_seg)


def _layer(x, mod, l, depth, p, *, nb, seq, ctx, caches, state_out, norm_final):
    is_ctx = ctx is None
    seg = nb * seq if is_ctx else seq
    outs = _in_proj(x, p["norm_mix"][l], mod, p["w_in"][l], rows_per_seg=seg, seq=seq, layer=l,
                    depth=depth, caches=caches if is_ctx else None)
    zb, zf = outs[0], outs[1]
    new_caches = tuple(outs[2:]) if is_ctx else None
    if is_ctx:
        yft = _fourier_ctx(zb, nb, seq)
        o_n = _attn_ctx(zb, nb, seq)
    else:
        yft = _fourier_lat(zb, nb, seq)
        o_n = _na_latent(zb, ctx[0], ctx[1], p["na_rpb"][l], nb, seq)
    hg_c = 64
    lbr = 256 if is_ctx else 512
    lbr = min(lbr, seq)
    o_f, state_out = _hgrn(zb, zf, p["lbp"][l][0], None if is_ctx else ctx[2],
                           state_out if is_ctx else None, nb=nb, seq=seq, rev=False, layer=l,
                           depth=depth, c=hg_c, lb_rows=lbr)
    o_b, state_out = _hgrn(zb, zf, p["lbp"][l][1], None if is_ctx else ctx[3],
                           state_out if is_ctx else None, nb=nb, seq=seq, rev=True, layer=l,
                           depth=depth, c=hg_c, lb_rows=lbr)
    x = _merge(x, mod, zb, yft, o_f, o_b, o_n, p["hg_norm"][l], p["w_fo"][l], p["w_ho"][l],
               p["w_no"][l], p["w_out"][l], rows_per_seg=seg)
    i = l // 2
    if l % 2 == 0:
        x = _ffn_dense(x, p["norm_ffn"][l], mod, p["w1_d"][i], p["w3_d"][i], p["w2_d"][i], norm_final,
                       rows_per_seg=seg)
    else:
        x = _moe(x, p["norm_ffn"][l], mod, p["w_router"][i], p["w1_e"][i], p["w3_e"][i], p["w2_e"][i],
                 norm_final, rows_per_seg=seg)
    return x, new_caches, state_out


def kernel(x_prompt, x_sample, cache_k, cache_v, state_hgrn, c, c_ctx, w_mod, b_mod, norm_mix,
           norm_ffn, w_in, hg_lb, hg_norm, na_rpb, w_fo, w_ho, w_no, w_out, w1_d, w3_d, w2_d,
           w_router, w1_e, w3_e, w2_e, norm_final):
    nb_c, seq_c, d = x_prompt.shape
    nb_l, seq_l, _ = x_sample.shape
    depth = w_in.shape[0]
    past = cache_k.shape[2]

    lbs = jnp.cumsum(jax.nn.softmax(hg_lb.astype(F32), axis=0), axis=0)
    lbs = lbs - lbs[0:1]
    lbp = jnp.stack([jnp.log(lbs), jnp.log1p(-lbs)], axis=2)

    bf = lambda w: w.astype(BF16)
    p = dict(norm_mix=norm_mix, norm_ffn=norm_ffn, w_in=bf(w_in), hg_norm=hg_norm, na_rpb=na_rpb,
             w_fo=bf(w_fo), w_ho=bf(w_ho), w_no=bf(w_no), w_out=bf(w_out), w1_d=bf(w1_d),
             w3_d=bf(w3_d), w2_d=bf(w2_d), w_router=w_router, w1_e=bf(w1_e), w3_e=bf(w3_e),
             w2_e=bf(w2_e), lbp=lbp)

    mods = _mod_vectors(jnp.concatenate([c_ctx[None], c], axis=0), w_mod, b_mod)

    x = x_prompt.reshape(nb_c * seq_c, d)
    caches, state_out = (None, None), None
    for l in range(depth):
        x, caches, state_out = _layer(x, mods[l, 0:1], l, depth, p, nb=nb_c, seq=seq_c, ctx=None,
                                      caches=caches, state_out=state_out,
                                      norm_final=norm_final if l == depth - 1 else None)
    y_prompt = x.reshape(nb_c, seq_c, d)

    xs = x_sample.reshape(nb_l * seq_l, d)
    for l in range(depth):
        ctx = (bf(cache_k[:, l].reshape(nb_l, past, CH)), bf(cache_v[:, l].reshape(nb_l, past, CH)),
               state_hgrn[:, l, 0], state_hgrn[:, l, 1])
        xs, _, _ = _layer(xs, mods[l, 1:], l, depth, p, nb=nb_l, seq=seq_l, ctx=ctx, caches=None,
                          state_out=None, norm_final=norm_final if l == depth - 1 else None)
    y_sample = xs.reshape(nb_l, seq_l, d)

    new_cache_k = caches[0].reshape(nb_c, depth, seq_c, NA_HEADS, NA_HD)
    new_cache_v = caches[1].reshape(nb_c, depth, seq_c, NA_HEADS, NA_HD)
    return (y_prompt, y_sample, new_cache_k, new_cache_v, state_out)
```

```python
import functools
import math

import numpy as np
import jax
import jax.numpy as jnp
from jax import lax
from jax.experimental import pallas as pl
from jax.experimental.pallas import tpu as pltpu

F32 = jnp.float32
BF16 = jnp.bfloat16

LANES = 128
SUBLANES = 8
VMEM_LIMIT = 56 * 1024 * 1024
ROW_TILE = 512

FT_GROUPS = 4
FT_GD = 128
HG_HEADS = 4
HG_D = 128
NA_HEADS = 8
NA_HD = 64
GRID_W = 64
WIN_R = 8
WIN_C = 16
N_EXPERTS = 8
EPS = 1e-6
CH = 512
NEG = -1e30

ZB_SRC = (9, 10, 11, 12, 13, 14, 0, 1, 4, 5, 6, 7, 8)
ZB_UFT, ZB_HQ, ZB_HI, ZB_HG, ZB_NQ, ZB_NK, ZB_NV = 6, 7, 8, 9, 10, 11, 12
ZF_SRC = (2, 3)
SRC_NK, SRC_NV = 7, 8


def _cparams(sem):
    return pltpu.CompilerParams(dimension_semantics=sem, vmem_limit_bytes=VMEM_LIMIT)


def _const_spec(shape):
    nd = len(shape)
    return pl.BlockSpec(shape, lambda *_: (0,) * nd, pipeline_mode=pl.Buffered(1))


def _silu(x):
    return x * (1.0 / (1.0 + jnp.exp(-x)))


def _sigmoid(x):
    return 1.0 / (1.0 + jnp.exp(-x))


def _norm_mod(x, g, shift, scale):
    ms = jnp.mean(x * x, axis=-1, keepdims=True)
    return (x * lax.rsqrt(ms + EPS) * g) * (1.0 + scale) + shift


def _mod_kernel(c_ref, w_ref, b_ref, o_ref, *, nv, tn):
    w = w_ref[0]
    for v in range(nv):
        s = _silu(c_ref[v])
        s = jnp.concatenate([s] * (tn // LANES), axis=1)
        o_ref[0, v:v + 1, :] = jnp.sum(w * s, axis=0, keepdims=True) + b_ref[0]


def _mod_vectors(cvecs, w_mod, b_mod):
    nv, d = cvecs.shape
    depth, _, n6 = w_mod.shape
    tn = 512
    cb = jnp.broadcast_to(cvecs[:, :, None], (nv, d, LANES))
    out = pl.pallas_call(
        functools.partial(_mod_kernel, nv=nv, tn=tn),
        grid=(depth, n6 // tn),
        in_specs=[
            pl.BlockSpec((nv, d, LANES), lambda l, j: (0, 0, 0)),
            pl.BlockSpec((1, d, tn), lambda l, j: (l, 0, j)),
            pl.BlockSpec((1, 1, tn), lambda l, j: (l, 0, j)),
        ],
        out_specs=pl.BlockSpec((1, nv, tn), lambda l, j: (l, 0, j)),
        out_shape=jax.ShapeDtypeStruct((depth, nv, n6), F32),
        compiler_params=_cparams(("parallel", "parallel")),
        name="mod_vectors",
    )(cb, w_mod, b_mod.reshape(depth, 1, n6))
    return out.reshape(depth, nv, 6, d)


def _in_kernel(x_ref, g_ref, mod_ref, w_ref, *refs, with_cache):
    if with_cache:
        zb_ref, zf_ref, ck_ref, cv_ref = refs[-4:]
    else:
        zb_ref, zf_ref = refs[-2:]
    h = _norm_mod(x_ref[...], g_ref[...], mod_ref[0, 0:1, :], mod_ref[0, 1:2, :]).astype(BF16)
    for src in range(15):
        z = jnp.dot(h, w_ref[:, src * CH:(src + 1) * CH], preferred_element_type=F32)
        if src in ZB_SRC:
            j = ZB_SRC.index(src)
            zb_ref[:, j * CH:(j + 1) * CH] = z.astype(BF16)
        if src in ZF_SRC:
            j = ZF_SRC.index(src)
            zf_ref[:, j * CH:(j + 1) * CH] = z
        if with_cache and src in (SRC_NK, SRC_NV):
            c_ref = ck_ref if src == SRC_NK else cv_ref
            nseq, _, seq, _ = c_ref.shape
            for s in range(nseq):
                c_ref[s, 0] = z[s * seq:(s + 1) * seq]


def _in_proj(x, g, mod, w_bf, *, rows_per_seg, seq, layer, depth, caches):
    t, d = x.shape
    tm = ROW_TILE
    with_cache = caches is not None
    spt = rows_per_seg // tm
    in_specs = [
        pl.BlockSpec((tm, d), lambda i: (i, 0)),
        _const_spec((1, d)),
        pl.BlockSpec((1, 6, d), lambda i: (i // spt, 0, 0)),
        _const_spec((d, 15 * CH)),
    ]
    args = [x, g.reshape(1, d), mod, w_bf]
    out_specs = [
        pl.BlockSpec((tm, len(ZB_SRC) * CH), lambda i: (i, 0)),
        pl.BlockSpec((tm, len(ZF_SRC) * CH), lambda i: (i, 0)),
    ]
    out_shape = [
        jax.ShapeDtypeStruct((t, len(ZB_SRC) * CH), BF16),
        jax.ShapeDtypeStruct((t, len(ZF_SRC) * CH), F32),
    ]
    aliases = {}
    if with_cache:
        assert tm % seq == 0, "a row tile holds whole context sequences"
        cspec = pl.BlockSpec((tm // seq, 1, seq, CH), lambda i: (i, layer, 0, 0))
        out_specs += [cspec, cspec]
        cshape = jax.ShapeDtypeStruct((t // seq, depth, seq, CH), F32)
        out_shape += [cshape, cshape]
        if caches[0] is not None:
            in_specs += [pl.BlockSpec(memory_space=pl.ANY)] * 2
            args += list(caches)
            aliases = {4: 2, 5: 3}
    return pl.pallas_call(
        functools.partial(_in_kernel, with_cache=with_cache),
        grid=(t // tm,),
        in_specs=in_specs,
        out_specs=out_specs,
        out_shape=out_shape,
        input_output_aliases=aliases,
        compiler_params=_cparams(("parallel",)),
        name="in_proj",
    )(*args)


def _dft_tables(n, scale):
    k = np.arange(n)
    ang = 2.0 * np.pi * ((k[:, None] * k[None, :]) % n) / n
    return np.cos(ang) * scale, np.sin(ang) * scale


def _ft_ctx_kernel(u_ref, cs_ref, lm_ref, y_ref):
    u = u_ref[...]
    cs = cs_ref[...].astype(BF16)
    zc, zs = [], []
    for g in range(FT_GROUPS):
        z = jnp.dot(u[:, g * FT_GD:(g + 1) * FT_GD], cs, preferred_element_type=F32)
        zc.append(z[:, :FT_GD])
        zs.append(z[:, FT_GD:])
    zz = jnp.concatenate([jnp.concatenate(zc, axis=1), jnp.concatenate(zs, axis=1)], axis=0)
    y = jnp.dot(lm_ref[...].astype(BF16), zz.astype(BF16), preferred_element_type=F32)
    for g in range(FT_GROUPS):
        y_ref[g] = y[:, g * FT_GD:(g + 1) * FT_GD]


def _fourier_ctx(zb, nb, seq):
    cc, sc = _dft_tables(FT_GD, FT_GD ** -0.5)
    cl, sl = _dft_tables(seq, seq ** -0.5)
    cs = jnp.asarray(np.concatenate([cc, sc], axis=1), F32)
    lm = jnp.asarray(np.concatenate([cl, -sl], axis=1), F32)
    return pl.pallas_call(
        _ft_ctx_kernel,
        grid=(nb,),
        in_specs=[
            pl.BlockSpec((seq, CH), lambda b: (b, ZB_UFT)),
            _const_spec(cs.shape),
            _const_spec(lm.shape),
        ],
        out_specs=pl.BlockSpec((FT_GROUPS, seq, FT_GD), lambda b: (0, b, 0)),
        out_shape=jax.ShapeDtypeStruct((FT_GROUPS, nb * seq, FT_GD), F32),
        compiler_params=_cparams(("parallel",)),
        name="fourier_ctx",
    )(zb, cs, lm)


def _ft_lat_kernel(u_ref, cs_ref, w1_ref, twc_ref, tws_ref, w2_ref, y_ref, zr_ref, zi_ref,
                   ar_ref, ai_ref, *, n, lw):
    rows = n * n
    rc = min(rows, 512)
    ng = lw // FT_GD
    cs = cs_ref[...].astype(BF16)
    w1 = w1_ref[...].astype(BF16)
    w2 = w2_ref[...].astype(BF16)
    for r0 in range(0, rows, rc):
        for g in range(ng):
            z = jnp.dot(u_ref[r0:r0 + rc, g * FT_GD:(g + 1) * FT_GD], cs,
                        preferred_element_type=F32)
            zr_ref[g, r0:r0 + rc, :] = z[:, :FT_GD]
            zi_ref[g, r0:r0 + rc, :] = -z[:, FT_GD:]

    def gather(re_ref, im_ref, idx):
        xr = jnp.concatenate([re_ref[g, idx, :] for g in range(ng)], axis=1)
        xi = jnp.concatenate([im_ref[g, idx, :] for g in range(ng)], axis=1)
        return jnp.concatenate([xr, xi], axis=0).astype(BF16)

    def stage1(a, carry):
        idx = pl.ds(a, n, stride=n)
        y = jnp.dot(w1, gather(zr_ref, zi_ref, idx), preferred_element_type=F32)
        yr, yi = y[:n], y[n:]
        tc = jnp.concatenate([twc_ref[a]] * ng, axis=1)
        ts = jnp.concatenate([tws_ref[a]] * ng, axis=1)
        zr = yr * tc - yi * ts
        zi = yr * ts + yi * tc
        for g in range(ng):
            ar_ref[g, idx, :] = zr[:, g * FT_GD:(g + 1) * FT_GD]
            ai_ref[g, idx, :] = zi[:, g * FT_GD:(g + 1) * FT_GD]
        return carry

    lax.fori_loop(0, n, stage1, 0, unroll=4)

    def stage2(q, carry):
        r0 = pl.multiple_of(q * n, n)
        y = jnp.dot(w2, gather(ar_ref, ai_ref, pl.ds(r0, n)), preferred_element_type=F32)
        for g in range(ng):
            y_ref[g, pl.ds(q, n, stride=n), :] = y[:, g * FT_GD:(g + 1) * FT_GD]
        return carry

    lax.fori_loop(0, n, stage2, 0, unroll=4)


def _fourier_lat(zb, nb, seq):
    n = math.isqrt(seq)
    assert n * n == seq and n % SUBLANES == 0
    lw = 256
    cc, sc = _dft_tables(FT_GD, FT_GD ** -0.5)
    cs = jnp.asarray(np.concatenate([cc, sc], axis=1), F32)
    pr, ps = _dft_tables(n, 1.0 / n)
    w1 = jnp.asarray(np.block([[pr, ps], [-ps, pr]]), F32)
    k = np.arange(n)
    ang = 2.0 * np.pi * (k[:, None] * k[None, :]) / seq
    twc = jnp.asarray(np.broadcast_to(np.cos(ang)[:, :, None], (n, n, LANES)), F32)
    tws = jnp.asarray(np.broadcast_to(-np.sin(ang)[:, :, None], (n, n, LANES)), F32)
    wr, ws = _dft_tables(n, 1.0)
    w2 = jnp.asarray(np.concatenate([wr, ws], axis=1), F32)
    nh = CH // lw
    return pl.pallas_call(
        functools.partial(_ft_lat_kernel, n=n, lw=lw),
        grid=(nb, nh),
        in_specs=[
            pl.BlockSpec((seq, lw), lambda b, h: (b, ZB_UFT * nh + h)),
            _const_spec(cs.shape),
            _const_spec(w1.shape),
            _const_spec(twc.shape),
            _const_spec(tws.shape),
            _const_spec(w2.shape),
        ],
        out_specs=pl.BlockSpec((lw // FT_GD, seq, FT_GD), lambda b, h: (h, b, 0)),
        out_shape=jax.ShapeDtypeStruct((FT_GROUPS, nb * seq, FT_GD), F32),
        scratch_shapes=[pltpu.VMEM((lw // FT_GD, seq, FT_GD), F32)] * 4,
        compiler_params=_cparams(("parallel", "parallel")),
        name="fourier_lat",
    )(zb, cs, w1, twc, tws, w2)


HG_SAFE = 160.0


def _hgrn_tables(c, rev):
    t = np.arange(c)
    tri = (t[None, :] >= t[:, None]) if rev else (t[None, :] <= t[:, None])
    lv = np.full((c, c), -1, np.int32)
    n, i = 1, 0
    while n < c:
        same = (t[:, None] // (2 * n)) == (t[None, :] // (2 * n))
        hi_t, hi_s = (t[:, None] % (2 * n)) >= n, (t[None, :] % (2 * n)) >= n
        pair = same & (~hi_t & hi_s if rev else hi_t & ~hi_s)
        lv[pair] = i
        n, i = 2 * n, i + 1
    return tri.astype(np.float32), lv


def _split3(x):
    hi = x.astype(BF16)
    r = x - hi.astype(F32)
    mid = r.astype(BF16)
    lo = (r - mid.astype(F32)).astype(BF16)
    return hi, mid, lo


def _hgrn_kernel(hq_ref, hf_ref, hi_ref, lb_ref, tri_ref, lv_ref, *refs, c, nch, rev, has_s0):
    if has_s0:
        s0_ref, o_ref, st_ref, b_ref, k_ref, q_ref = refs
    else:
        o_ref, so_ref, st_ref, b_ref, k_ref, q_ref = refs
    j = pl.program_id(1)

    @pl.when(j == 0)
    def _():
        for h in range(HG_HEADS):
            st_ref[h] = s0_ref[0, h].T if has_s0 else jnp.zeros((HG_D, HG_D), F32)

    la = lb_ref[0:1, :]
    l1 = lb_ref[1:2, :]
    lv = lv_ref[...]
    row8 = lax.broadcasted_iota(jnp.int32, (c // SUBLANES, SUBLANES, HG_D), 1)
    ti = lax.broadcasted_iota(jnp.int32, (c, c), 0)
    si = lax.broadcasted_iota(jnp.int32, (c, c), 1)
    causal = (si >= ti) if rev else (si <= ti)
    half = c // 2
    same_half = jnp.logical_and(causal, (ti < half) == (si < half))
    cross_half = jnp.logical_and(causal, (ti < half) != (si < half))
    row = lax.broadcasted_iota(jnp.int32, (c, HG_D), 0)
    first_rows = (row >= half) if rev else (row < half)
    bl_row = c - 1 if not rev else 0
    mid_row = half if rev else half - 1

    tmax = jnp.float32(0.0)
    for cc in range(nch):
        rows = slice(cc * c, (cc + 1) * c)
        z = hf_ref[rows, :]
        ls = jnp.minimum(z, 0.0) - jnp.log1p(jnp.exp(-jnp.abs(z)))
        cc_ = l1 + ls
        logf = jnp.maximum(la, cc_) + jnp.log1p(jnp.exp(-jnp.abs(la - cc_)))
        k_ref[rows, :] = 1.0 - jnp.exp(logf)
        b = sum(jnp.dot(tri_ref[...], p, preferred_element_type=F32) for p in _split3(logf))
        b_ref[rows, :] = b
        q_ref[rows, :] = _silu(hq_ref[rows, :].astype(F32))
        bm = b[mid_row:mid_row + 1, :]
        bl = b[bl_row:bl_row + 1, :]
        tmax = jnp.maximum(tmax, jnp.maximum(jnp.max(jnp.abs(bm)), jnp.max(jnp.abs(bl - bm))))
    safe = tmax <= HG_SAFE

    def chunk(cc, fast):
        rows = slice(cc * c, (cc + 1) * c)
        b = b_ref[rows, :]
        kk = k_ref[rows, :]
        q = q_ref[rows, :]
        v = hi_ref[rows, :]
        blast = b_ref[cc * c + bl_row:cc * c + bl_row + 1, :]
        bmid = b_ref[cc * c + mid_row:cc * c + mid_row + 1, :]
        brefs = {}
        n = SUBLANES
        while not fast and n < c:
            parts = []
            for m in range(c // (2 * n)):
                r = cc * c + 2 * n * m + (n if rev else n - 1)
                parts.append(jnp.broadcast_to(b_ref[r:r + 1, :], (2 * n, HG_HEADS * HG_D)))
            brefs[n] = jnp.concatenate(parts, axis=0) if len(parts) > 1 else parts[0]
            n *= 2
        outs = []
        for h in range(HG_HEADS):
            hs = slice(h * HG_D, (h + 1) * HG_D)
            qh, kh, bh, vh, blh = q[:, hs], kk[:, hs], b[:, hs], v[:, hs], blast[:, hs]
            st = st_ref[h]
            if fast:
                bmh = bmid[:, hs]
                r1 = 0.5 * bmh
                r2 = 0.5 * (bmh + blh)
                ref = jnp.where(first_rows, r1, r2)
                qn = qh * jnp.exp(bh - ref)
                kn = kh * jnp.exp(ref - bh)
                sn = lax.dot_general(qn.astype(BF16), kn.astype(BF16), (((1,), (1,)), ((), ())),
                                     preferred_element_type=F32)
                e = jnp.exp(-jnp.abs(bh - bmh))
                sx = lax.dot_general((qh * e).astype(BF16), (kh * e).astype(BF16),
                                     (((1,), (1,)), ((), ())), preferred_element_type=F32)
                a = jnp.where(same_half, sn, jnp.where(cross_half, sx, 0.0))
                qe = (qn * jnp.where(first_rows, jnp.exp(r1), jnp.exp(r2))).astype(BF16)
                ke = (kn * jnp.where(first_rows, jnp.exp(blh - r1), jnp.exp(blh - r2))).astype(BF16)
                o = lax.dot_general(qe, st.astype(BF16), (((1,), (1,)), ((), ())),
                                    preferred_element_type=F32)
                o = o + jnp.dot(a.astype(BF16), vh, preferred_element_type=F32)
            else:
                o, ke = _hgrn_levels(qh, kh, bh, vh, blh, st, brefs, hs, lv, row8, c, rev)
            outs.append(o)
            kv = lax.dot_general(vh, ke, (((0,), (0,)), ((), ())), preferred_element_type=F32)
            st_ref[h] = st * jnp.exp(blh) + kv
        o_ref[rows, :] = jnp.concatenate(outs, axis=1)

    order = [nch - 1 - ci if rev else ci for ci in range(nch)]

    @pl.when(safe)
    def _():
        for cc in order:
            chunk(cc, True)

    @pl.when(jnp.logical_not(safe))
    def _():
        for cc in order:
            chunk(cc, False)

    if not has_s0:
        @pl.when(j == pl.num_programs(1) - 1)
        def _():
            for h in range(HG_HEADS):
                so_ref[0, 0, 0, h] = st_ref[h].T


def _hgrn_levels(qh, kh, bh, vh, blh, st, brefs, hs, lv, row8, c, rev):
    qe = (qh * jnp.exp(bh)).astype(BF16)
    o = lax.dot_general(qe, st.astype(BF16), (((1,), (1,)), ((), ())), preferred_element_type=F32)
    b3 = bh.reshape(c // SUBLANES, SUBLANES, HG_D)
    a = jnp.zeros((c, c), F32)
    n, i = 1, 0
    while n < c:
        if n < SUBLANES:
            bref = None
            for g in range(SUBLANES // (2 * n)):
                r = 2 * n * g + (n if rev else n - 1)
                bc = jnp.broadcast_to(b3[:, r:r + 1, :], b3.shape)
                bref = bc if bref is None else jnp.where(row8 >= 2 * n * g, bc, bref)
            bref = bref.reshape(c, HG_D)
        else:
            bref = brefs[n][:, hs]
        e = jnp.exp(-jnp.abs(bh - bref))
        sn = lax.dot_general((qh * e).astype(BF16), (kh * e).astype(BF16),
                             (((1,), (1,)), ((), ())), preferred_element_type=F32)
        a = jnp.where(lv == i, sn, a)
        n, i = 2 * n, i + 1
    o = o + jnp.dot(a.astype(BF16), vh, preferred_element_type=F32)
    o = o + jnp.sum(qh * kh, axis=-1, keepdims=True) * vh.astype(F32)
    return o, (kh * jnp.exp(blh - bh)).astype(BF16)


def _hgrn(zb, zf, lbp, s0, state_out, *, nb, seq, rev, layer, depth, c, lb_rows):
    t = zb.shape[0]
    nblk = seq // lb_rows
    nch = lb_rows // c
    tri, lv = _hgrn_tables(c, rev)
    d = int(rev)

    def rowmap(b, j):
        return b * nblk + (nblk - 1 - j if rev else j)

    in_specs = [
        pl.BlockSpec((lb_rows, CH), lambda b, j: (rowmap(b, j), ZB_HQ)),
        pl.BlockSpec((lb_rows, CH), lambda b, j: (rowmap(b, j), d)),
        pl.BlockSpec((lb_rows, CH), lambda b, j: (rowmap(b, j), ZB_HI)),
        _const_spec((2, CH)),
        _const_spec((c, c)),
        _const_spec((c, c)),
    ]
    args = [zb, zf, zb, lbp, jnp.asarray(tri, BF16), jnp.asarray(lv)]
    has_s0 = s0 is not None
    if has_s0:
        in_specs.append(pl.BlockSpec((1, HG_HEADS, HG_D, HG_D), lambda b, j: (b, 0, 0, 0)))
        args.append(s0)
    sshape = (nb, depth, 2, HG_HEADS, HG_D, HG_D)
    aliases = {}
    if state_out is not None:
        in_specs.append(pl.BlockSpec(memory_space=pl.ANY))
        args.append(state_out)
        aliases = {len(args) - 1: 1}
    kern = functools.partial(_hgrn_kernel, c=c, nch=nch, rev=rev, has_s0=has_s0)
    if state_out is not None:
        kern = functools.partial(_drop_ref, kern, len(args) - 1)
    out_specs = [pl.BlockSpec((lb_rows, CH), lambda b, j: (rowmap(b, j), 0))]
    out_shape = [jax.ShapeDtypeStruct((t, CH), F32)]
    if not has_s0:
        out_specs.append(pl.BlockSpec((1, 1, 1, HG_HEADS, HG_D, HG_D), lambda b, j: (b, layer, d, 0, 0, 0)))
        out_shape.append(jax.ShapeDtypeStruct(sshape, F32))
    outs = pl.pallas_call(
        kern,
        grid=(nb, nblk),
        in_specs=in_specs,
        out_specs=out_specs,
        out_shape=out_shape,
        scratch_shapes=[pltpu.VMEM((HG_HEADS, HG_D, HG_D), F32)] + [pltpu.VMEM((lb_rows, CH), F32)] * 3,
        input_output_aliases=aliases,
        compiler_params=_cparams(("parallel", "arbitrary")),
        name="hgrn_bwd" if rev else "hgrn_fwd",
    )(*args)
    return (outs[0], None) if has_s0 else tuple(outs)


def _drop_ref(kern, pos, *refs):
    return kern(*refs[:pos], *refs[pos + 1:])


def _head_mask(a):
    lane = lax.broadcasted_iota(jnp.int32, (1, 2 * NA_HD), 1)
    return (lane // NA_HD) == a


def _qk(q, k):
    return lax.dot_general(q, k, (((1,), (1,)), ((), ())), preferred_element_type=F32)


def _attn_ctx_kernel(q_ref, k_ref, v_ref, o_ref):
    w = 2 * NA_HD
    for pr in range(NA_HEADS // 2):
        ps = slice(pr * w, (pr + 1) * w)
        q = q_ref[:, ps] * (NA_HD ** -0.5)
        k = k_ref[:, ps]
        v = v_ref[:, ps]
        outs = []
        for a in range(2):
            s = _qk(jnp.where(_head_mask(a), q, jnp.zeros_like(q)), k)
            p = jnp.exp(s - jnp.max(s, axis=-1, keepdims=True))
            l = jnp.sum(p, axis=-1, keepdims=True)
            outs.append(jnp.dot(p.astype(BF16), v, preferred_element_type=F32) / l)
        o_ref[:, ps] = jnp.where(_head_mask(0), outs[0], outs[1]).astype(BF16)


def _attn_ctx(zb, nb, seq):
    return pl.pallas_call(
        _attn_ctx_kernel,
        grid=(nb,),
        in_specs=[
            pl.BlockSpec((seq, CH), lambda b: (b, ZB_NQ)),
            pl.BlockSpec((seq, CH), lambda b: (b, ZB_NK)),
            pl.BlockSpec((seq, CH), lambda b: (b, ZB_NV)),
        ],
        out_specs=pl.BlockSpec((seq, CH), lambda b: (b, 0)),
        out_shape=jax.ShapeDtypeStruct((nb * seq, CH), BF16),
        compiler_params=_cparams(("parallel",)),
        name="attn_ctx",
    )(zb, zb, zb)


NA_TR = 8


def _na_geometry(rows):
    kr = min(WIN_R, rows)
    span = min(NA_TR + WIN_R, rows)
    ntile = rows // NA_TR
    pats, var, starts = [], [], []
    for t in range(ntile):
        r0 = t * NA_TR
        start = int(np.clip(r0 - kr // 2, 0, rows - span))
        qr = r0 + np.arange(NA_TR)
        row_start = np.clip(qr - kr // 2, 0, rows - kr)
        krow = start + np.arange(span)
        rv = (krow[None, :] >= row_start[:, None]) & (krow[None, :] < row_start[:, None] + kr)
        d_row = np.where(rv, krow[None, :] - qr[:, None] + WIN_R - 1, -1)
        for vi, p in enumerate(pats):
            if np.array_equal(p, d_row):
                var.append(vi)
                break
        else:
            var.append(len(pats))
            pats.append(d_row)
        starts.append(start)
    return pats, np.asarray(var, np.int32), np.asarray(starts, np.int32), span


def _na_col_tables(rpb):
    qc = np.arange(GRID_W)
    col_start = np.clip(qc - WIN_C // 2, 0, GRID_W - WIN_C)
    cv = (qc[None, :] >= col_start[:, None]) & (qc[None, :] < col_start[:, None] + WIN_C)
    d_col = qc[None, :] - qc[:, None] + WIN_C - 1
    onehot = (d_col[None] == np.arange(2 * WIN_C - 1)[:, None, None]) & cv[None]
    t = jnp.einsum("hrd,dqk->hrqk", rpb, jnp.asarray(onehot, F32), precision=lax.Precision.HIGHEST)
    t = jnp.where(jnp.asarray(cv)[None, None], t, NEG)
    return jnp.concatenate([t, t], axis=-1)


def _na_kernel(var_ref, start_ref, q_ref, k_ref, v_ref, kc_ref, vc_ref, tab_ref, o_ref, bias_ref,
               *, nkey, pats):
    t = pl.program_id(2)
    var = var_ref[t]
    changed = jnp.logical_or(t == 0, var != var_ref[jnp.maximum(t - 1, 0)])
    left = lax.broadcasted_iota(jnp.int32, (GRID_W, 2 * GRID_W), 1) < GRID_W
    neg = jnp.full((GRID_W, 2 * GRID_W), NEG, F32)

    for vi, d_row in enumerate(pats):
        @pl.when(jnp.logical_and(changed, var == vi))
        def _(d_row=d_row):
            for a in range(2):
                for i in range(d_row.shape[0]):
                    for j in range(0, d_row.shape[1], 2):
                        r0, r1 = int(d_row[i, j]), int(d_row[i, j + 1])
                        b0 = tab_ref[a, r0] if r0 >= 0 else neg
                        b1 = tab_ref[a, r1] if r1 >= 0 else neg
                        blk = neg if (r0 < 0 and r1 < 0) else jnp.where(left, b0, b1)
                        bias_ref[a, i * GRID_W:(i + 1) * GRID_W, j * GRID_W:(j + 2) * GRID_W] = blk

    st = pl.multiple_of(start_ref[t] * GRID_W, GRID_W)
    q = q_ref[...] * (NA_HD ** -0.5)
    kw = k_ref[pl.ds(st, nkey), :]
    vw = v_ref[pl.ds(st, nkey), :]
    kc = kc_ref[0]
    vc = vc_ref[0]
    outs = []
    for a in range(2):
        qa = jnp.where(_head_mask(a), q, jnp.zeros_like(q))
        s1 = _qk(qa, kw) + bias_ref[a]
        s2 = _qk(qa, kc)
        m = jnp.maximum(jnp.max(s1, axis=-1, keepdims=True), jnp.max(s2, axis=-1, keepdims=True))
        p1 = jnp.exp(s1 - m)
        p2 = jnp.exp(s2 - m)
        l = jnp.sum(p1, axis=-1, keepdims=True) + jnp.sum(p2, axis=-1, keepdims=True)
        o = (jnp.dot(p1.astype(BF16), vw, preferred_element_type=F32)
             + jnp.dot(p2.astype(BF16), vc, preferred_element_type=F32))
        outs.append(o / l)
    o_ref[...] = jnp.where(_head_mask(0), outs[0], outs[1]).astype(BF16)


def _na_latent(zb, kctx, vctx, rpb, nb, seq):
    rows = seq // GRID_W
    pats, var, starts, span = _na_geometry(rows)
    nkey = span * GRID_W
    tq = NA_TR * GRID_W
    ntile = rows // NA_TR
    hp = NA_HEADS // 2
    w = 2 * NA_HD
    past = kctx.shape[1]
    assert span % 2 == 0
    tabs = _na_col_tables(rpb)
    nrow = tabs.shape[1]
    grid_spec = pltpu.PrefetchScalarGridSpec(
        num_scalar_prefetch=2,
        grid=(nb, hp, ntile),
        in_specs=[
            pl.BlockSpec((tq, w), lambda b, p, t, *_: (b * ntile + t, ZB_NQ * hp + p)),
            pl.BlockSpec((seq, w), lambda b, p, t, *_: (b, ZB_NK * hp + p)),
            pl.BlockSpec((seq, w), lambda b, p, t, *_: (b, ZB_NV * hp + p)),
            pl.BlockSpec((1, past, w), lambda b, p, t, *_: (b, 0, p)),
            pl.BlockSpec((1, past, w), lambda b, p, t, *_: (b, 0, p)),
            pl.BlockSpec((2, nrow, GRID_W, 2 * GRID_W), lambda b, p, t, *_: (p, 0, 0, 0)),
        ],
        out_specs=pl.BlockSpec((tq, w), lambda b, p, t, *_: (b * ntile + t, p)),
        scratch_shapes=[pltpu.VMEM((2, tq, nkey), F32)],
    )
    return pl.pallas_call(
        functools.partial(_na_kernel, nkey=nkey, pats=pats),
        grid_spec=grid_spec,
        out_shape=jax.ShapeDtypeStruct((nb * seq, CH), BF16),
        compiler_params=_cparams(("parallel", "parallel", "arbitrary")),
        name="na_latent",
    )(jnp.asarray(var), jnp.asarray(starts), zb, zb, zb, kctx, vctx, tabs)


def _merge_kernel(x_ref, mod_ref, gate_ref, yft_ref, of_ref, ob_ref, hg_ref, on_ref, hn_ref,
                  wfo_ref, who_ref, wno_ref, wout_ref, o_ref):
    d = x_ref.shape[1]
    yft = jnp.concatenate([yft_ref[g] for g in range(FT_GROUPS)], axis=1)
    y_ft = jnp.dot(yft.astype(BF16), wfo_ref[...], preferred_element_type=F32)
    o = of_ref[...] + ob_ref[...]
    parts = []
    for h in range(HG_HEADS):
        oh = o[:, h * HG_D:(h + 1) * HG_D]
        ms = jnp.mean(oh * oh, axis=-1, keepdims=True)
        parts.append(oh * lax.rsqrt(ms + EPS))
    oh = jnp.concatenate(parts, axis=1) * hn_ref[...]
    oh = (oh * _silu(hg_ref[...].astype(F32))).astype(BF16)
    y_hg = jnp.dot(oh, who_ref[...], preferred_element_type=F32)
    y_na = jnp.dot(on_ref[...], wno_ref[...], preferred_element_type=F32)
    g = gate_ref[...].astype(F32)
    merged = (_sigmoid(g[:, :d]) * y_ft + _sigmoid(g[:, d:2 * d]) * y_hg
              + _sigmoid(g[:, 2 * d:]) * y_na)
    y = jnp.dot(merged.astype(BF16), wout_ref[...], preferred_element_type=F32)
    o_ref[...] = x_ref[...] + mod_ref[0, 2:3, :] * y


def _merge(x, mod, zb, yft, o_f, o_b, o_n, hg_norm, w_fo, w_ho, w_no, w_out, *, rows_per_seg):
    t, d = x.shape
    tm = ROW_TILE
    spt = rows_per_seg // tm
    assert 3 * d == 6 * CH, "the three merge gates fill the first six bf16 splits"
    row = lambda i: (i, 0)
    return pl.pallas_call(
        _merge_kernel,
        grid=(t // tm,),
        in_specs=[
            pl.BlockSpec((tm, d), row),
            pl.BlockSpec((1, 6, d), lambda i: (i // spt, 0, 0)),
            pl.BlockSpec((tm, 3 * d), row),
            pl.BlockSpec((FT_GROUPS, tm, FT_GD), lambda i: (0, i, 0)),
            pl.BlockSpec((tm, CH), row),
            pl.BlockSpec((tm, CH), row),
            pl.BlockSpec((tm, CH), lambda i: (i, ZB_HG)),
            pl.BlockSpec((tm, CH), row),
            _const_spec((1, CH)),
            _const_spec((CH, d)),
            _const_spec((CH, d)),
            _const_spec((CH, d)),
            _const_spec((d, d)),
        ],
        out_specs=pl.BlockSpec((tm, d), row),
        out_shape=jax.ShapeDtypeStruct((t, d), F32),
        compiler_params=_cparams(("parallel",)),
        name="merge",
    )(x, mod, zb, yft, o_f, o_b, zb, o_n, hg_norm.reshape(1, CH), w_fo, w_ho, w_no, w_out)


FF_CHUNK = 256


def _swiglu_acc(h, w1_ref, w3_ref, w2_ref, lead=()):
    dff = w1_ref.shape[-1]
    acc = None
    for c0 in range(0, dff, FF_CHUNK):
        cs = slice(c0, c0 + FF_CHUNK)
        g = jnp.dot(h, w1_ref[lead + (slice(None), cs)], preferred_element_type=F32)
        u = jnp.dot(h, w3_ref[lead + (slice(None), cs)], preferred_element_type=F32)
        a = (_silu(g) * u).astype(BF16)
        y = jnp.dot(a, w2_ref[lead + (cs, slice(None))], preferred_element_type=F32)
        acc = y if acc is None else acc + y
    return acc


def _ffn_kernel(x_ref, g_ref, mod_ref, w1_ref, w3_ref, w2_ref, *refs, final):
    o_ref = refs[-1]
    x = x_ref[...]
    h = _norm_mod(x, g_ref[...], mod_ref[0, 3:4, :], mod_ref[0, 4:5, :]).astype(BF16)
    y = x + mod_ref[0, 5:6, :] * _swiglu_acc(h, w1_ref, w3_ref, w2_ref)
    if final:
        ms = jnp.mean(y * y, axis=-1, keepdims=True)
        y = y * lax.rsqrt(ms + EPS) * refs[0][...]
    o_ref[...] = y


def _ffn_dense(x, g, mod, w1, w3, w2, norm_final, *, rows_per_seg):
    t, d = x.shape
    dff = w1.shape[1]
    tm = ROW_TILE
    spt = rows_per_seg // tm
    final = norm_final is not None
    in_specs = [
        pl.BlockSpec((tm, d), lambda i: (i, 0)),
        _const_spec((1, d)),
        pl.BlockSpec((1, 6, d), lambda i: (i // spt, 0, 0)),
        _const_spec((d, dff)),
        _const_spec((d, dff)),
        _const_spec((dff, d)),
    ]
    args = [x, g.reshape(1, d), mod, w1, w3, w2]
    if final:
        in_specs.append(_const_spec((1, d)))
        args.append(norm_final.reshape(1, d))
    return pl.pallas_call(
        functools.partial(_ffn_kernel, final=final),
        grid=(t // tm,),
        in_specs=in_specs,
        out_specs=pl.BlockSpec((tm, d), lambda i: (i, 0)),
        out_shape=jax.ShapeDtypeStruct((t, d), F32),
        compiler_params=_cparams(("parallel",)),
        name="ffn_dense",
    )(*args)


def _router_kernel(x_ref, g_ref, mod_ref, wr_ref, h_ref, r_ref):
    h = _norm_mod(x_ref[...], g_ref[...], mod_ref[0, 3:4, :], mod_ref[0, 4:5, :])
    h_ref[...] = h
    logits = jnp.dot(h, wr_ref[...], precision=lax.Precision.HIGHEST,
                     preferred_element_type=F32)
    lane = lax.broadcasted_iota(jnp.int32, logits.shape, 1)
    logits = jnp.where(lane < N_EXPERTS, logits, -jnp.inf)
    m1 = jnp.max(logits, axis=-1, keepdims=True)
    i1 = jnp.min(jnp.where(logits == m1, lane, LANES), axis=-1, keepdims=True)
    rest = jnp.where(lane == i1, -jnp.inf, logits)
    m2 = jnp.max(rest, axis=-1, keepdims=True)
    i2 = jnp.min(jnp.where(rest == m2, lane, LANES), axis=-1, keepdims=True)
    e2 = jnp.exp(m2 - m1)
    w1 = 1.0 / (1.0 + e2)
    w2 = e2 / (1.0 + e2)
    r = jnp.where(lane == 0, i1.astype(F32), jnp.where(lane == 1, i2.astype(F32),
                  jnp.where(lane == 2, w1, jnp.where(lane == 3, w2, 0.0))))
    r_ref[...] = r


def _router(x, g, mod, w_router, *, rows_per_seg):
    t, d = x.shape
    tm = ROW_TILE
    spt = rows_per_seg // tm
    wr =jnp.zeros((d, LANES), F32).at[:, :N_EXPERTS].set(w_router)
    return pl.pallas_call(
        _router_kernel,
        grid=(t // tm,),
        in_specs=[
            pl.BlockSpec((tm, d), lambda i: (i, 0)),
            _const_spec((1, d)),
            pl.BlockSpec((1, 6, d), lambda i: (i // spt, 0, 0)),
            _const_spec((d, LANES)),
        ],
        out_specs=[pl.BlockSpec((tm, d), lambda i: (i, 0)), pl.BlockSpec((tm, LANES), lambda i: (i, 0))],
        out_shape=[jax.ShapeDtypeStruct((t, d), F32), jax.ShapeDtypeStruct((t, LANES), F32)],
        compiler_params=_cparams(("parallel",)),
        name="router",
    )(x, g.reshape(1, d), mod, wr)


MOE_TM = 512


def _moe_ffn_kernel(te_ref, nt_ref, src_ref, h_hbm, w1_ref, w3_ref, w2_ref, y_ref, hbuf, sem):
    i = pl.program_id(0)
    nt = nt_ref[0]
    slot = i % 2

    def gather(tile, s):
        base = tile * MOE_TM

        def issue(r, carry):
            pltpu.make_async_copy(h_hbm.at[pl.ds(src_ref[base + r], 1)],
                                  hbuf.at[s, pl.ds(r, 1)], sem.at[s]).start()
            return carry

        lax.fori_loop(0, MOE_TM, issue, 0, unroll=8)

    @pl.when(jnp.logical_and(i == 0, nt > 0))
    def _():
        gather(0, 0)

    @pl.when(i < nt)
    def _():
        @pl.when(i + 1 < nt)
        def _():
            gather(i + 1, 1 - slot)

        pltpu.make_async_copy(h_hbm.at[pl.ds(0, MOE_TM)], hbuf.at[slot], sem.at[slot]).wait()
        y_ref[...] = _swiglu_acc(hbuf[slot].astype(BF16), w1_ref, w3_ref, w2_ref, lead=(0,))

    @pl.when(i >= nt)
    def _():
        y_ref[...] = jnp.zeros_like(y_ref)


def _moe_ffn(h, tile_expert, n_tiles, src, w1, w3, w2, ntile_max):
    t, d = h.shape
    dff = w1.shape[-1]
    grid_spec = pltpu.PrefetchScalarGridSpec(
        num_scalar_prefetch=3,
        grid=(ntile_max,),
        in_specs=[
            pl.BlockSpec(memory_space=pl.ANY),
            pl.BlockSpec((1, d, dff), lambda i, te, nt, src: (te[i], 0, 0)),
            pl.BlockSpec((1, d, dff), lambda i, te, nt, src: (te[i], 0, 0)),
            pl.BlockSpec((1, dff, d), lambda i, te, nt, src: (te[i], 0, 0)),
        ],
        out_specs=pl.BlockSpec((MOE_TM, d), lambda i, te, nt, src: (i, 0)),
        scratch_shapes=[pltpu.VMEM((2, MOE_TM, d), F32), pltpu.SemaphoreType.DMA((2,))],
    )
    return pl.pallas_call(
        _moe_ffn_kernel,
        grid_spec=grid_spec,
        out_shape=jax.ShapeDtypeStruct((ntile_max * MOE_TM, d), F32),
        compiler_params=_cparams(("arbitrary",)),
        name="moe_ffn",
    )(tile_expert, n_tiles, src, h, w1, w3, w2)


def _combine_kernel(pos_ref, x_ref, mod_ref, r_ref, y_hbm, *refs, final, tm):
    o_ref, ybuf, sem = refs[-3:]
    i = pl.program_id(0)
    slot = i % 2

    def gather(tile, s):
        base = 2 * tile * tm

        def issue(r, carry):
            for k in range(2):
                pltpu.make_async_copy(y_hbm.at[pl.ds(pos_ref[base + 2 * r + k], 1)],
                                      ybuf.at[s, k, pl.ds(r, 1)], sem.at[s]).start()
            return carry

        lax.fori_loop(0, tm, issue, 0, unroll=4)

    @pl.when(i == 0)
    def _():
        gather(0, 0)

    @pl.when(i + 1 < pl.num_programs(0))
    def _():
        gather(i + 1, 1 - slot)

    for k in range(2):
        pltpu.make_async_copy(y_hbm.at[pl.ds(0, tm)], ybuf.at[slot, k], sem.at[slot]).wait()
    r = r_ref[...]
    f = r[:, 2:3] * ybuf[slot, 0] + r[:, 3:4] * ybuf[slot, 1]
    y = x_ref[...] + mod_ref[0, 5:6, :] * f
    if final:
        ms = jnp.mean(y * y, axis=-1, keepdims=True)
        y = y * lax.rsqrt(ms + EPS) * refs[0][...]
    o_ref[...] = y


def _moe_combine(x, mod, route, pos, y_sorted, norm_final, *, rows_per_seg):
    t, d = x.shape
    tm = 256
    spt = rows_per_seg // tm
    final = norm_final is not None
    in_specs = [
        pl.BlockSpec((tm, d), lambda i, pos: (i, 0)),
        pl.BlockSpec((1, 6, d), lambda i, pos: (i // spt, 0, 0)),
        pl.BlockSpec((tm, LANES), lambda i, pos: (i, 0)),
        pl.BlockSpec(memory_space=pl.ANY),
    ]
    args = [x, mod, route, y_sorted]
    if final:
        in_specs.append(pl.BlockSpec((1, d), lambda i, pos: (0, 0)))
        args.append(norm_final.reshape(1, d))
    grid_spec = pltpu.PrefetchScalarGridSpec(
        num_scalar_prefetch=1,
        grid=(t // tm,),
        in_specs=in_specs,
        out_specs=pl.BlockSpec((tm, d), lambda i, pos: (i, 0)),
        scratch_shapes=[pltpu.VMEM((2, 2, tm, d), F32), pltpu.SemaphoreType.DMA((2,))],
    )
    return pl.pallas_call(
        functools.partial(_combine_kernel, final=final, tm=tm),
        grid_spec=grid_spec,
        out_shape=jax.ShapeDtypeStruct((t, d), F32),
        compiler_params=_cparams(("arbitrary",)),
        name="moe_combine",
    )(pos.reshape(-1), *args)


def _moe(x, g, mod, w_router, w1, w3, w2, norm_final, *, rows_per_seg):
    t, d = x.shape
    h, route = _router(x, g, mod, w_router, rows_per_seg=rows_per_seg)
    ids = route[:, :2].astype(jnp.int32)
    onehot = (ids[:, :, None] == jnp.arange(N_EXPERTS)[None, None, :]).astype(jnp.int32)
    sel = onehot.sum(axis=1)
    counts = sel.sum(axis=0)
    rank = jnp.cumsum(sel, axis=0) - sel
    padded = ((counts + MOE_TM - 1) // MOE_TM) * MOE_TM
    pstart = jnp.cumsum(padded) - padded
    cstart = jnp.cumsum(counts) - counts
    pos_e = pstart[None, :] + rank
    pos = jnp.take_along_axis(pos_e, ids, axis=1).astype(jnp.int32)
    ntile_max = (2 * t) // MOE_TM + N_EXPERTS
    tok = jnp.arange(t, dtype=jnp.int32)
    key = jnp.sort((ids * t + tok[:, None]).reshape(-1))
    sorted_tok = key % t
    tile_start = jnp.arange(ntile_max, dtype=jnp.int32) * MOE_TM
    pend = pstart + padded
    tile_expert = jnp.minimum(jnp.sum(tile_start[:, None] >= pend[None, :], axis=1), N_EXPERTS - 1)
    n_tiles = (jnp.sum(padded) // MOE_TM).astype(jnp.int32).reshape(1)
    p = jnp.arange(ntile_max * MOE_TM, dtype=jnp.int32)
    e_p = jnp.repeat(tile_expert, MOE_TM)
    r_p = jnp.minimum(p - pstart[e_p], jnp.maximum(counts[e_p] - 1, 0))
    src = sorted_tok[jnp.clip(cstart[e_p] + r_p, 0, 2 * t - 1)].astype(jnp.int32)
    y_sorted = _moe_ffn(h, tile_expert.astype(jnp.int32), n_tiles, src, w1, w3, w2, ntile_max)
    return _moe_combine(x, mod, route, pos, y_sorted, norm_final, rows_per_seg=rows_per_seg)


def _layer(x, mod, l, depth, p, *, nb, seq, ctx, caches, state_out, norm_final):
    is_ctx = ctx is None
    seg = nb * seq if is_ctx else seq
    outs = _in_proj(x, p["norm_mix"][l], mod, p["w_in"][l], rows_per_seg=seg, seq=seq, layer=l,
                    depth=depth, caches=caches if is_ctx else None)
    zb, zf = outs[0], outs[1]
    new_caches = tuple(outs[2:]) if is_ctx else None
    if is_ctx:
        yft = _fourier_ctx(zb, nb, seq)
        o_n = _attn_ctx(zb, nb, seq)
    else:
        yft = _fourier_lat(zb, nb, seq)
        o_n = _na_latent(zb, ctx[0], ctx[1], p["na_rpb"][l], nb, seq)
    hg_c = 128
    lbr = 256 if is_ctx else 512
    lbr = min(lbr, seq)
    o_f, state_out = _hgrn(zb, zf, p["lbp"][l][0], None if is_ctx else ctx[2],
                           state_out if is_ctx else None, nb=nb, seq=seq, rev=False, layer=l,
                           depth=depth, c=hg_c, lb_rows=lbr)
    o_b, state_out = _hgrn(zb, zf, p["lbp"][l][1], None if is_ctx else ctx[3],
                           state_out if is_ctx else None, nb=nb, seq=seq, rev=True, layer=l,
                           depth=depth, c=hg_c, lb_rows=lbr)
    x = _merge(x, mod, zb, yft, o_f, o_b, o_n, p["hg_norm"][l], p["w_fo"][l], p["w_ho"][l],
               p["w_no"][l], p["w_out"][l], rows_per_seg=seg)
    i = l // 2
    if l % 2 == 0:
        x = _ffn_dense(x, p["norm_ffn"][l], mod, p["w1_d"][i], p["w3_d"][i], p["w2_d"][i], norm_final,
                       rows_per_seg=seg)
    else:
        x = _moe(x, p["norm_ffn"][l], mod, p["w_router"][i], p["w1_e"][i], p["w3_e"][i], p["w2_e"][i],
                 norm_final, rows_per_seg=seg)
    return x, new_caches, state_out


def kernel(x_prompt, x_sample, cache_k, cache_v, state_hgrn, c, c_ctx, w_mod, b_mod, norm_mix,
           norm_ffn, w_in, hg_lb, hg_norm, na_rpb, w_fo, w_ho, w_no, w_out, w1_d, w3_d, w2_d,
           w_router, w1_e, w3_e, w2_e, norm_final):
    nb_c, seq_c, d = x_prompt.shape
    nb_l, seq_l, _ = x_sample.shape
    depth = w_in.shape[0]
    past = cache_k.shape[2]

    lbs = jnp.cumsum(jax.nn.softmax(hg_lb.astype(F32), axis=0), axis=0)
    lbs = lbs - lbs[0:1]
    lbp = jnp.stack([jnp.log(lbs), jnp.log1p(-lbs)], axis=2)

    bf = lambda w: w.astype(BF16)
    p = dict(norm_mix=norm_mix, norm_ffn=norm_ffn, w_in=bf(w_in), hg_norm=hg_norm, na_rpb=na_rpb,
             w_fo=bf(w_fo), w_ho=bf(w_ho), w_no=bf(w_no), w_out=bf(w_out), w1_d=bf(w1_d),
             w3_d=bf(w3_d), w2_d=bf(w2_d), w_router=w_router, w1_e=bf(w1_e), w3_e=bf(w3_e),
             w2_e=bf(w2_e), lbp=lbp)

    mods = _mod_vectors(jnp.concatenate([c_ctx[None], c], axis=0), w_mod, b_mod)

    x = x_prompt.reshape(nb_c * seq_c, d)
    caches, state_out = (None, None), None
    for l in range(depth):
        x, caches, state_out = _layer(x, mods[l, 0:1], l, depth, p, nb=nb_c, seq=seq_c, ctx=None,
                                      caches=caches, state_out=state_out,
                                      norm_final=norm_final if l == depth - 1 else None)
    y_prompt = x.reshape(nb_c, seq_c, d)

    xs = x_sample.reshape(nb_l * seq_l, d)
    for l in range(depth):
        ctx = (bf(cache_k[:, l].reshape(nb_l, past, CH)), bf(cache_v[:, l].reshape(nb_l, past, CH)),
               state_hgrn[:, l, 0], state_hgrn[:, l, 1])
        xs, _, _ = _layer(xs, mods[l, 1:], l, depth, p, nb=nb_l, seq=seq_l, ctx=ctx, caches=None,
                          state_out=None, norm_final=norm_final if l == depth - 1 else None)
    y_sample = xs.reshape(nb_l, seq_l, d)

    new_cache_k = caches[0].reshape(nb_c, depth, seq_c, NA_HEADS, NA_HD)
    new_cache_v = caches[1].reshape(nb_c, depth, seq_c, NA_HEADS, NA_HD)
    return (y_prompt, y_sample, new_cache_k, new_cache_v, state_out)
```

```python
import functools
import math

import numpy as np
import jax
import jax.numpy as jnp
from jax import lax
from jax.experimental import pallas as pl
from jax.experimental.pallas import tpu as pltpu

F32 = jnp.float32
BF16 = jnp.bfloat16

LANES = 128
SUBLANES = 8
VMEM_LIMIT = 56 * 1024 * 1024
ROW_TILE = 512

FT_GROUPS = 4
FT_GD = 128
HG_HEADS = 4
HG_D = 128
NA_HEADS = 8
NA_HD = 64
GRID_W = 64
WIN_R = 8
WIN_C = 16
N_EXPERTS = 8
EPS = 1e-6
CH = 512
NEG = -1e30

ZB_SRC = (9, 10, 11, 12, 13, 14, 0, 1, 4, 5, 6, 7, 8)
ZB_UFT, ZB_HQ, ZB_HI, ZB_HG, ZB_NQ, ZB_NK, ZB_NV = 6, 7, 8, 9, 10, 11, 12
ZF_SRC = (2, 3)
SRC_NK, SRC_NV = 7, 8


def _cparams(sem):
    return pltpu.CompilerParams(dimension_semantics=sem, vmem_limit_bytes=VMEM_LIMIT)


def _const_spec(shape):
    nd = len(shape)
    return pl.BlockSpec(shape, lambda *_: (0,) * nd, pipeline_mode=pl.Buffered(1))


def _silu(x):
    return x * (1.0 / (1.0 + jnp.exp(-x)))


def _sigmoid(x):
    return 1.0 / (1.0 + jnp.exp(-x))


def _norm_mod(x, g, shift, scale):
    ms = jnp.mean(x * x, axis=-1, keepdims=True)
    return (x * lax.rsqrt(ms + EPS) * g) * (1.0 + scale) + shift


def _mod_kernel(c_ref, w_ref, b_ref, o_ref, *, nv, tn):
    w = w_ref[0]
    for v in range(nv):
        s = _silu(c_ref[v])
        s = jnp.concatenate([s] * (tn // LANES), axis=1)
        o_ref[0, v:v + 1, :] = jnp.sum(w * s, axis=0, keepdims=True) + b_ref[0]


def _mod_vectors(cvecs, w_mod, b_mod):
    nv, d = cvecs.shape
    depth, _, n6 = w_mod.shape
    tn = 512
    cb = jnp.broadcast_to(cvecs[:, :, None], (nv, d, LANES))
    out = pl.pallas_call(
        functools.partial(_mod_kernel, nv=nv, tn=tn),
        grid=(depth, n6 // tn),
        in_specs=[
            pl.BlockSpec((nv, d, LANES), lambda l, j: (0, 0, 0)),
            pl.BlockSpec((1, d, tn), lambda l, j: (l, 0, j)),
            pl.BlockSpec((1, 1, tn), lambda l, j: (l, 0, j)),
        ],
        out_specs=pl.BlockSpec((1, nv, tn), lambda l, j: (l, 0, j)),
        out_shape=jax.ShapeDtypeStruct((depth, nv, n6), F32),
        compiler_params=_cparams(("parallel", "parallel")),
        name="mod_vectors",
    )(cb, w_mod, b_mod.reshape(depth, 1, n6))
    return out.reshape(depth, nv, 6, d)


def _in_kernel(x_ref, g_ref, mod_ref, w_ref, *refs, with_cache):
    if with_cache:
        zb_ref, zf_ref, ck_ref, cv_ref = refs[-4:]
    else:
        zb_ref, zf_ref = refs[-2:]
    h = _norm_mod(x_ref[...], g_ref[...], mod_ref[0, 0:1, :], mod_ref[0, 1:2, :]).astype(BF16)
    for src in range(15):
        z = jnp.dot(h, w_ref[:, src * CH:(src + 1) * CH], preferred_element_type=F32)
        if src in ZB_SRC:
            j = ZB_SRC.index(src)
            zb_ref[:, j * CH:(j + 1) * CH] = z.astype(BF16)
        if src in ZF_SRC:
            j = ZF_SRC.index(src)
            zf_ref[:, j * CH:(j + 1) * CH] = z
        if with_cache and src in (SRC_NK, SRC_NV):
            c_ref = ck_ref if src == SRC_NK else cv_ref
            nseq, _, seq, _ = c_ref.shape
            for s in range(nseq):
                c_ref[s, 0] = z[s * seq:(s + 1) * seq]


def _in_proj(x, g, mod, w_bf, *, rows_per_seg, seq, layer, depth, caches):
    t, d = x.shape
    tm = ROW_TILE
    with_cache = caches is not None
    spt = rows_per_seg // tm
    in_specs = [
        pl.BlockSpec((tm, d), lambda i: (i, 0)),
        _const_spec((1, d)),
        pl.BlockSpec((1, 6, d), lambda i: (i // spt, 0, 0)),
        _const_spec((d, 15 * CH)),
    ]
    args = [x, g.reshape(1, d), mod, w_bf]
    out_specs = [
        pl.BlockSpec((tm, len(ZB_SRC) * CH), lambda i: (i, 0)),
        pl.BlockSpec((tm, len(ZF_SRC) * CH), lambda i: (i, 0)),
    ]
    out_shape = [
        jax.ShapeDtypeStruct((t, len(ZB_SRC) * CH), BF16),
        jax.ShapeDtypeStruct((t, len(ZF_SRC) * CH), F32),
    ]
    aliases = {}
    if with_cache:
        assert tm % seq == 0, "a row tile holds whole context sequences"
        cspec = pl.BlockSpec((tm // seq, 1, seq, CH), lambda i: (i, layer, 0, 0))
        out_specs += [cspec, cspec]
        cshape = jax.ShapeDtypeStruct((t // seq, depth, seq, CH), F32)
        out_shape += [cshape, cshape]
        if caches[0] is not None:
            in_specs += [pl.BlockSpec(memory_space=pl.ANY)] * 2
            args += list(caches)
            aliases = {4: 2, 5: 3}
    return pl.pallas_call(
        functools.partial(_in_kernel, with_cache=with_cache),
        grid=(t // tm,),
        in_specs=in_specs,
        out_specs=out_specs,
        out_shape=out_shape,
        input_output_aliases=aliases,
        compiler_params=_cparams(("parallel",)),
        name="in_proj",
    )(*args)


def _dft_tables(n, scale):
    k = np.arange(n)
    ang = 2.0 * np.pi * ((k[:, None] * k[None, :]) % n) / n
    return np.cos(ang) * scale, np.sin(ang) * scale


def _ft_ctx_kernel(u_ref, cs_ref, lm_ref, y_ref):
    u = u_ref[...]
    cs = cs_ref[...].astype(BF16)
    zc, zs = [], []
    for g in range(FT_GROUPS):
        z = jnp.dot(u[:, g * FT_GD:(g + 1) * FT_GD], cs, preferred_element_type=F32)
        zc.append(z[:, :FT_GD])
        zs.append(z[:, FT_GD:])
    zz = jnp.concatenate([jnp.concatenate(zc, axis=1), jnp.concatenate(zs, axis=1)], axis=0)
    y = jnp.dot(lm_ref[...].astype(BF16), zz.astype(BF16), preferred_element_type=F32)
    for g in range(FT_GROUPS):
        y_ref[g] = y[:, g * FT_GD:(g + 1) * FT_GD]


def _fourier_ctx(zb, nb, seq):
    cc, sc = _dft_tables(FT_GD, FT_GD ** -0.5)
    cl, sl = _dft_tables(seq, seq ** -0.5)
    cs = jnp.asarray(np.concatenate([cc, sc], axis=1), F32)
    lm = jnp.asarray(np.concatenate([cl, -sl], axis=1), F32)
    return pl.pallas_call(
        _ft_ctx_kernel,
        grid=(nb,),
        in_specs=[
            pl.BlockSpec((seq, CH), lambda b: (b, ZB_UFT)),
            _const_spec(cs.shape),
            _const_spec(lm.shape),
        ],
        out_specs=pl.BlockSpec((FT_GROUPS, seq, FT_GD), lambda b: (0, b, 0)),
        out_shape=jax.ShapeDtypeStruct((FT_GROUPS, nb * seq, FT_GD), F32),
        compiler_params=_cparams(("parallel",)),
        name="fourier_ctx",
    )(zb, cs, lm)


def _ft_lat_kernel(u_ref, cs_ref, w1_ref, twc_ref, tws_ref, w2_ref, y_ref, zr_ref, zi_ref,
                   ar_ref, ai_ref, *, n, lw):
    rows = n * n
    rc = min(rows, 512)
    ng = lw // FT_GD
    cs = cs_ref[...].astype(BF16)
    w1 = w1_ref[...].astype(BF16)
    w2 = w2_ref[...].astype(BF16)
    for r0 in range(0, rows, rc):
        for g in range(ng):
            z = jnp.dot(u_ref[r0:r0 + rc, g * FT_GD:(g + 1) * FT_GD], cs,
                        preferred_element_type=F32)
            zr_ref[g, r0:r0 + rc, :] = z[:, :FT_GD]
            zi_ref[g, r0:r0 + rc, :] = -z[:, FT_GD:]

    def gather(re_ref, im_ref, idx):
        xr = jnp.concatenate([re_ref[g, idx, :] for g in range(ng)], axis=1)
        xi = jnp.concatenate([im_ref[g, idx, :] for g in range(ng)], axis=1)
        return jnp.concatenate([xr, xi], axis=0).astype(BF16)

    def stage1(a, carry):
        idx = pl.ds(a, n, stride=n)
        y = jnp.dot(w1, gather(zr_ref, zi_ref, idx), preferred_element_type=F32)
        yr, yi = y[:n], y[n:]
        tc = jnp.concatenate([twc_ref[a]] * ng, axis=1)
        ts = jnp.concatenate([tws_ref[a]] * ng, axis=1)
        zr = yr * tc - yi * ts
        zi = yr * ts + yi * tc
        for g in range(ng):
            ar_ref[g, idx, :] = zr[:, g * FT_GD:(g + 1) * FT_GD]
            ai_ref[g, idx, :] = zi[:, g * FT_GD:(g + 1) * FT_GD]
        return carry

    lax.fori_loop(0, n, stage1, 0, unroll=4)

    def stage2(q, carry):
        r0 = pl.multiple_of(q * n, n)
        y = jnp.dot(w2, gather(ar_ref, ai_ref, pl.ds(r0, n)), preferred_element_type=F32)
        for g in range(ng):
            y_ref[g, pl.ds(q, n, stride=n), :] = y[:, g * FT_GD:(g + 1) * FT_GD]
        return carry

    lax.fori_loop(0, n, stage2, 0, unroll=4)


def _fourier_lat(zb, nb, seq):
    n = math.isqrt(seq)
    assert n * n == seq and n % SUBLANES == 0
    lw = 256
    cc, sc = _dft_tables(FT_GD, FT_GD ** -0.5)
    cs = jnp.asarray(np.concatenate([cc, sc], axis=1), F32)
    pr, ps = _dft_tables(n, 1.0 / n)
    w1 = jnp.asarray(np.block([[pr, ps], [-ps, pr]]), F32)
    k = np.arange(n)
    ang = 2.0 * np.pi * (k[:, None] * k[None, :]) / seq
    twc = jnp.asarray(np.broadcast_to(np.cos(ang)[:, :, None], (n, n, LANES)), F32)
    tws = jnp.asarray(np.broadcast_to(-np.sin(ang)[:, :, None], (n, n, LANES)), F32)
    wr, ws = _dft_tables(n, 1.0)
    w2 = jnp.asarray(np.concatenate([wr, ws], axis=1), F32)
    nh = CH // lw
    return pl.pallas_call(
        functools.partial(_ft_lat_kernel, n=n, lw=lw),
        grid=(nb, nh),
        in_specs=[
            pl.BlockSpec((seq, lw), lambda b, h: (b, ZB_UFT * nh + h)),
            _const_spec(cs.shape),
            _const_spec(w1.shape),
            _const_spec(twc.shape),
            _const_spec(tws.shape),
            _const_spec(w2.shape),
        ],
        out_specs=pl.BlockSpec((lw // FT_GD, seq, FT_GD), lambda b, h: (h, b, 0)),
        out_shape=jax.ShapeDtypeStruct((FT_GROUPS, nb * seq, FT_GD), F32),
        scratch_shapes=[pltpu.VMEM((lw // FT_GD, seq, FT_GD), F32)] * 4,
        compiler_params=_cparams(("parallel", "parallel")),
        name="fourier_lat",
    )(zb, cs, w1, twc, tws, w2)


HG_SAFE = 160.0


def _hgrn_tables(c, rev):
    t = np.arange(c)
    tri = (t[None, :] >= t[:, None]) if rev else (t[None, :] <= t[:, None])
    lv = np.full((c, c), -1, np.int32)
    n, i = 1, 0
    while n < c:
        same = (t[:, None] // (2 * n)) == (t[None, :] // (2 * n))
        hi_t, hi_s = (t[:, None] % (2 * n)) >= n, (t[None, :] % (2 * n)) >= n
        pair = same & (~hi_t & hi_s if rev else hi_t & ~hi_s)
        lv[pair] = i
        n, i = 2 * n, i + 1
    return tri.astype(np.float32), lv


def _split3(x):
    hi = x.astype(BF16)
    r = x - hi.astype(F32)
    mid = r.astype(BF16)
    lo = (r - mid.astype(F32)).astype(BF16)
    return hi, mid, lo


def _hgrn_kernel(hq_ref, hf_ref, hi_ref, lb_ref, tri_ref, lv_ref, *refs, c, nch, rev, has_s0):
    if has_s0:
        s0_ref, o_ref, st_ref, b_ref, k_ref, q_ref = refs
    else:
        o_ref, so_ref, st_ref, b_ref, k_ref, q_ref = refs
    j = pl.program_id(1)

    @pl.when(j == 0)
    def _():
        for h in range(HG_HEADS):
            st_ref[h] = s0_ref[0, h].T if has_s0 else jnp.zeros((HG_D, HG_D), F32)

    la = lb_ref[0:1, :]
    l1 = lb_ref[1:2, :]
    lv = lv_ref[...]
    row8 = lax.broadcasted_iota(jnp.int32, (c // SUBLANES, SUBLANES, HG_D), 1)
    ti = lax.broadcasted_iota(jnp.int32, (c, c), 0)
    si = lax.broadcasted_iota(jnp.int32, (c, c), 1)
    causal = (si >= ti) if rev else (si <= ti)
    half = c // 2
    same_half = jnp.logical_and(causal, (ti < half) == (si < half))
    cross_half = jnp.logical_and(causal, (ti < half) != (si < half))
    row = lax.broadcasted_iota(jnp.int32, (c, HG_D), 0)
    first_rows = (row >= half) if rev else (row < half)
    bl_row = c - 1 if not rev else 0
    mid_row = half if rev else half - 1

    tmax = jnp.float32(0.0)
    for cc in range(nch):
        rows = slice(cc * c, (cc + 1) * c)
        z = hf_ref[rows, :]
        ls = jnp.minimum(z, 0.0) - jnp.log(1.0 + jnp.exp(-jnp.abs(z)))
        cc_ = l1 + ls
        logf = jnp.maximum(la, cc_) + jnp.log(1.0 + jnp.exp(-jnp.abs(la - cc_)))
        k_ref[rows, :] = 1.0 - jnp.exp(logf)
        b = sum(jnp.dot(tri_ref[...], p, preferred_element_type=F32) for p in _split3(logf))
        b_ref[rows, :] = b
        q_ref[rows, :] = _silu(hq_ref[rows, :].astype(F32))
        bm = b[mid_row:mid_row + 1, :]
        bl = b[bl_row:bl_row + 1, :]
        tmax = jnp.maximum(tmax, jnp.maximum(jnp.max(jnp.abs(bm)), jnp.max(jnp.abs(bl - bm))))
    safe = tmax <= HG_SAFE

    def chunk(cc, fast):
        rows = slice(cc * c, (cc + 1) * c)
        b = b_ref[rows, :]
        kk = k_ref[rows, :]
        q = q_ref[rows, :]
        v = hi_ref[rows, :]
        blast = b_ref[cc * c + bl_row:cc * c + bl_row + 1, :]
        bmid = b_ref[cc * c + mid_row:cc * c + mid_row + 1, :]
        brefs = {}
        n = SUBLANES
        while not fast and n < c:
            parts = []
            for m in range(c // (2 * n)):
                r = cc * c + 2 * n * m + (n if rev else n - 1)
                parts.append(jnp.broadcast_to(b_ref[r:r + 1, :], (2 * n, HG_HEADS * HG_D)))
            brefs[n] = jnp.concatenate(parts, axis=0) if len(parts) > 1 else parts[0]
            n *= 2
        outs = []
        for h in range(HG_HEADS):
            hs = slice(h * HG_D, (h + 1) * HG_D)
            qh, kh, bh, vh, blh = q[:, hs], kk[:, hs], b[:, hs], v[:, hs], blast[:, hs]
            st = st_ref[h]
            if fast:
                bmh = bmid[:, hs]
                r1 = 0.5 * bmh
                r2 = 0.5 * (bmh + blh)
                ref = jnp.where(first_rows, r1, r2)
                qn = qh * jnp.exp(bh - ref)
                kn = kh * jnp.exp(ref - bh)
                sn = lax.dot_general(qn.astype(BF16), kn.astype(BF16), (((1,), (1,)), ((), ())),
                                     preferred_element_type=F32)
                e = jnp.exp(-jnp.abs(bh - bmh))
                sx = lax.dot_general((qh * e).astype(BF16), (kh * e).astype(BF16),
                                     (((1,), (1,)), ((), ())), preferred_element_type=F32)
                a = jnp.where(same_half, sn, jnp.where(cross_half, sx, 0.0))
                qe = (qn * jnp.where(first_rows, jnp.exp(r1), jnp.exp(r2))).astype(BF16)
                ke = (kn * jnp.where(first_rows, jnp.exp(blh - r1), jnp.exp(blh - r2))).astype(BF16)
                o = lax.dot_general(qe, st.astype(BF16), (((1,), (1,)), ((), ())),
                                    preferred_element_type=F32)
                o = o + jnp.dot(a.astype(BF16), vh, preferred_element_type=F32)
            else:
                o, ke = _hgrn_levels(qh, kh, bh, vh, blh, st, brefs, hs, lv, row8, c, rev)
            outs.append(o)
            kv = lax.dot_general(vh, ke, (((0,), (0,)), ((), ())), preferred_element_type=F32)
            st_ref[h] = st * jnp.exp(blh) + kv
        o_ref[rows, :] = jnp.concatenate(outs, axis=1)

    order = [nch - 1 - ci if rev else ci for ci in range(nch)]

    @pl.when(safe)
    def _():
        for cc in order:
            chunk(cc, True)

    @pl.when(jnp.logical_not(safe))
    def _():
        for cc in order:
            chunk(cc, False)

    if not has_s0:
        @pl.when(j == pl.num_programs(1) - 1)
        def _():
            for h in range(HG_HEADS):
                so_ref[0, 0, 0, h] = st_ref[h].T


def _hgrn_levels(qh, kh, bh, vh, blh, st, brefs, hs, lv, row8, c, rev):
    qe = (qh * jnp.exp(bh)).astype(BF16)
    o = lax.dot_general(qe, st.astype(BF16), (((1,), (1,)), ((), ())), preferred_element_type=F32)
    b3 = bh.reshape(c // SUBLANES, SUBLANES, HG_D)
    a = jnp.zeros((c, c), F32)
    n, i = 1, 0
    while n < c:
        if n < SUBLANES:
            bref = None
            for g in range(SUBLANES // (2 * n)):
                r = 2 * n * g + (n if rev else n - 1)
                bc = jnp.broadcast_to(b3[:, r:r + 1, :], b3.shape)
                bref = bc if bref is None else jnp.where(row8 >= 2 * n * g, bc, bref)
            bref = bref.reshape(c, HG_D)
        else:
            bref = brefs[n][:, hs]
        e = jnp.exp(-jnp.abs(bh - bref))
        sn = lax.dot_general((qh * e).astype(BF16), (kh * e).astype(BF16),
                             (((1,), (1,)), ((), ())), preferred_element_type=F32)
        a = jnp.where(lv == i, sn, a)
        n, i = 2 * n, i + 1
    o = o + jnp.dot(a.astype(BF16), vh, preferred_element_type=F32)
    o = o + jnp.sum(qh * kh, axis=-1, keepdims=True) * vh.astype(F32)
    return o, (kh * jnp.exp(blh - bh)).astype(BF16)


def _hgrn(zb, zf, lbp, s0, state_out, *, nb, seq, rev, layer, depth, c, lb_rows):
    t = zb.shape[0]
    nblk = seq // lb_rows
    nch = lb_rows // c
    tri, lv = _hgrn_tables(c, rev)
    d = int(rev)

    def rowmap(b, j):
        return b * nblk + (nblk - 1 - j if rev else j)

    in_specs = [
        pl.BlockSpec((lb_rows, CH), lambda b, j: (rowmap(b, j), ZB_HQ)),
        pl.BlockSpec((lb_rows, CH), lambda b, j: (rowmap(b, j), d)),
        pl.BlockSpec((lb_rows, CH), lambda b, j: (rowmap(b, j), ZB_HI)),
        _const_spec((2, CH)),
        _const_spec((c, c)),
        _const_spec((c, c)),
    ]
    args = [zb, zf, zb, lbp, jnp.asarray(tri, BF16), jnp.asarray(lv)]
    has_s0 = s0 is not None
    if has_s0:
        in_specs.append(pl.BlockSpec((1, HG_HEADS, HG_D, HG_D), lambda b, j: (b, 0, 0, 0)))
        args.append(s0)
    sshape = (nb, depth, 2, HG_HEADS, HG_D, HG_D)
    aliases = {}
    if state_out is not None:
        in_specs.append(pl.BlockSpec(memory_space=pl.ANY))
        args.append(state_out)
        aliases = {len(args) - 1: 1}
    kern = functools.partial(_hgrn_kernel, c=c, nch=nch, rev=rev, has_s0=has_s0)
    if state_out is not None:
        kern = functools.partial(_drop_ref, kern, len(args) - 1)
    out_specs = [pl.BlockSpec((lb_rows, CH), lambda b, j: (rowmap(b, j), 0))]
    out_shape = [jax.ShapeDtypeStruct((t, CH), F32)]
    if not has_s0:
        out_specs.append(pl.BlockSpec((1, 1, 1, HG_HEADS, HG_D, HG_D), lambda b, j: (b, layer, d, 0, 0, 0)))
        out_shape.append(jax.ShapeDtypeStruct(sshape, F32))
    outs = pl.pallas_call(
        kern,
        grid=(nb, nblk),
        in_specs=in_specs,
        out_specs=out_specs,
        out_shape=out_shape,
        scratch_shapes=[pltpu.VMEM((HG_HEADS, HG_D, HG_D), F32)] + [pltpu.VMEM((lb_rows, CH), F32)] * 3,
        input_output_aliases=aliases,
        compiler_params=_cparams(("parallel", "arbitrary")),
        name="hgrn_bwd" if rev else "hgrn_fwd",
    )(*args)
    return (outs[0], None) if has_s0 else tuple(outs)


def _drop_ref(kern, pos, *refs):
    return kern(*refs[:pos], *refs[pos + 1:])


def _head_mask(a):
    lane = lax.broadcasted_iota(jnp.int32, (1, 2 * NA_HD), 1)
    return (lane // NA_HD) == a


def _qk(q, k):
    return lax.dot_general(q, k, (((1,), (1,)), ((), ())), preferred_element_type=F32)


def _attn_ctx_kernel(q_ref, k_ref, v_ref, o_ref):
    w = 2 * NA_HD
    for pr in range(NA_HEADS // 2):
        ps = slice(pr * w, (pr + 1) * w)
        q = q_ref[:, ps] * (NA_HD ** -0.5)
        k = k_ref[:, ps]
        v = v_ref[:, ps]
        outs = []
        for a in range(2):
            s = _qk(jnp.where(_head_mask(a), q, jnp.zeros_like(q)), k)
            p = jnp.exp(s - jnp.max(s, axis=-1, keepdims=True))
            l = jnp.sum(p, axis=-1, keepdims=True)
            outs.append(jnp.dot(p.astype(BF16), v, preferred_element_type=F32) / l)
        o_ref[:, ps] = jnp.where(_head_mask(0), outs[0], outs[1]).astype(BF16)


def _attn_ctx(zb, nb, seq):
    return pl.pallas_call(
        _attn_ctx_kernel,
        grid=(nb,),
        in_specs=[
            pl.BlockSpec((seq, CH), lambda b: (b, ZB_NQ)),
            pl.BlockSpec((seq, CH), lambda b: (b, ZB_NK)),
            pl.BlockSpec((seq, CH), lambda b: (b, ZB_NV)),
        ],
        out_specs=pl.BlockSpec((seq, CH), lambda b: (b, 0)),
        out_shape=jax.ShapeDtypeStruct((nb * seq, CH), BF16),
        compiler_params=_cparams(("parallel",)),
        name="attn_ctx",
    )(zb, zb, zb)


NA_TR = 8


def _na_geometry(rows):
    kr = min(WIN_R, rows)
    span = min(NA_TR + WIN_R, rows)
    ntile = rows // NA_TR
    pats, var, starts = [], [], []
    for t in range(ntile):
        r0 = t * NA_TR
        start = int(np.clip(r0 - kr // 2, 0, rows - span))
        qr = r0 + np.arange(NA_TR)
        row_start = np.clip(qr - kr // 2, 0, rows - kr)
        krow = start + np.arange(span)
        rv = (krow[None, :] >= row_start[:, None]) & (krow[None, :] < row_start[:, None] + kr)
        d_row = np.where(rv, krow[None, :] - qr[:, None] + WIN_R - 1, -1)
        for vi, p in enumerate(pats):
            if np.array_equal(p, d_row):
                var.append(vi)
                break
        else:
            var.append(len(pats))
            pats.append(d_row)
        starts.append(start)
    return pats, np.asarray(var, np.int32), np.asarray(starts, np.int32), span


def _na_col_tables(rpb):
    qc = np.arange(GRID_W)
    col_start = np.clip(qc - WIN_C // 2, 0, GRID_W - WIN_C)
    cv = (qc[None, :] >= col_start[:, None]) & (qc[None, :] < col_start[:, None] + WIN_C)
    d_col = qc[None, :] - qc[:, None] + WIN_C - 1
    onehot = (d_col[None] == np.arange(2 * WIN_C - 1)[:, None, None]) & cv[None]
    t = jnp.einsum("hrd,dqk->hrqk", rpb, jnp.asarray(onehot, F32), precision=lax.Precision.HIGHEST)
    t = jnp.where(jnp.asarray(cv)[None, None], t, NEG)
    return jnp.concatenate([t, t], axis=-1)


def _na_kernel(var_ref, start_ref, q_ref, k_ref, v_ref, kc_ref, vc_ref, tab_ref, o_ref, bias_ref,
               *, nkey, pats):
    t = pl.program_id(2)
    var = var_ref[t]
    changed = jnp.logical_or(t == 0, var != var_ref[jnp.maximum(t - 1, 0)])
    left = lax.broadcasted_iota(jnp.int32, (GRID_W, 2 * GRID_W), 1) < GRID_W
    neg = jnp.full((GRID_W, 2 * GRID_W), NEG, F32)

    for vi, d_row in enumerate(pats):
        @pl.when(jnp.logical_and(changed, var == vi))
        def _(d_row=d_row):
            for a in range(2):
                for i in range(d_row.shape[0]):
                    for j in range(0, d_row.shape[1], 2):
                        r0, r1 = int(d_row[i, j]), int(d_row[i, j + 1])
                        b0 = tab_ref[a, r0] if r0 >= 0 else neg
                        b1 = tab_ref[a, r1] if r1 >= 0 else neg
                        blk = neg if (r0 < 0 and r1 < 0) else jnp.where(left, b0, b1)
                        bias_ref[a, i * GRID_W:(i + 1) * GRID_W, j * GRID_W:(j + 2) * GRID_W] = blk

    st = pl.multiple_of(start_ref[t] * GRID_W, GRID_W)
    q = q_ref[...] * (NA_HD ** -0.5)
    kw = k_ref[pl.ds(st, nkey), :]
    vw = v_ref[pl.ds(st, nkey), :]
    kc = kc_ref[0]
    vc = vc_ref[0]
    outs = []
    for a in range(2):
        qa = jnp.where(_head_mask(a), q, jnp.zeros_like(q))
        s1 = _qk(qa, kw) + bias_ref[a]
        s2 = _qk(qa, kc)
        m = jnp.maximum(jnp.max(s1, axis=-1, keepdims=True), jnp.max(s2, axis=-1, keepdims=True))
        p1 = jnp.exp(s1 - m)
        p2 = jnp.exp(s2 - m)
        l = jnp.sum(p1, axis=-1, keepdims=True) + jnp.sum(p2, axis=-1, keepdims=True)
        o = (jnp.dot(p1.astype(BF16), vw, preferred_element_type=F32)
             + jnp.dot(p2.astype(BF16), vc, preferred_element_type=F32))
        outs.append(o / l)
    o_ref[...] = jnp.where(_head_mask(0), outs[0], outs[1]).astype(BF16)


def _na_latent(zb, kctx, vctx, rpb, nb, seq):
    rows = seq // GRID_W
    pats, var, starts, span = _na_geometry(rows)
    nkey = span * GRID_W
    tq = NA_TR * GRID_W
    ntile = rows // NA_TR
    hp = NA_HEADS // 2
    w = 2 * NA_HD
    past = kctx.shape[1]
    assert span % 2 == 0
    tabs = _na_col_tables(rpb)
    nrow = tabs.shape[1]
    grid_spec = pltpu.PrefetchScalarGridSpec(
        num_scalar_prefetch=2,
        grid=(nb, hp, ntile),
        in_specs=[
            pl.BlockSpec((tq, w), lambda b, p, t, *_: (b * ntile + t, ZB_NQ * hp + p)),
            pl.BlockSpec((seq, w), lambda b, p, t, *_: (b, ZB_NK * hp + p)),
            pl.BlockSpec((seq, w), lambda b, p, t, *_: (b, ZB_NV * hp + p)),
            pl.BlockSpec((1, past, w), lambda b, p, t, *_: (b, 0, p)),
            pl.BlockSpec((1, past, w), lambda b, p, t, *_: (b, 0, p)),
            pl.BlockSpec((2, nrow, GRID_W, 2 * GRID_W), lambda b, p, t, *_: (p, 0, 0, 0)),
        ],
        out_specs=pl.BlockSpec((tq, w), lambda b, p, t, *_: (b * ntile + t, p)),
        scratch_shapes=[pltpu.VMEM((2, tq, nkey), F32)],
    )
    return pl.pallas_call(
        functools.partial(_na_kernel, nkey=nkey, pats=pats),
        grid_spec=grid_spec,
        out_shape=jax.ShapeDtypeStruct((nb * seq, CH), BF16),
        compiler_params=_cparams(("parallel", "parallel", "arbitrary")),
        name="na_latent",
    )(jnp.asarray(var), jnp.asarray(starts), zb, zb, zb, kctx, vctx, tabs)


def _merge_kernel(x_ref, mod_ref, gate_ref, yft_ref, of_ref, ob_ref, hg_ref, on_ref, hn_ref,
                  wfo_ref, who_ref, wno_ref, wout_ref, o_ref):
    d = x_ref.shape[1]
    yft = jnp.concatenate([yft_ref[g] for g in range(FT_GROUPS)], axis=1)
    y_ft = jnp.dot(yft.astype(BF16), wfo_ref[...], preferred_element_type=F32)
    o = of_ref[...] + ob_ref[...]
    parts = []
    for h in range(HG_HEADS):
        oh = o[:, h * HG_D:(h + 1) * HG_D]
        ms = jnp.mean(oh * oh, axis=-1, keepdims=True)
        parts.append(oh * lax.rsqrt(ms + EPS))
    oh = jnp.concatenate(parts, axis=1) * hn_ref[...]
    oh = (oh * _silu(hg_ref[...].astype(F32))).astype(BF16)
    y_hg = jnp.dot(oh, who_ref[...], preferred_element_type=F32)
    y_na = jnp.dot(on_ref[...], wno_ref[...], preferred_element_type=F32)
    g = gate_ref[...].astype(F32)
    merged = (_sigmoid(g[:, :d]) * y_ft + _sigmoid(g[:, d:2 * d]) * y_hg
              + _sigmoid(g[:, 2 * d:]) * y_na)
    y = jnp.dot(merged.astype(BF16), wout_ref[...], preferred_element_type=F32)
    o_ref[...] = x_ref[...] + mod_ref[0, 2:3, :] * y


def _merge(x, mod, zb, yft, o_f, o_b, o_n, hg_norm, w_fo, w_ho, w_no, w_out, *, rows_per_seg):
    t, d = x.shape
    tm = ROW_TILE
    spt = rows_per_seg // tm
    assert 3 * d == 6 * CH, "the three merge gates fill the first six bf16 splits"
    row = lambda i: (i, 0)
    return pl.pallas_call(
        _merge_kernel,
        grid=(t // tm,),
        in_specs=[
            pl.BlockSpec((tm, d), row),
            pl.BlockSpec((1, 6, d), lambda i: (i // spt, 0, 0)),
            pl.BlockSpec((tm, 3 * d), row),
            pl.BlockSpec((FT_GROUPS, tm, FT_GD), lambda i: (0, i, 0)),
            pl.BlockSpec((tm, CH), row),
            pl.BlockSpec((tm, CH), row),
            pl.BlockSpec((tm, CH), lambda i: (i, ZB_HG)),
            pl.BlockSpec((tm, CH), row),
            _const_spec((1, CH)),
            _const_spec((CH, d)),
            _const_spec((CH, d)),
            _const_spec((CH, d)),
            _const_spec((d, d)),
        ],
        out_specs=pl.BlockSpec((tm, d), row),
        out_shape=jax.ShapeDtypeStruct((t, d), F32),
        compiler_params=_cparams(("parallel",)),
        name="merge",
    )(x, mod, zb, yft, o_f, o_b, zb, o_n, hg_norm.reshape(1, CH), w_fo, w_ho, w_no, w_out)


FF_CHUNK = 256


def _swiglu_acc(h, w1_ref, w3_ref, w2_ref, lead=()):
    dff = w1_ref.shape[-1]
    acc = None
    for c0 in range(0, dff, FF_CHUNK):
        cs = slice(c0, c0 + FF_CHUNK)
        g = jnp.dot(h, w1_ref[lead + (slice(None), cs)], preferred_element_type=F32)
        u = jnp.dot(h, w3_ref[lead + (slice(None), cs)], preferred_element_type=F32)
        a = (_silu(g) * u).astype(BF16)
        y = jnp.dot(a, w2_ref[lead + (cs, slice(None))], preferred_element_type=F32)
        acc = y if acc is None else acc + y
    return acc


def _ffn_kernel(x_ref, g_ref, mod_ref, w1_ref, w3_ref, w2_ref, *refs, final):
    o_ref = refs[-1]
    x = x_ref[...]
    h = _norm_mod(x, g_ref[...], mod_ref[0, 3:4, :], mod_ref[0, 4:5, :]).astype(BF16)
    y = x + mod_ref[0, 5:6, :] * _swiglu_acc(h, w1_ref, w3_ref, w2_ref)
    if final:
        ms = jnp.mean(y * y, axis=-1, keepdims=True)
        y = y * lax.rsqrt(ms + EPS) * refs[0][...]
    o_ref[...] = y


def _ffn_dense(x, g, mod, w1, w3, w2, norm_final, *, rows_per_seg):
    t, d = x.shape
    dff = w1.shape[1]
    tm = ROW_TILE
    spt = rows_per_seg // tm
    final = norm_final is not None
    in_specs = [
        pl.BlockSpec((tm, d), lambda i: (i, 0)),
        _const_spec((1, d)),
        pl.BlockSpec((1, 6, d), lambda i: (i // spt, 0, 0)),
        _const_spec((d, dff)),
        _const_spec((d, dff)),
        _const_spec((dff, d)),
    ]
    args = [x, g.reshape(1, d), mod, w1, w3, w2]
    if final:
        in_specs.append(_const_spec((1, d)))
        args.append(norm_final.reshape(1, d))
    return pl.pallas_call(
        functools.partial(_ffn_kernel, final=final),
        grid=(t // tm,),
        in_specs=in_specs,
        out_specs=pl.BlockSpec((tm, d), lambda i: (i, 0)),
        out_shape=jax.ShapeDtypeStruct((t, d), F32),
        compiler_params=_cparams(("parallel",)),
        name="ffn_dense",
    )(*args)


def _router_kernel(x_ref, g_ref, mod_ref, wr_ref, h_ref, r_ref):
    h = _norm_mod(x_ref[...], g_ref[...], mod_ref[0, 3:4, :], mod_ref[0, 4:5, :])
    h_ref[...] = h
    logits = jnp.dot(h, wr_ref[...], precision=lax.Precision.HIGHEST,
                     preferred_element_type=F32)
    lane = lax.broadcasted_iota(jnp.int32, logits.shape, 1)
    logits = jnp.where(lane < N_EXPERTS, logits, -jnp.inf)
    m1 = jnp.max(logits, axis=-1, keepdims=True)
    i1 = jnp.min(jnp.where(logits == m1, lane, LANES), axis=-1, keepdims=True)
    rest = jnp.where(lane == i1, -jnp.inf, logits)
    m2 = jnp.max(rest, axis=-1, keepdims=True)
    i2 = jnp.min(jnp.where(rest == m2, lane, LANES), axis=-1, keepdims=True)
    e2 = jnp.exp(m2 - m1)
    w1 = 1.0 / (1.0 + e2)
    w2 = e2 / (1.0 + e2)
    r = jnp.where(lane == 0, i1.astype(F32), jnp.where(lane == 1, i2.astype(F32),
                  jnp.where(lane == 2, w1, jnp.where(lane == 3, w2, 0.0))))
    r_ref[...] = r


def _router(x, g, mod, w_router, *, rows_per_seg):
    t, d = x.shape
    tm = ROW_TILE
    spt = rows_per_seg // tm
    wr =jnp.zeros((d, LANES), F32).at[:, :N_EXPERTS].set(w_router)
    return pl.pallas_call(
        _router_kernel,
        grid=(t // tm,),
        in_specs=[
            pl.BlockSpec((tm, d), lambda i: (i, 0)),
            _const_spec((1, d)),
            pl.BlockSpec((1, 6, d), lambda i: (i // spt, 0, 0)),
            _const_spec((d, LANES)),
        ],
        out_specs=[pl.BlockSpec((tm, d), lambda i: (i, 0)), pl.BlockSpec((tm, LANES), lambda i: (i, 0))],
        out_shape=[jax.ShapeDtypeStruct((t, d), F32), jax.ShapeDtypeStruct((t, LANES), F32)],
        compiler_params=_cparams(("parallel",)),
        name="router",
    )(x, g.reshape(1, d), mod, wr)


MOE_TM = 512


def _moe_ffn_kernel(te_ref, nt_ref, src_ref, h_hbm, w1_ref, w3_ref, w2_ref, y_ref, hbuf, sem):
    i = pl.program_id(0)
    nt = nt_ref[0]
    slot = i % 2

    def gather(tile, s):
        base = tile * MOE_TM

        def issue(r, carry):
            pltpu.make_async_copy(h_hbm.at[pl.ds(src_ref[base + r], 1)],
                                  hbuf.at[s, pl.ds(r, 1)], sem.at[s]).start()
            return carry

        lax.fori_loop(0, MOE_TM, issue, 0, unroll=8)

    @pl.when(jnp.logical_and(i == 0, nt > 0))
    def _():
        gather(0, 0)

    @pl.when(i < nt)
    def _():
        @pl.when(i + 1 < nt)
        def _():
            gather(i + 1, 1 - slot)

        pltpu.make_async_copy(h_hbm.at[pl.ds(0, MOE_TM)], hbuf.at[slot], sem.at[slot]).wait()
        y_ref[...] = _swiglu_acc(hbuf[slot].astype(BF16), w1_ref, w3_ref, w2_ref, lead=(0,))

    @pl.when(i >= nt)
    def _():
        y_ref[...] = jnp.zeros_like(y_ref)


def _moe_ffn(h, tile_expert, n_tiles, src, w1, w3, w2, ntile_max):
    t, d = h.shape
    dff = w1.shape[-1]
    grid_spec = pltpu.PrefetchScalarGridSpec(
        num_scalar_prefetch=3,
        grid=(ntile_max,),
        in_specs=[
            pl.BlockSpec(memory_space=pl.ANY),
            pl.BlockSpec((1, d, dff), lambda i, te, nt, src: (te[i], 0, 0)),
            pl.BlockSpec((1, d, dff), lambda i, te, nt, src: (te[i], 0, 0)),
            pl.BlockSpec((1, dff, d), lambda i, te, nt, src: (te[i], 0, 0)),
        ],
        out_specs=pl.BlockSpec((MOE_TM, d), lambda i, te, nt, src: (i, 0)),
        scratch_shapes=[pltpu.VMEM((2, MOE_TM, d), F32), pltpu.SemaphoreType.DMA((2,))],
    )
    return pl.pallas_call(
        _moe_ffn_kernel,
        grid_spec=grid_spec,
        out_shape=jax.ShapeDtypeStruct((ntile_max * MOE_TM, d), F32),
        compiler_params=_cparams(("arbitrary",)),
        name="moe_ffn",
    )(tile_expert, n_tiles, src, h, w1, w3, w2)


def _combine_kernel(pos_ref, x_ref, mod_ref, r_ref, y_hbm, *refs, final, tm):
    o_ref, ybuf, sem = refs[-3:]
    i = pl.program_id(0)
    slot = i % 2

    def gather(tile, s):
        base = 2 * tile * tm

        def issue(r, carry):
            for k in range(2):
                pltpu.make_async_copy(y_hbm.at[pl.ds(pos_ref[base + 2 * r + k], 1)],
                                      ybuf.at[s, k, pl.ds(r, 1)], sem.at[s]).start()
            return carry

        lax.fori_loop(0, tm, issue, 0, unroll=4)

    @pl.when(i == 0)
    def _():
        gather(0, 0)

    @pl.when(i + 1 < pl.num_programs(0))
    def _():
        gather(i + 1, 1 - slot)

    for k in range(2):
        pltpu.make_async_copy(y_hbm.at[pl.ds(0, tm)], ybuf.at[slot, k], sem.at[slot]).wait()
    r = r_ref[...]
    f = r[:, 2:3] * ybuf[slot, 0] + r[:, 3:4] * ybuf[slot, 1]
    y = x_ref[...] + mod_ref[0, 5:6, :] * f
    if final:
        ms = jnp.mean(y * y, axis=-1, keepdims=True)
        y = y * lax.rsqrt(ms + EPS) * refs[0][...]
    o_ref[...] = y


def _moe_combine(x, mod, route, pos, y_sorted, norm_final, *, rows_per_seg):
    t, d = x.shape
    tm = 256
    spt = rows_per_seg // tm
    final = norm_final is not None
    in_specs = [
        pl.BlockSpec((tm, d), lambda i, pos: (i, 0)),
        pl.BlockSpec((1, 6, d), lambda i, pos: (i // spt, 0, 0)),
        pl.BlockSpec((tm, LANES), lambda i, pos: (i, 0)),
        pl.BlockSpec(memory_space=pl.ANY),
    ]
    args = [x, mod, route, y_sorted]
    if final:
        in_specs.append(pl.BlockSpec((1, d), lambda i, pos: (0, 0)))
        args.append(norm_final.reshape(1, d))
    grid_spec = pltpu.PrefetchScalarGridSpec(
        num_scalar_prefetch=1,
        grid=(t // tm,),
        in_specs=in_specs,
        out_specs=pl.BlockSpec((tm, d), lambda i, pos: (i, 0)),
        scratch_shapes=[pltpu.VMEM((2, 2, tm, d), F32), pltpu.SemaphoreType.DMA((2,))],
    )
    return pl.pallas_call(
        functools.partial(_combine_kernel, final=final, tm=tm),
        grid_spec=grid_spec,
        out_shape=jax.ShapeDtypeStruct((t, d), F32),
        compiler_params=_cparams(("arbitrary",)),
        name="moe_combine",
    )(pos.reshape(-1), *args)


def _moe(x, g, mod, w_router, w1, w3, w2, norm_final, *, rows_per_seg):
    t, d = x.shape
    h, route = _router(x, g, mod, w_router, rows_per_seg=rows_per_seg)
    ids = route[:, :2].astype(jnp.int32)
    onehot = (ids[:, :, None] == jnp.arange(N_EXPERTS)[None, None, :]).astype(jnp.int32)
    sel = onehot.sum(axis=1)
    counts = sel.sum(axis=0)
    rank = jnp.cumsum(sel, axis=0) - sel
    padded = ((counts + MOE_TM - 1) // MOE_TM) * MOE_TM
    pstart = jnp.cumsum(padded) - padded
    cstart = jnp.cumsum(counts) - counts
    pos_e = pstart[None, :] + rank
    pos = jnp.take_along_axis(pos_e, ids, axis=1).astype(jnp.int32)
    ntile_max = (2 * t) // MOE_TM + N_EXPERTS
    tok = jnp.arange(t, dtype=jnp.int32)
    key = jnp.sort((ids * t + tok[:, None]).reshape(-1))
    sorted_tok = key % t
    tile_start = jnp.arange(ntile_max, dtype=jnp.int32) * MOE_TM
    pend = pstart + padded
    tile_expert = jnp.minimum(jnp.sum(tile_start[:, None] >= pend[None, :], axis=1), N_EXPERTS - 1)
    n_tiles = (jnp.sum(padded) // MOE_TM).astype(jnp.int32).reshape(1)
    p = jnp.arange(ntile_max * MOE_TM, dtype=jnp.int32)
    e_p = jnp.repeat(tile_expert, MOE_TM)
    r_p = jnp.minimum(p - pstart[e_p], jnp.maximum(counts[e_p] - 1, 0))
    src = sorted_tok[jnp.clip(cstart[e_p] + r_p, 0, 2 * t - 1)].astype(jnp.int32)
    y_sorted = _moe_ffn(h, tile_expert.astype(jnp.int32), n_tiles, src, w1, w3, w2, ntile_max)
    return _moe_combine(x, mod, route, pos, y_sorted, norm_final, rows_per_seg=rows_per_seg)


def _layer(x, mod, l, depth, p, *, nb, seq, ctx, caches, state_out, norm_final):
    is_ctx = ctx is None
    seg = nb * seq if is_ctx else seq
    outs = _in_proj(x, p["norm_mix"][l], mod, p["w_in"][l], rows_per_seg=seg, seq=seq, layer=l,
                    depth=depth, caches=caches if is_ctx else None)
    zb, zf = outs[0], outs[1]
    new_caches = tuple(outs[2:]) if is_ctx else None
    if is_ctx:
        yft = _fourier_ctx(zb, nb, seq)
        o_n = _attn_ctx(zb, nb, seq)
    else:
        yft = _fourier_lat(zb, nb, seq)
        o_n = _na_latent(zb, ctx[0], ctx[1], p["na_rpb"][l], nb, seq)
    hg_c = 128
    lbr = 256 if is_ctx else 512
    lbr = min(lbr, seq)
    o_f, state_out = _hgrn(zb, zf, p["lbp"][l][0], None if is_ctx else ctx[2],
                           state_out if is_ctx else None, nb=nb, seq=seq, rev=False, layer=l,
                           depth=depth, c=hg_c, lb_rows=lbr)
    o_b, state_out = _hgrn(zb, zf, p["lbp"][l][1], None if is_ctx else ctx[3],
                           state_out if is_ctx else None, nb=nb, seq=seq, rev=True, layer=l,
                           depth=depth, c=hg_c, lb_rows=lbr)
    x = _merge(x, mod, zb, yft, o_f, o_b, o_n, p["hg_norm"][l], p["w_fo"][l], p["w_ho"][l],
               p["w_no"][l], p["w_out"][l], rows_per_seg=seg)
    i = l // 2
    if l % 2 == 0:
        x = _ffn_dense(x, p["norm_ffn"][l], mod, p["w1_d"][i], p["w3_d"][i], p["w2_d"][i], norm_final,
                       rows_per_seg=seg)
    else:
        x = _moe(x, p["norm_ffn"][l], mod, p["w_router"][i], p["w1_e"][i], p["w3_e"][i], p["w2_e"][i],
                 norm_final, rows_per_seg=seg)
    return x, new_caches, state_out


def kernel(x_prompt, x_sample, cache_k, cache_v, state_hgrn, c, c_ctx, w_mod, b_mod, norm_mix,
           norm_ffn, w_in, hg_lb, hg_norm, na_rpb, w_fo, w_ho, w_no, w_out, w1_d, w3_d, w2_d,
           w_router, w1_e, w3_e, w2_e, norm_final):
    nb_c, seq_c, d = x_prompt.shape
    nb_l, seq_l, _ = x_sample.shape
    depth = w_in.shape[0]
    past = cache_k.shape[2]

    lbs = jnp.cumsum(jax.nn.softmax(hg_lb.astype(F32), axis=0), axis=0)
    lbs = lbs - lbs[0:1]
    lbp = jnp.stack([jnp.log(lbs), jnp.log1p(-lbs)], axis=2)

    bf = lambda w: w.astype(BF16)
    p = dict(norm_mix=norm_mix, norm_ffn=norm_ffn, w_in=bf(w_in), hg_norm=hg_norm, na_rpb=na_rpb,
             w_fo=bf(w_fo), w_ho=bf(w_ho), w_no=bf(w_no), w_out=bf(w_out), w1_d=bf(w1_d),
             w3_d=bf(w3_d), w2_d=bf(w2_d), w_router=w_router, w1_e=bf(w1_e), w3_e=bf(w3_e),
             w2_e=bf(w2_e), lbp=lbp)

    mods = _mod_vectors(jnp.concatenate([c_ctx[None], c], axis=0), w_mod, b_mod)

    x = x_prompt.reshape(nb_c * seq_c, d)
    caches, state_out = (None, None), None
    for l in range(depth):
        x, caches, state_out = _layer(x, mods[l, 0:1], l, depth, p, nb=nb_c, seq=seq_c, ctx=None,
                                      caches=caches, state_out=state_out,
                                      norm_final=norm_final if l == depth - 1 else None)
    y_prompt = x.reshape(nb_c, seq_c, d)

    xs = x_sample.reshape(nb_l * seq_l, d)
    for l in range(depth):
        ctx = (bf(cache_k[:, l].reshape(nb_l, past, CH)), bf(cache_v[:, l].reshape(nb_l, past, CH)),
               state_hgrn[:, l, 0], state_hgrn[:, l, 1])
        xs, _, _ = _layer(xs, mods[l, 1:], l, depth, p, nb=nb_l, seq=seq_l, ctx=ctx, caches=None,
                          state_out=None, norm_final=norm_final if l == depth - 1 else None)
    y_sample = xs.reshape(nb_l, seq_l, d)

    new_cache_k = caches[0].reshape(nb_c, depth, seq_c, NA_HEADS, NA_HD)
    new_cache_v = caches[1].reshape(nb_c, depth, seq_c, NA_HEADS, NA_HD)
    return (y_prompt, y_sample, new_cache_k, new_cache_v, state_out)
```

```python
import functools
import math

import numpy as np
import jax
import jax.numpy as jnp
from jax import lax
from jax.experimental import pallas as pl
from jax.experimental.pallas import tpu as pltpu

F32 = jnp.float32
BF16 = jnp.bfloat16

LANES = 128
SUBLANES = 8
VMEM_LIMIT = 56 * 1024 * 1024
ROW_TILE = 512

FT_GROUPS = 4
FT_GD = 128
HG_HEADS = 4
HG_D = 128
NA_HEADS = 8
NA_HD = 64
GRID_W = 64
WIN_R = 8
WIN_C = 16
N_EXPERTS = 8
EPS = 1e-6
CH = 512
NEG = -1e30

ZB_SRC = (9, 10, 11, 12, 13, 14, 0, 1, 4, 5, 6, 7, 8)
ZB_UFT, ZB_HQ, ZB_HI, ZB_HG, ZB_NQ, ZB_NK, ZB_NV = 6, 7, 8, 9, 10, 11, 12
ZF_SRC = (2, 3)
SRC_NK, SRC_NV = 7, 8


def _cparams(sem):
    return pltpu.CompilerParams(dimension_semantics=sem, vmem_limit_bytes=VMEM_LIMIT)


def _const_spec(shape):
    nd = len(shape)
    return pl.BlockSpec(shape, lambda *_: (0,) * nd, pipeline_mode=pl.Buffered(1))


def _silu(x):
    return x * (1.0 / (1.0 + jnp.exp(-x)))


def _sigmoid(x):
    return 1.0 / (1.0 + jnp.exp(-x))


def _norm_mod(x, g, shift, scale):
    ms = jnp.mean(x * x, axis=-1, keepdims=True)
    return (x * lax.rsqrt(ms + EPS) * g) * (1.0 + scale) + shift


def _mod_kernel(c_ref, w_ref, b_ref, o_ref, *, nv, tn):
    w = w_ref[0]
    for v in range(nv):
        s = _silu(c_ref[v])
        s = jnp.concatenate([s] * (tn // LANES), axis=1)
        o_ref[0, v:v + 1, :] = jnp.sum(w * s, axis=0, keepdims=True) + b_ref[0]


def _mod_vectors(cvecs, w_mod, b_mod):
    nv, d = cvecs.shape
    depth, _, n6 = w_mod.shape
    tn = 512
    cb = jnp.broadcast_to(cvecs[:, :, None], (nv, d, LANES))
    out = pl.pallas_call(
        functools.partial(_mod_kernel, nv=nv, tn=tn),
        grid=(depth, n6 // tn),
        in_specs=[
            pl.BlockSpec((nv, d, LANES), lambda l, j: (0, 0, 0)),
            pl.BlockSpec((1, d, tn), lambda l, j: (l, 0, j)),
            pl.BlockSpec((1, 1, tn), lambda l, j: (l, 0, j)),
        ],
        out_specs=pl.BlockSpec((1, nv, tn), lambda l, j: (l, 0, j)),
        out_shape=jax.ShapeDtypeStruct((depth, nv, n6), F32),
        compiler_params=_cparams(("parallel", "parallel")),
        name="mod_vectors",
    )(cb, w_mod, b_mod.reshape(depth, 1, n6))
    return out.reshape(depth, nv, 6, d)


def _in_kernel(x_ref, g_ref, mod_ref, w_ref, *refs, with_cache):
    if with_cache:
        zb_ref, zf_ref, ck_ref, cv_ref = refs[-4:]
    else:
        zb_ref, zf_ref = refs[-2:]
    h = _norm_mod(x_ref[...], g_ref[...], mod_ref[0, 0:1, :], mod_ref[0, 1:2, :]).astype(BF16)
    for src in range(15):
        z = jnp.dot(h, w_ref[:, src * CH:(src + 1) * CH], preferred_element_type=F32)
        if src in ZB_SRC:
            j = ZB_SRC.index(src)
            zb_ref[:, j * CH:(j + 1) * CH] = z.astype(BF16)
        if src in ZF_SRC:
            j = ZF_SRC.index(src)
            zf_ref[:, j * CH:(j + 1) * CH] = z
        if with_cache and src in (SRC_NK, SRC_NV):
            c_ref = ck_ref if src == SRC_NK else cv_ref
            nseq, _, seq, _ = c_ref.shape
            for s in range(nseq):
                c_ref[s, 0] = z[s * seq:(s + 1) * seq]


def _in_proj(x, g, mod, w_bf, *, rows_per_seg, seq, layer, depth, caches):
    t, d = x.shape
    tm = ROW_TILE
    with_cache = caches is not None
    spt = rows_per_seg // tm
    in_specs = [
        pl.BlockSpec((tm, d), lambda i: (i, 0)),
        _const_spec((1, d)),
        pl.BlockSpec((1, 6, d), lambda i: (i // spt, 0, 0)),
        _const_spec((d, 15 * CH)),
    ]
    args = [x, g.reshape(1, d), mod, w_bf]
    out_specs = [
        pl.BlockSpec((tm, len(ZB_SRC) * CH), lambda i: (i, 0)),
        pl.BlockSpec((tm, len(ZF_SRC) * CH), lambda i: (i, 0)),
    ]
    out_shape = [
        jax.ShapeDtypeStruct((t, len(ZB_SRC) * CH), BF16),
        jax.ShapeDtypeStruct((t, len(ZF_SRC) * CH), F32),
    ]
    aliases = {}
    if with_cache:
        assert tm % seq == 0, "a row tile holds whole context sequences"
        cspec = pl.BlockSpec((tm // seq, 1, seq, CH), lambda i: (i, layer, 0, 0))
        out_specs += [cspec, cspec]
        cshape = jax.ShapeDtypeStruct((t // seq, depth, seq, CH), F32)
        out_shape += [cshape, cshape]
        if caches[0] is not None:
            in_specs += [pl.BlockSpec(memory_space=pl.ANY)] * 2
            args += list(caches)
            aliases = {4: 2, 5: 3}
    return pl.pallas_call(
        functools.partial(_in_kernel, with_cache=with_cache),
        grid=(t // tm,),
        in_specs=in_specs,
        out_specs=out_specs,
        out_shape=out_shape,
        input_output_aliases=aliases,
        compiler_params=_cparams(("parallel",)),
        name="in_proj",
    )(*args)


def _dft_tables(n, scale):
    k = np.arange(n)
    ang = 2.0 * np.pi * ((k[:, None] * k[None, :]) % n) / n
    return np.cos(ang) * scale, np.sin(ang) * scale


def _ft_ctx_kernel(u_ref, cs_ref, lm_ref, y_ref):
    u = u_ref[...]
    cs = cs_ref[...].astype(BF16)
    zc, zs = [], []
    for g in range(FT_GROUPS):
        z = jnp.dot(u[:, g * FT_GD:(g + 1) * FT_GD], cs, preferred_element_type=F32)
        zc.append(z[:, :FT_GD])
        zs.append(z[:, FT_GD:])
    zz = jnp.concatenate([jnp.concatenate(zc, axis=1), jnp.concatenate(zs, axis=1)], axis=0)
    y = jnp.dot(lm_ref[...].astype(BF16), zz.astype(BF16), preferred_element_type=F32)
    for g in range(FT_GROUPS):
        y_ref[g] = y[:, g * FT_GD:(g + 1) * FT_GD]


def _fourier_ctx(zb, nb, seq):
    cc, sc = _dft_tables(FT_GD, FT_GD ** -0.5)
    cl, sl = _dft_tables(seq, seq ** -0.5)
    cs = jnp.asarray(np.concatenate([cc, sc], axis=1), F32)
    lm = jnp.asarray(np.concatenate([cl, -sl], axis=1), F32)
    return pl.pallas_call(
        _ft_ctx_kernel,
        grid=(nb,),
        in_specs=[
            pl.BlockSpec((seq, CH), lambda b: (b, ZB_UFT)),
            _const_spec(cs.shape),
            _const_spec(lm.shape),
        ],
        out_specs=pl.BlockSpec((FT_GROUPS, seq, FT_GD), lambda b: (0, b, 0)),
        out_shape=jax.ShapeDtypeStruct((FT_GROUPS, nb * seq, FT_GD), F32),
        compiler_params=_cparams(("parallel",)),
        name="fourier_ctx",
    )(zb, cs, lm)


def _ft_lat_kernel(u_ref, cs_ref, w1_ref, twc_ref, tws_ref, w2_ref, y_ref, zr_ref, zi_ref,
                   ar_ref, ai_ref, *, n, lw):
    rows = n * n
    rc = min(rows, 512)
    ng = lw // FT_GD
    cs = cs_ref[...].astype(BF16)
    w1 = w1_ref[...].astype(BF16)
    w2 = w2_ref[...].astype(BF16)
    for r0 in range(0, rows, rc):
        for g in range(ng):
            z = jnp.dot(u_ref[r0:r0 + rc, g * FT_GD:(g + 1) * FT_GD], cs,
                        preferred_element_type=F32)
            zr_ref[g, r0:r0 + rc, :] = z[:, :FT_GD]
            zi_ref[g, r0:r0 + rc, :] = -z[:, FT_GD:]

    def gather(re_ref, im_ref, idx):
        xr = jnp.concatenate([re_ref[g, idx, :] for g in range(ng)], axis=1)
        xi = jnp.concatenate([im_ref[g, idx, :] for g in range(ng)], axis=1)
        return jnp.concatenate([xr, xi], axis=0).astype(BF16)

    def stage1(a, carry):
        idx = pl.ds(a, n, stride=n)
        y = jnp.dot(w1, gather(zr_ref, zi_ref, idx), preferred_element_type=F32)
        yr, yi = y[:n], y[n:]
        tc = jnp.concatenate([twc_ref[a]] * ng, axis=1)
        ts = jnp.concatenate([tws_ref[a]] * ng, axis=1)
        zr = yr * tc - yi * ts
        zi = yr * ts + yi * tc
        for g in range(ng):
            ar_ref[g, idx, :] = zr[:, g * FT_GD:(g + 1) * FT_GD]
            ai_ref[g, idx, :] = zi[:, g * FT_GD:(g + 1) * FT_GD]
        return carry

    lax.fori_loop(0, n, stage1, 0, unroll=4)

    def stage2(q, carry):
        r0 = pl.multiple_of(q * n, n)
        y = jnp.dot(w2, gather(ar_ref, ai_ref, pl.ds(r0, n)), preferred_element_type=F32)
        for g in range(ng):
            y_ref[g, pl.ds(q, n, stride=n), :] = y[:, g * FT_GD:(g + 1) * FT_GD]
        return carry

    lax.fori_loop(0, n, stage2, 0, unroll=4)


def _fourier_lat(zb, nb, seq):
    n = math.isqrt(seq)
    assert n * n == seq and n % SUBLANES == 0
    lw = 256
    cc, sc = _dft_tables(FT_GD, FT_GD ** -0.5)
    cs = jnp.asarray(np.concatenate([cc, sc], axis=1), F32)
    pr, ps = _dft_tables(n, 1.0 / n)
    w1 = jnp.asarray(np.block([[pr, ps], [-ps, pr]]), F32)
    k = np.arange(n)
    ang = 2.0 * np.pi * (k[:, None] * k[None, :]) / seq
    twc = jnp.asarray(np.broadcast_to(np.cos(ang)[:, :, None], (n, n, LANES)), F32)
    tws = jnp.asarray(np.broadcast_to(-np.sin(ang)[:, :, None], (n, n, LANES)), F32)
    wr, ws = _dft_tables(n, 1.0)
    w2 = jnp.asarray(np.concatenate([wr, ws], axis=1), F32)
    nh = CH // lw
    return pl.pallas_call(
        functools.partial(_ft_lat_kernel, n=n, lw=lw),
        grid=(nb, nh),
        in_specs=[
            pl.BlockSpec((seq, lw), lambda b, h: (b, ZB_UFT * nh + h)),
            _const_spec(cs.shape),
            _const_spec(w1.shape),
            _const_spec(twc.shape),
            _const_spec(tws.shape),
            _const_spec(w2.shape),
        ],
        out_specs=pl.BlockSpec((lw // FT_GD, seq, FT_GD), lambda b, h: (h, b, 0)),
        out_shape=jax.ShapeDtypeStruct((FT_GROUPS, nb * seq, FT_GD), F32),
        scratch_shapes=[pltpu.VMEM((lw // FT_GD, seq, FT_GD), F32)] * 4,
        compiler_params=_cparams(("parallel", "parallel")),
        name="fourier_lat",
    )(zb, cs, w1, twc, tws, w2)


HG_SAFE = 160.0


def _hgrn_tables(c, rev):
    t = np.arange(c)
    tri = (t[None, :] >= t[:, None]) if rev else (t[None, :] <= t[:, None])
    lv = np.full((c, c), -1, np.int32)
    n, i = 1, 0
    while n < c:
        same = (t[:, None] // (2 * n)) == (t[None, :] // (2 * n))
        hi_t, hi_s = (t[:, None] % (2 * n)) >= n, (t[None, :] % (2 * n)) >= n
        pair = same & (~hi_t & hi_s if rev else hi_t & ~hi_s)
        lv[pair] = i
        n, i = 2 * n, i + 1
    return tri.astype(np.float32), lv


def _split3(x):
    hi = x.astype(BF16)
    r = x - hi.astype(F32)
    mid = r.astype(BF16)
    lo = (r - mid.astype(F32)).astype(BF16)
    return hi, mid, lo


def _hgrn_kernel(hq_ref, hf_ref, hi_ref, lb_ref, tri_ref, lv_ref, *refs, c, nch, rev, has_s0):
    if has_s0:
        s0_ref, o_ref, st_ref, b_ref, k_ref, q_ref = refs
    else:
        o_ref, so_ref, st_ref, b_ref, k_ref, q_ref = refs
    j = pl.program_id(1)

    @pl.when(j == 0)
    def _():
        for h in range(HG_HEADS):
            st_ref[h] = s0_ref[0, h].T if has_s0 else jnp.zeros((HG_D, HG_D), F32)

    la = lb_ref[0:1, :]
    l1 = lb_ref[1:2, :]
    lv = lv_ref[...]
    row8 = lax.broadcasted_iota(jnp.int32, (c // SUBLANES, SUBLANES, HG_D), 1)
    ti = lax.broadcasted_iota(jnp.int32, (c, c), 0)
    si = lax.broadcasted_iota(jnp.int32, (c, c), 1)
    causal = (si >= ti) if rev else (si <= ti)
    half = c // 2
    same_half = jnp.logical_and(causal, (ti < half) == (si < half))
    cross_half = jnp.logical_and(causal, (ti < half) != (si < half))
    row = lax.broadcasted_iota(jnp.int32, (c, HG_D), 0)
    first_rows = (row >= half) if rev else (row < half)
    bl_row = c - 1 if not rev else 0
    mid_row = half if rev else half - 1

    tmax = jnp.float32(0.0)
    for cc in range(nch):
        rows = slice(cc * c, (cc + 1) * c)
        z = hf_ref[rows, :]
        ls = jnp.minimum(z, 0.0) - jnp.log(1.0 + jnp.exp(-jnp.abs(z)))
        cc_ = l1 + ls
        logf = jnp.maximum(la, cc_) + jnp.log(1.0 + jnp.exp(-jnp.abs(la - cc_)))
        k_ref[rows, :] = 1.0 - jnp.exp(logf)
        b = sum(jnp.dot(tri_ref[...], p, preferred_element_type=F32) for p in _split3(logf))
        b_ref[rows, :] = b
        q_ref[rows, :] = _silu(hq_ref[rows, :].astype(F32))
        bm = b[mid_row:mid_row + 1, :]
        bl = b[bl_row:bl_row + 1, :]
        tmax = jnp.maximum(tmax, jnp.maximum(jnp.max(jnp.abs(bm)), jnp.max(jnp.abs(bl - bm))))
    safe = tmax <= HG_SAFE

    def chunk(cc, fast):
        rows = slice(cc * c, (cc + 1) * c)
        b = b_ref[rows, :]
        kk = k_ref[rows, :]
        q = q_ref[rows, :]
        v = hi_ref[rows, :]
        blast = b_ref[cc * c + bl_row:cc * c + bl_row + 1, :]
        bmid = b_ref[cc * c + mid_row:cc * c + mid_row + 1, :]
        brefs = {}
        n = SUBLANES
        while not fast and n < c:
            parts = []
            for m in range(c // (2 * n)):
                r = cc * c + 2 * n * m + (n if rev else n - 1)
                parts.append(jnp.broadcast_to(b_ref[r:r + 1, :], (2 * n, HG_HEADS * HG_D)))
            brefs[n] = jnp.concatenate(parts, axis=0) if len(parts) > 1 else parts[0]
            n *= 2
        outs = []
        for h in range(HG_HEADS):
            hs = slice(h * HG_D, (h + 1) * HG_D)
            qh, kh, bh, vh, blh = q[:, hs], kk[:, hs], b[:, hs], v[:, hs], blast[:, hs]
            st = st_ref[h]
            if fast:
                bmh = bmid[:, hs]
                r1 = 0.5 * bmh
                r2 = 0.5 * (bmh + blh)
                ref = jnp.where(first_rows, r1, r2)
                qn = qh * jnp.exp(bh - ref)
                kn = kh * jnp.exp(ref - bh)
                sn = lax.dot_general(qn.astype(BF16), kn.astype(BF16), (((1,), (1,)), ((), ())),
                                     preferred_element_type=F32)
                e = jnp.exp(-jnp.abs(bh - bmh))
                sx = lax.dot_general((qh * e).astype(BF16), (kh * e).astype(BF16),
                                     (((1,), (1,)), ((), ())), preferred_element_type=F32)
                a = jnp.where(same_half, sn, jnp.where(cross_half, sx, 0.0))
                qe = (qn * jnp.where(first_rows, jnp.exp(r1), jnp.exp(r2))).astype(BF16)
                ke = (kn * jnp.where(first_rows, jnp.exp(blh - r1), jnp.exp(blh - r2))).astype(BF16)
                o = lax.dot_general(qe, st.astype(BF16), (((1,), (1,)), ((), ())),
                                    preferred_element_type=F32)
                o = o + jnp.dot(a.astype(BF16), vh, preferred_element_type=F32)
            else:
                o, ke = _hgrn_levels(qh, kh, bh, vh, blh, st, brefs, hs, lv, row8, c, rev)
            outs.append(o)
            kv = lax.dot_general(vh, ke, (((0,), (0,)), ((), ())), preferred_element_type=F32)
            st_ref[h] = st * jnp.exp(blh) + kv
        o_ref[rows, :] = jnp.concatenate(outs, axis=1)

    order = [nch - 1 - ci if rev else ci for ci in range(nch)]

    @pl.when(safe)
    def _():
        for cc in order:
            chunk(cc, True)

    @pl.when(jnp.logical_not(safe))
    def _():
        for cc in order:
            chunk(cc, False)

    if not has_s0:
        @pl.when(j == pl.num_programs(1) - 1)
        def _():
            for h in range(HG_HEADS):
                so_ref[0, 0, 0, h] = st_ref[h].T


def _hgrn_levels(qh, kh, bh, vh, blh, st, brefs, hs, lv, row8, c, rev):
    qe = (qh * jnp.exp(bh)).astype(BF16)
    o = lax.dot_general(qe, st.astype(BF16), (((1,), (1,)), ((), ())), preferred_element_type=F32)
    b3 = bh.reshape(c // SUBLANES, SUBLANES, HG_D)
    a = jnp.zeros((c, c), F32)
    n, i = 1, 0
    while n < c:
        if n < SUBLANES:
            bref = None
            for g in range(SUBLANES // (2 * n)):
                r = 2 * n * g + (n if rev else n - 1)
                bc = jnp.broadcast_to(b3[:, r:r + 1, :], b3.shape)
                bref = bc if bref is None else jnp.where(row8 >= 2 * n * g, bc, bref)
            bref = bref.reshape(c, HG_D)
        else:
            bref = brefs[n][:, hs]
        e = jnp.exp(-jnp.abs(bh - bref))
        sn = lax.dot_general((qh * e).astype(BF16), (kh * e).astype(BF16),
                             (((1,), (1,)), ((), ())), preferred_element_type=F32)
        a = jnp.where(lv == i, sn, a)
        n, i = 2 * n, i + 1
    o = o + jnp.dot(a.astype(BF16), vh, preferred_element_type=F32)
    o = o + jnp.sum(qh * kh, axis=-1, keepdims=True) * vh.astype(F32)
    return o, (kh * jnp.exp(blh - bh)).astype(BF16)


def _hgrn(zb, zf, lbp, s0, state_out, *, nb, seq, rev, layer, depth, c, lb_rows):
    t = zb.shape[0]
    nblk = seq // lb_rows
    nch = lb_rows // c
    tri, lv = _hgrn_tables(c, rev)
    d = int(rev)

    def rowmap(b, j):
        return b * nblk + (nblk - 1 - j if rev else j)

    in_specs = [
        pl.BlockSpec((lb_rows, CH), lambda b, j: (rowmap(b, j), ZB_HQ)),
        pl.BlockSpec((lb_rows, CH), lambda b, j: (rowmap(b, j), d)),
        pl.BlockSpec((lb_rows, CH), lambda b, j: (rowmap(b, j), ZB_HI)),
        _const_spec((2, CH)),
        _const_spec((c, c)),
        _const_spec((c, c)),
    ]
    args = [zb, zf, zb, lbp, jnp.asarray(tri, BF16), jnp.asarray(lv)]
    has_s0 = s0 is not None
    if has_s0:
        in_specs.append(pl.BlockSpec((1, HG_HEADS, HG_D, HG_D), lambda b, j: (b, 0, 0, 0)))
        args.append(s0)
    sshape = (nb, depth, 2, HG_HEADS, HG_D, HG_D)
    aliases = {}
    if state_out is not None:
        in_specs.append(pl.BlockSpec(memory_space=pl.ANY))
        args.append(state_out)
        aliases = {len(args) - 1: 1}
    kern = functools.partial(_hgrn_kernel, c=c, nch=nch, rev=rev, has_s0=has_s0)
    if state_out is not None:
        kern = functools.partial(_drop_ref, kern, len(args) - 1)
    out_specs = [pl.BlockSpec((lb_rows, CH), lambda b, j: (rowmap(b, j), 0))]
    out_shape = [jax.ShapeDtypeStruct((t, CH), F32)]
    if not has_s0:
        out_specs.append(pl.BlockSpec((1, 1, 1, HG_HEADS, HG_D, HG_D), lambda b, j: (b, layer, d, 0, 0, 0)))
        out_shape.append(jax.ShapeDtypeStruct(sshape, F32))
    outs = pl.pallas_call(
        kern,
        grid=(nb, nblk),
        in_specs=in_specs,
        out_specs=out_specs,
        out_shape=out_shape,
        scratch_shapes=[pltpu.VMEM((HG_HEADS, HG_D, HG_D), F32)] + [pltpu.VMEM((lb_rows, CH), F32)] * 3,
        input_output_aliases=aliases,
        compiler_params=_cparams(("parallel", "arbitrary")),
        name="hgrn_bwd" if rev else "hgrn_fwd",
    )(*args)
    return (outs[0], None) if has_s0 else tuple(outs)


def _drop_ref(kern, pos, *refs):
    return kern(*refs[:pos], *refs[pos + 1:])


def _head_mask(a):
    lane = lax.broadcasted_iota(jnp.int32, (1, 2 * NA_HD), 1)
    return (lane // NA_HD) == a


def _qk(q, k):
    return lax.dot_general(q, k, (((1,), (1,)), ((), ())), preferred_element_type=F32)


def _attn_ctx_kernel(q_ref, k_ref, v_ref, o_ref):
    w = 2 * NA_HD
    for pr in range(NA_HEADS // 2):
        ps = slice(pr * w, (pr + 1) * w)
        q = q_ref[:, ps] * (NA_HD ** -0.5)
        k = k_ref[:, ps]
        v = v_ref[:, ps]
        outs = []
        for a in range(2):
            s = _qk(jnp.where(_head_mask(a), q, jnp.zeros_like(q)), k)
            p = jnp.exp(s - jnp.max(s, axis=-1, keepdims=True))
            l = jnp.sum(p, axis=-1, keepdims=True)
            outs.append(jnp.dot(p.astype(BF16), v, preferred_element_type=F32) / l)
        o_ref[:, ps] = jnp.where(_head_mask(0), outs[0], outs[1]).astype(BF16)


def _attn_ctx(zb, nb, seq):
    return pl.pallas_call(
        _attn_ctx_kernel,
        grid=(nb,),
        in_specs=[
            pl.BlockSpec((seq, CH), lambda b: (b, ZB_NQ)),
            pl.BlockSpec((seq, CH), lambda b: (b, ZB_NK)),
            pl.BlockSpec((seq, CH), lambda b: (b, ZB_NV)),
        ],
        out_specs=pl.BlockSpec((seq, CH), lambda b: (b, 0)),
        out_shape=jax.ShapeDtypeStruct((nb * seq, CH), BF16),
        compiler_params=_cparams(("parallel",)),
        name="attn_ctx",
    )(zb, zb, zb)


NA_TR = 8


def _na_geometry(rows):
    kr = min(WIN_R, rows)
    span = min(NA_TR + WIN_R, rows)
    ntile = rows // NA_TR
    pats, var, starts = [], [], []
    for t in range(ntile):
        r0 = t * NA_TR
        start = int(np.clip(r0 - kr // 2, 0, rows - span))
        qr = r0 + np.arange(NA_TR)
        row_start = np.clip(qr - kr // 2, 0, rows - kr)
        krow = start + np.arange(span)
        rv = (krow[None, :] >= row_start[:, None]) & (krow[None, :] < row_start[:, None] + kr)
        d_row = np.where(rv, krow[None, :] - qr[:, None] + WIN_R - 1, -1)
        for vi, p in enumerate(pats):
            if np.array_equal(p, d_row):
                var.append(vi)
                break
        else:
            var.append(len(pats))
            pats.append(d_row)
        starts.append(start)
    return pats, np.asarray(var, np.int32), np.asarray(starts, np.int32), span


def _na_col_tables(rpb):
    qc = np.arange(GRID_W)
    col_start = np.clip(qc - WIN_C // 2, 0, GRID_W - WIN_C)
    cv = (qc[None, :] >= col_start[:, None]) & (qc[None, :] < col_start[:, None] + WIN_C)
    d_col = qc[None, :] - qc[:, None] + WIN_C - 1
    onehot = (d_col[None] == np.arange(2 * WIN_C - 1)[:, None, None]) & cv[None]
    t = jnp.einsum("hrd,dqk->hrqk", rpb, jnp.asarray(onehot, F32), precision=lax.Precision.HIGHEST)
    t = jnp.where(jnp.asarray(cv)[None, None], t, NEG)
    return jnp.concatenate([t, t], axis=-1)


def _na_kernel(var_ref, start_ref, q_ref, k_ref, v_ref, kc_ref, vc_ref, tab_ref, o_ref, bias_ref,
               *, nkey, pats):
    t = pl.program_id(2)
    var = var_ref[t]
    changed = jnp.logical_or(t == 0, var != var_ref[jnp.maximum(t - 1, 0)])
    left = lax.broadcasted_iota(jnp.int32, (GRID_W, 2 * GRID_W), 1) < GRID_W
    neg = jnp.full((GRID_W, 2 * GRID_W), NEG, F32)

    for vi, d_row in enumerate(pats):
        @pl.when(jnp.logical_and(changed, var == vi))
        def _(d_row=d_row):
            for a in range(2):
                for i in range(d_row.shape[0]):
                    for j in range(0, d_row.shape[1], 2):
                        r0, r1 = int(d_row[i, j]), int(d_row[i, j + 1])
                        b0 = tab_ref[a, r0] if r0 >= 0 else neg
                        b1 = tab_ref[a, r1] if r1 >= 0 else neg
                        blk = neg if (r0 < 0 and r1 < 0) else jnp.where(left, b0, b1)
                        bias_ref[a, i * GRID_W:(i + 1) * GRID_W, j * GRID_W:(j + 2) * GRID_W] = blk

    st = pl.multiple_of(start_ref[t] * GRID_W, GRID_W)
    q = q_ref[...] * (NA_HD ** -0.5)
    kw = k_ref[pl.ds(st, nkey), :]
    vw = v_ref[pl.ds(st, nkey), :]
    kc = kc_ref[0]
    vc = vc_ref[0]
    outs = []
    for a in range(2):
        qa = jnp.where(_head_mask(a), q, jnp.zeros_like(q))
        s1 = _qk(qa, kw) + bias_ref[a]
        s2 = _qk(qa, kc)
        m = jnp.maximum(jnp.max(s1, axis=-1, keepdims=True), jnp.max(s2, axis=-1, keepdims=True))
        p1 = jnp.exp(s1 - m)
        p2 = jnp.exp(s2 - m)
        l = jnp.sum(p1, axis=-1, keepdims=True) + jnp.sum(p2, axis=-1, keepdims=True)
        o = (jnp.dot(p1.astype(BF16), vw, preferred_element_type=F32)
             + jnp.dot(p2.astype(BF16), vc, preferred_element_type=F32))
        outs.append(o / l)
    o_ref[...] = jnp.where(_head_mask(0), outs[0], outs[1]).astype(BF16)


def _na_latent(zb, kctx, vctx, rpb, nb, seq):
    rows = seq // GRID_W
    pats, var, starts, span = _na_geometry(rows)
    nkey = span * GRID_W
    tq = NA_TR * GRID_W
    ntile = rows // NA_TR
    hp = NA_HEADS // 2
    w = 2 * NA_HD
    past = kctx.shape[1]
    assert span % 2 == 0
    tabs = _na_col_tables(rpb)
    nrow = tabs.shape[1]
    grid_spec = pltpu.PrefetchScalarGridSpec(
        num_scalar_prefetch=2,
        grid=(nb, hp, ntile),
        in_specs=[
            pl.BlockSpec((tq, w), lambda b, p, t, *_: (b * ntile + t, ZB_NQ * hp + p)),
            pl.BlockSpec((seq, w), lambda b, p, t, *_: (b, ZB_NK * hp + p)),
            pl.BlockSpec((seq, w), lambda b, p, t, *_: (b, ZB_NV * hp + p)),
            pl.BlockSpec((1, past, w), lambda b, p, t, *_: (b, 0, p)),
            pl.BlockSpec((1, past, w), lambda b, p, t, *_: (b, 0, p)),
            pl.BlockSpec((2, nrow, GRID_W, 2 * GRID_W), lambda b, p, t, *_: (p, 0, 0, 0)),
        ],
        out_specs=pl.BlockSpec((tq, w), lambda b, p, t, *_: (b * ntile + t, p)),
        scratch_shapes=[pltpu.VMEM((2, tq, nkey), F32)],
    )
    return pl.pallas_call(
        functools.partial(_na_kernel, nkey=nkey, pats=pats),
        grid_spec=grid_spec,
        out_shape=jax.ShapeDtypeStruct((nb * seq, CH), BF16),
        compiler_params=_cparams(("parallel", "parallel", "arbitrary")),
        name="na_latent",
    )(jnp.asarray(var), jnp.asarray(starts), zb, zb, zb, kctx, vctx, tabs)


def _merge_kernel(x_ref, mod_ref, gate_ref, yft_ref, of_ref, ob_ref, hg_ref, on_ref, hn_ref,
                  wfo_ref, who_ref, wno_ref, wout_ref, o_ref):
    d = x_ref.shape[1]
    yft = jnp.concatenate([yft_ref[g] for g in range(FT_GROUPS)], axis=1)
    y_ft = jnp.dot(yft.astype(BF16), wfo_ref[...], preferred_element_type=F32)
    o = of_ref[...] + ob_ref[...]
    parts = []
    for h in range(HG_HEADS):
        oh = o[:, h * HG_D:(h + 1) * HG_D]
        ms = jnp.mean(oh * oh, axis=-1, keepdims=True)
        parts.append(oh * lax.rsqrt(ms + EPS))
    oh = jnp.concatenate(parts, axis=1) * hn_ref[...]
    oh = (oh * _silu(hg_ref[...].astype(F32))).astype(BF16)
    y_hg = jnp.dot(oh, who_ref[...], preferred_element_type=F32)
    y_na = jnp.dot(on_ref[...], wno_ref[...], preferred_element_type=F32)
    g = gate_ref[...].astype(F32)
    merged = (_sigmoid(g[:, :d]) * y_ft + _sigmoid(g[:, d:2 * d]) * y_hg
              + _sigmoid(g[:, 2 * d:]) * y_na)
    y = jnp.dot(merged.astype(BF16), wout_ref[...], preferred_element_type=F32)
    o_ref[...] = x_ref[...] + mod_ref[0, 2:3, :] * y


def _merge(x, mod, zb, yft, o_f, o_b, o_n, hg_norm, w_fo, w_ho, w_no, w_out, *, rows_per_seg):
    t, d = x.shape
    tm = ROW_TILE
    spt = rows_per_seg // tm
    assert 3 * d == 6 * CH, "the three merge gates fill the first six bf16 splits"
    row = lambda i: (i, 0)
    return pl.pallas_call(
        _merge_kernel,
        grid=(t // tm,),
        in_specs=[
            pl.BlockSpec((tm, d), row),
            pl.BlockSpec((1, 6, d), lambda i: (i // spt, 0, 0)),
            pl.BlockSpec((tm, 3 * d), row),
            pl.BlockSpec((FT_GROUPS, tm, FT_GD), lambda i: (0, i, 0)),
            pl.BlockSpec((tm, CH), row),
            pl.BlockSpec((tm, CH), row),
            pl.BlockSpec((tm, CH), lambda i: (i, ZB_HG)),
            pl.BlockSpec((tm, CH), row),
            _const_spec((1, CH)),
            _const_spec((CH, d)),
            _const_spec((CH, d)),
            _const_spec((CH, d)),
            _const_spec((d, d)),
        ],
        out_specs=pl.BlockSpec((tm, d), row),
        out_shape=jax.ShapeDtypeStruct((t, d), F32),
        compiler_params=_cparams(("parallel",)),
        name="merge",
    )(x, mod, zb, yft, o_f, o_b, zb, o_n, hg_norm.reshape(1, CH), w_fo, w_ho, w_no, w_out)


FF_CHUNK = 256


def _swiglu_acc(h, w1_ref, w3_ref, w2_ref, lead=()):
    dff = w1_ref.shape[-1]
    acc = None
    for c0 in range(0, dff, FF_CHUNK):
        cs = slice(c0, c0 + FF_CHUNK)
        g = jnp.dot(h, w1_ref[lead + (slice(None), cs)], preferred_element_type=F32)
        u = jnp.dot(h, w3_ref[lead + (slice(None), cs)], preferred_element_type=F32)
        a = (_silu(g) * u).astype(BF16)
        y = jnp.dot(a, w2_ref[lead + (cs, slice(None))], preferred_element_type=F32)
        acc = y if acc is None else acc + y
    return acc


def _ffn_kernel(x_ref, g_ref, mod_ref, w1_ref, w3_ref, w2_ref, *refs, final):
    o_ref = refs[-1]
    x = x_ref[...]
    h = _norm_mod(x, g_ref[...], mod_ref[0, 3:4, :], mod_ref[0, 4:5, :]).astype(BF16)
    y = x + mod_ref[0, 5:6, :] * _swiglu_acc(h, w1_ref, w3_ref, w2_ref)
    if final:
        ms = jnp.mean(y * y, axis=-1, keepdims=True)
        y = y * lax.rsqrt(ms + EPS) * refs[0][...]
    o_ref[...] = y


def _ffn_dense(x, g, mod, w1, w3, w2, norm_final, *, rows_per_seg):
    t, d = x.shape
    dff = w1.shape[1]
    tm = ROW_TILE
    spt = rows_per_seg // tm
    final = norm_final is not None
    in_specs = [
        pl.BlockSpec((tm, d), lambda i: (i, 0)),
        _const_spec((1, d)),
        pl.BlockSpec((1, 6, d), lambda i: (i // spt, 0, 0)),
        _const_spec((d, dff)),
        _const_spec((d, dff)),
        _const_spec((dff, d)),
    ]
    args = [x, g.reshape(1, d), mod, w1, w3, w2]
    if final:
        in_specs.append(_const_spec((1, d)))
        args.append(norm_final.reshape(1, d))
    return pl.pallas_call(
        functools.partial(_ffn_kernel, final=final),
        grid=(t // tm,),
        in_specs=in_specs,
        out_specs=pl.BlockSpec((tm, d), lambda i: (i, 0)),
        out_shape=jax.ShapeDtypeStruct((t, d), F32),
        compiler_params=_cparams(("parallel",)),
        name="ffn_dense",
    )(*args)


def _router_kernel(x_ref, g_ref, mod_ref, wr_ref, h_ref, r_ref):
    h = _norm_mod(x_ref[...], g_ref[...], mod_ref[0, 3:4, :], mod_ref[0, 4:5, :])
    h_ref[...] = h
    logits = jnp.dot(h, wr_ref[...], precision=lax.Precision.HIGHEST,
                     preferred_element_type=F32)
    lane = lax.broadcasted_iota(jnp.int32, logits.shape, 1)
    logits = jnp.where(lane < N_EXPERTS, logits, -jnp.inf)
    m1 = jnp.max(logits, axis=-1, keepdims=True)
    i1 = jnp.min(jnp.where(logits == m1, lane, LANES), axis=-1, keepdims=True)
    rest = jnp.where(lane == i1, -jnp.inf, logits)
    m2 = jnp.max(rest, axis=-1, keepdims=True)
    i2 = jnp.min(jnp.where(rest == m2, lane, LANES), axis=-1, keepdims=True)
    e2 = jnp.exp(m2 - m1)
    w1 = 1.0 / (1.0 + e2)
    w2 = e2 / (1.0 + e2)
    r = jnp.where(lane == 0, i1.astype(F32), jnp.where(lane == 1, i2.astype(F32),
                  jnp.where(lane == 2, w1, jnp.where(lane == 3, w2, 0.0))))
    r_ref[...] = r


def _router(x, g, mod, w_router, *, rows_per_seg, h_all, row_off, total):
    t, d = x.shape
    tm = ROW_TILE
    spt = rows_per_seg // tm
    off = row_off // tm
    wr = jnp.zeros((d, LANES), F32).at[:, :N_EXPERTS].set(w_router)
    in_specs = [
        pl.BlockSpec((tm, d), lambda i: (i, 0)),
        _const_spec((1, d)),
        pl.BlockSpec((1, 6, d), lambda i: (i // spt, 0, 0)),
        _const_spec((d, LANES)),
    ]
    args = [x, g.reshape(1, d), mod, wr]
    kern, aliases = _router_kernel, {}
    if h_all is not None:
        in_specs.append(pl.BlockSpec(memory_space=pl.ANY))
        args.append(h_all)
        kern, aliases = functools.partial(_drop_ref, _router_kernel, 4), {4: 0}
    return pl.pallas_call(
        kern,
        grid=(t // tm,),
        in_specs=in_specs,
        out_specs=[pl.BlockSpec((tm, d), lambda i: (i + off, 0)), pl.BlockSpec((tm, LANES), lambda i: (i, 0))],
        out_shape=[jax.ShapeDtypeStruct((total, d), F32), jax.ShapeDtypeStruct((t, LANES), F32)],
        input_output_aliases=aliases,
        compiler_params=_cparams(("parallel",)),
        name="router",
    )(*args)


MOE_TM = 512


def _moe_ffn_kernel(te_ref, nt_ref, src_ref, h_hbm, w1_ref, w3_ref, w2_ref, y_ref, hbuf, sem):
    i = pl.program_id(0)
    nt = nt_ref[0]
    slot = i % 2

    def gather(tile, s):
        base = tile * MOE_TM

        def issue(r, carry):
            pltpu.make_async_copy(h_hbm.at[pl.ds(src_ref[base + r], 1)],
                                  hbuf.at[s, pl.ds(r, 1)], sem.at[s]).start()
            return carry

        lax.fori_loop(0, MOE_TM, issue, 0, unroll=8)

    @pl.when(jnp.logical_and(i == 0, nt > 0))
    def _():
        gather(0, 0)

    @pl.when(i < nt)
    def _():
        @pl.when(i + 1 < nt)
        def _():
            gather(i + 1, 1 - slot)

        pltpu.make_async_copy(h_hbm.at[pl.ds(0, MOE_TM)], hbuf.at[slot], sem.at[slot]).wait()
        y_ref[...] = _swiglu_acc(hbuf[slot].astype(BF16), w1_ref, w3_ref, w2_ref, lead=(0,))

    @pl.when(i >= nt)
    def _():
        y_ref[...] = jnp.zeros_like(y_ref)


def _moe_ffn(h, tile_expert, n_tiles, src, w1, w3, w2, ntile_max):
    t, d = h.shape
    dff = w1.shape[-1]
    grid_spec = pltpu.PrefetchScalarGridSpec(
        num_scalar_prefetch=3,
        grid=(ntile_max,),
        in_specs=[
            pl.BlockSpec(memory_space=pl.ANY),
            pl.BlockSpec((1, d, dff), lambda i, te, nt, src: (te[i], 0, 0)),
            pl.BlockSpec((1, d, dff), lambda i, te, nt, src: (te[i], 0, 0)),
            pl.BlockSpec((1, dff, d), lambda i, te, nt, src: (te[i], 0, 0)),
        ],
        out_specs=pl.BlockSpec((MOE_TM, d), lambda i, te, nt, src: (i, 0)),
        scratch_shapes=[pltpu.VMEM((2, MOE_TM, d), F32), pltpu.SemaphoreType.DMA((2,))],
    )
    return pl.pallas_call(
        _moe_ffn_kernel,
        grid_spec=grid_spec,
        out_shape=jax.ShapeDtypeStruct((ntile_max * MOE_TM, d), F32),
        compiler_params=_cparams(("arbitrary",)),
        name="moe_ffn",
    )(tile_expert, n_tiles, src, h, w1, w3, w2)


def _combine_kernel(pos_ref, x_ref, mod_ref, r_ref, y_hbm, *refs, final, tm):
    o_ref, ybuf, sem = refs[-3:]
    i = pl.program_id(0)
    slot = i % 2

    def gather(tile, s):
        base = 2 * tile * tm

        def issue(r, carry):
            for k in range(2):
                pltpu.make_async_copy(y_hbm.at[pl.ds(pos_ref[base + 2 * r + k], 1)],
                                      ybuf.at[s, k, pl.ds(r, 1)], sem.at[s]).start()
            return carry

        lax.fori_loop(0, tm, issue, 0, unroll=4)

    @pl.when(i == 0)
    def _():
        gather(0, 0)

    @pl.when(i + 1 < pl.num_programs(0))
    def _():
        gather(i + 1, 1 - slot)

    for k in range(2):
        pltpu.make_async_copy(y_hbm.at[pl.ds(0, tm)], ybuf.at[slot, k], sem.at[slot]).wait()
    r = r_ref[...]
    f = r[:, 2:3] * ybuf[slot, 0] + r[:, 3:4] * ybuf[slot, 1]
    y = x_ref[...] + mod_ref[0, 5:6, :] * f
    if final:
        ms = jnp.mean(y * y, axis=-1, keepdims=True)
        y = y * lax.rsqrt(ms + EPS) * refs[0][...]
    o_ref[...] = y


def _moe_combine(x, mod, route, pos, y_sorted, norm_final, *, rows_per_seg):
    t, d = x.shape
    tm = 256
    spt = rows_per_seg // tm
    final = norm_final is not None
    in_specs = [
        pl.BlockSpec((tm, d), lambda i, pos: (i, 0)),
        pl.BlockSpec((1, 6, d), lambda i, pos: (i // spt, 0, 0)),
        pl.BlockSpec((tm, LANES), lambda i, pos: (i, 0)),
        pl.BlockSpec(memory_space=pl.ANY),
    ]
    args = [x, mod, route, y_sorted]
    if final:
        in_specs.append(pl.BlockSpec((1, d), lambda i, pos: (0, 0)))
        args.append(norm_final.reshape(1, d))
    grid_spec = pltpu.PrefetchScalarGridSpec(
        num_scalar_prefetch=1,
        grid=(t // tm,),
        in_specs=in_specs,
        out_specs=pl.BlockSpec((tm, d), lambda i, pos: (i, 0)),
        scratch_shapes=[pltpu.VMEM((2, 2, tm, d), F32), pltpu.SemaphoreType.DMA((2,))],
    )
    return pl.pallas_call(
        functools.partial(_combine_kernel, final=final, tm=tm),
        grid_spec=grid_spec,
        out_shape=jax.ShapeDtypeStruct((t, d), F32),
        compiler_params=_cparams(("arbitrary",)),
        name="moe_combine",
    )(pos.reshape(-1), *args)


def _moe(sets, g, w_router, w1, w3, w2, norm_final):
    t = sum(x.shape[0] for x, _, _ in sets)
    h, routes, off = None, [], 0
    for x, mod, seg in sets:
        h, r = _router(x, g, mod, w_router, rows_per_seg=seg, h_all=h, row_off=off, total=t)
        routes.append(r)
        off += x.shape[0]
    route = jnp.concatenate(routes, axis=0)
    ids = route[:, :2].astype(jnp.int32)
    onehot = (ids[:, :, None] == jnp.arange(N_EXPERTS)[None, None, :]).astype(jnp.int32)
    sel = onehot.sum(axis=1)
    counts = sel.sum(axis=0)
    rank = jnp.cumsum(sel, axis=0) - sel
    padded = ((counts + MOE_TM - 1) // MOE_TM) * MOE_TM
    pstart = jnp.cumsum(padded) - padded
    cstart = jnp.cumsum(counts) - counts
    pos_e = pstart[None, :] + rank
    pos = jnp.take_along_axis(pos_e, ids, axis=1).astype(jnp.int32)
    ntile_max = (2 * t) // MOE_TM + N_EXPERTS
    tok = jnp.arange(t, dtype=jnp.int32)
    key = jnp.sort((ids * t + tok[:, None]).reshape(-1))
    sorted_tok = key % t
    tile_start = jnp.arange(ntile_max, dtype=jnp.int32) * MOE_TM
    pend = pstart + padded
    tile_expert = jnp.minimum(jnp.sum(tile_start[:, None] >= pend[None, :], axis=1), N_EXPERTS - 1)
    n_tiles = (jnp.sum(padded) // MOE_TM).astype(jnp.int32).reshape(1)
    p = jnp.arange(ntile_max * MOE_TM, dtype=jnp.int32)
    e_p = jnp.repeat(tile_expert, MOE_TM)
    r_p = jnp.minimum(p - pstart[e_p], jnp.maximum(counts[e_p] - 1, 0))
    src = sorted_tok[jnp.clip(cstart[e_p] + r_p, 0, 2 * t - 1)].astype(jnp.int32)
    y_sorted = _moe_ffn(h, tile_expert.astype(jnp.int32), n_tiles, src, w1, w3, w2, ntile_max)
    outs, off = [], 0
    for (x, mod, seg), r in zip(sets, routes):
        n = x.shape[0]
        outs.append(_moe_combine(x, mod, r, pos[off:off + n], y_sorted, norm_final, rows_per_seg=seg))
        off += n
    return outs


def _mixer(x, mod, l, depth, p, *, nb, seq, seg, ctx, caches, state_out):
    is_ctx = ctx is None
    outs = _in_proj(x, p["norm_mix"][l], mod, p["w_in"][l], rows_per_seg=seg, seq=seq, layer=l,
                    depth=depth, caches=caches if is_ctx else None)
    zb, zf = outs[0], outs[1]
    new_caches = tuple(outs[2:]) if is_ctx else None
    if is_ctx:
        yft = _fourier_ctx(zb, nb, seq)
        o_n = _attn_ctx(zb, nb, seq)
    else:
        yft = _fourier_lat(zb, nb, seq)
        o_n = _na_latent(zb, ctx[0], ctx[1], p["na_rpb"][l], nb, seq)
    hg_c = 128
    lbr = 256 if is_ctx else 512
    lbr = min(lbr, seq)
    o_f, state_out = _hgrn(zb, zf, p["lbp"][l][0], None if is_ctx else ctx[2],
                           state_out if is_ctx else None, nb=nb, seq=seq, rev=False, layer=l,
                           depth=depth, c=hg_c, lb_rows=lbr)
    o_b, state_out = _hgrn(zb, zf, p["lbp"][l][1], None if is_ctx else ctx[3],
                           state_out if is_ctx else None, nb=nb, seq=seq, rev=True, layer=l,
                           depth=depth, c=hg_c, lb_rows=lbr)
    x = _merge(x, mod, zb, yft, o_f, o_b, o_n, p["hg_norm"][l], p["w_fo"][l], p["w_ho"][l],
               p["w_no"][l], p["w_out"][l], rows_per_seg=seg)
    return x, new_caches, state_out


def kernel(x_prompt, x_sample, cache_k, cache_v, state_hgrn, c, c_ctx, w_mod, b_mod, norm_mix,
           norm_ffn, w_in, hg_lb, hg_norm, na_rpb, w_fo, w_ho, w_no, w_out, w1_d, w3_d, w2_d,
           w_router, w1_e, w3_e, w2_e, norm_final):
    nb_c, seq_c, d = x_prompt.shape
    nb_l, seq_l, _ = x_sample.shape
    depth = w_in.shape[0]
    past = cache_k.shape[2]

    lbs = jnp.cumsum(jax.nn.softmax(hg_lb.astype(F32), axis=0), axis=0)
    lbs = lbs - lbs[0:1]
    lbp = jnp.stack([jnp.log(lbs), jnp.log1p(-lbs)], axis=2)

    bf = lambda w: w.astype(BF16)
    p = dict(norm_mix=norm_mix, norm_ffn=norm_ffn, w_in=bf(w_in), hg_norm=hg_norm, na_rpb=na_rpb,
             w_fo=bf(w_fo), w_ho=bf(w_ho), w_no=bf(w_no), w_out=bf(w_out), w1_d=bf(w1_d),
             w3_d=bf(w3_d), w2_d=bf(w2_d), w_router=w_router, w1_e=bf(w1_e), w3_e=bf(w3_e),
             w2_e=bf(w2_e), lbp=lbp)

    mods = _mod_vectors(jnp.concatenate([c_ctx[None], c], axis=0), w_mod, b_mod)

    x = x_prompt.reshape(nb_c * seq_c, d)
    xs = x_sample.reshape(nb_l * seq_l, d)
    seg_c, seg_l = nb_c * seq_c, seq_l
    caches, state_out = (None, None), None
    for l in range(depth):
        mod_c, mod_l = mods[l, 0:1], mods[l, 1:]
        x, caches, state_out = _mixer(x, mod_c, l, depth, p, nb=nb_c, seq=seq_c, seg=seg_c, ctx=None,
                                      caches=caches, state_out=state_out)
        ctx = (bf(cache_k[:, l].reshape(nb_l, past, CH)), bf(cache_v[:, l].reshape(nb_l, past, CH)),
               state_hgrn[:, l, 0], state_hgrn[:, l, 1])
        xs, _, _ = _mixer(xs, mod_l, l, depth, p, nb=nb_l, seq=seq_l, seg=seg_l, ctx=ctx, caches=None,
                          state_out=None)
        nf = norm_final if l == depth - 1 else None
        i = l // 2
        if l % 2 == 0:
            x = _ffn_dense(x, norm_ffn[l], mod_c, p["w1_d"][i], p["w3_d"][i], p["w2_d"][i], nf,
                           rows_per_seg=seg_c)
            xs = _ffn_dense(xs, norm_ffn[l], mod_l, p["w1_d"][i], p["w3_d"][i], p["w2_d"][i], nf,
                            rows_per_seg=seg_l)
        else:
            x, xs = _moe([(x, mod_c, seg_c), (xs, mod_l, seg_l)], norm_ffn[l], w_router[i],
                         p["w1_e"][i], p["w3_e"][i], p["w2_e"][i], nf)
    y_prompt = x.reshape(nb_c, seq_c, d)
    y_sample = xs.reshape(nb_l, seq_l, d)

    new_cache_k = caches[0].reshape(nb_c, depth, seq_c, NA_HEADS, NA_HD)
    new_cache_v = caches[1].reshape(nb_c, depth, seq_c, NA_HEADS, NA_HD)
    return (y_prompt, y_sample, new_cache_k, new_cache_v, state_out)
```

```python
import functools
import math

import numpy as np
import jax
import jax.numpy as jnp
from jax import lax
from jax.experimental import pallas as pl
from jax.experimental.pallas import tpu as pltpu

F32 = jnp.float32
BF16 = jnp.bfloat16

LANES = 128
SUBLANES = 8
VMEM_LIMIT = 56 * 1024 * 1024
ROW_TILE = 512

FT_GROUPS = 4
FT_GD = 128
HG_HEADS = 4
HG_D = 128
NA_HEADS = 8
NA_HD = 64
GRID_W = 64
WIN_R = 8
WIN_C = 16
N_EXPERTS = 8
EPS = 1e-6
CH = 512
NEG = -1e30

ZB_SRC = (9, 10, 11, 12, 13, 14, 0, 1, 4, 5, 6, 7, 8)
ZB_UFT, ZB_HQ, ZB_HI, ZB_HG, ZB_NQ, ZB_NK, ZB_NV = 6, 7, 8, 9, 10, 11, 12
ZF_SRC = (2, 3)
SRC_NK, SRC_NV = 7, 8


def _cparams(sem):
    return pltpu.CompilerParams(dimension_semantics=sem, vmem_limit_bytes=VMEM_LIMIT)


def _const_spec(shape):
    nd = len(shape)
    return pl.BlockSpec(shape, lambda *_: (0,) * nd, pipeline_mode=pl.Buffered(1))


def _silu(x):
    return x * (1.0 / (1.0 + jnp.exp(-x)))


def _sigmoid(x):
    return 1.0 / (1.0 + jnp.exp(-x))


def _norm_mod(x, g, shift, scale):
    ms = jnp.mean(x * x, axis=-1, keepdims=True)
    return (x * lax.rsqrt(ms + EPS) * g) * (1.0 + scale) + shift


def _mod_kernel(c_ref, w_ref, b_ref, o_ref, *, nv, tn):
    w = w_ref[0]
    for v in range(nv):
        s = _silu(c_ref[v])
        s = jnp.concatenate([s] * (tn // LANES), axis=1)
        o_ref[0, v:v + 1, :] = jnp.sum(w * s, axis=0, keepdims=True) + b_ref[0]


def _mod_vectors(cvecs, w_mod, b_mod):
    nv, d = cvecs.shape
    depth, _, n6 = w_mod.shape
    tn = 512
    cb = jnp.broadcast_to(cvecs[:, :, None], (nv, d, LANES))
    out = pl.pallas_call(
        functools.partial(_mod_kernel, nv=nv, tn=tn),
        grid=(depth, n6 // tn),
        in_specs=[
            pl.BlockSpec((nv, d, LANES), lambda l, j: (0, 0, 0)),
            pl.BlockSpec((1, d, tn), lambda l, j: (l, 0, j)),
            pl.BlockSpec((1, 1, tn), lambda l, j: (l, 0, j)),
        ],
        out_specs=pl.BlockSpec((1, nv, tn), lambda l, j: (l, 0, j)),
        out_shape=jax.ShapeDtypeStruct((depth, nv, n6), F32),
        compiler_params=_cparams(("parallel", "parallel")),
        name="mod_vectors",
    )(cb, w_mod, b_mod.reshape(depth, 1, n6))
    return out.reshape(depth, nv, 6, d)


def _in_kernel(x_ref, g_ref, mod_ref, w_ref, *refs, with_cache):
    if with_cache:
        zb_ref, zf_ref, ck_ref, cv_ref = refs[-4:]
    else:
        zb_ref, zf_ref = refs[-2:]
    h = _norm_mod(x_ref[...], g_ref[...], mod_ref[0, 0:1, :], mod_ref[0, 1:2, :]).astype(BF16)
    for src in range(15):
        z = jnp.dot(h, w_ref[:, src * CH:(src + 1) * CH], preferred_element_type=F32)
        if src in ZB_SRC:
            j = ZB_SRC.index(src)
            zb_ref[:, j * CH:(j + 1) * CH] = z.astype(BF16)
        if src in ZF_SRC:
            j = ZF_SRC.index(src)
            zf_ref[:, j * CH:(j + 1) * CH] = z
        if with_cache and src in (SRC_NK, SRC_NV):
            c_ref = ck_ref if src == SRC_NK else cv_ref
            nseq, _, seq, _ = c_ref.shape
            for s in range(nseq):
                c_ref[s, 0] = z[s * seq:(s + 1) * seq]


def _in_proj(x, g, mod, w_bf, *, rows_per_seg, seq, layer, depth, caches):
    t, d = x.shape
    tm = ROW_TILE
    with_cache = caches is not None
    spt = rows_per_seg // tm
    in_specs = [
        pl.BlockSpec((tm, d), lambda i: (i, 0)),
        _const_spec((1, d)),
        pl.BlockSpec((1, 6, d), lambda i: (i // spt, 0, 0)),
        _const_spec((d, 15 * CH)),
    ]
    args = [x, g.reshape(1, d), mod, w_bf]
    out_specs = [
        pl.BlockSpec((tm, len(ZB_SRC) * CH), lambda i: (i, 0)),
        pl.BlockSpec((tm, len(ZF_SRC) * CH), lambda i: (i, 0)),
    ]
    out_shape = [
        jax.ShapeDtypeStruct((t, len(ZB_SRC) * CH), BF16),
        jax.ShapeDtypeStruct((t, len(ZF_SRC) * CH), F32),
    ]
    aliases = {}
    if with_cache:
        assert tm % seq == 0, "a row tile holds whole context sequences"
        cspec = pl.BlockSpec((tm // seq, 1, seq, CH), lambda i: (i, layer, 0, 0))
        out_specs += [cspec, cspec]
        cshape = jax.ShapeDtypeStruct((t // seq, depth, seq, CH), F32)
        out_shape += [cshape, cshape]
        if caches[0] is not None:
            in_specs += [pl.BlockSpec(memory_space=pl.ANY)] * 2
            args += list(caches)
            aliases = {4: 2, 5: 3}
    return pl.pallas_call(
        functools.partial(_in_kernel, with_cache=with_cache),
        grid=(t // tm,),
        in_specs=in_specs,
        out_specs=out_specs,
        out_shape=out_shape,
        input_output_aliases=aliases,
        compiler_params=_cparams(("parallel",)),
        name="in_proj",
    )(*args)


def _dft_tables(n, scale):
    k = np.arange(n)
    ang = 2.0 * np.pi * ((k[:, None] * k[None, :]) % n) / n
    return np.cos(ang) * scale, np.sin(ang) * scale


def _ft_ctx_kernel(u_ref, cs_ref, lm_ref, y_ref):
    u = u_ref[...]
    cs = cs_ref[...].astype(BF16)
    zc, zs = [], []
    for g in range(FT_GROUPS):
        z = jnp.dot(u[:, g * FT_GD:(g + 1) * FT_GD], cs, preferred_element_type=F32)
        zc.append(z[:, :FT_GD])
        zs.append(z[:, FT_GD:])
    zz = jnp.concatenate([jnp.concatenate(zc, axis=1), jnp.concatenate(zs, axis=1)], axis=0)
    y = jnp.dot(lm_ref[...].astype(BF16), zz.astype(BF16), preferred_element_type=F32)
    for g in range(FT_GROUPS):
        y_ref[g] = y[:, g * FT_GD:(g + 1) * FT_GD]


def _fourier_ctx(zb, nb, seq):
    cc, sc = _dft_tables(FT_GD, FT_GD ** -0.5)
    cl, sl = _dft_tables(seq, seq ** -0.5)
    cs = jnp.asarray(np.concatenate([cc, sc], axis=1), F32)
    lm = jnp.asarray(np.concatenate([cl, -sl], axis=1), F32)
    return pl.pallas_call(
        _ft_ctx_kernel,
        grid=(nb,),
        in_specs=[
            pl.BlockSpec((seq, CH), lambda b: (b, ZB_UFT)),
            _const_spec(cs.shape),
            _const_spec(lm.shape),
        ],
        out_specs=pl.BlockSpec((FT_GROUPS, seq, FT_GD), lambda b: (0, b, 0)),
        out_shape=jax.ShapeDtypeStruct((FT_GROUPS, nb * seq, FT_GD), F32),
        compiler_params=_cparams(("parallel",)),
        name="fourier_ctx",
    )(zb, cs, lm)


def _ft_lat_kernel(u_ref, cs_ref, w1_ref, twc_ref, tws_ref, w2_ref, y_ref, zr_ref, zi_ref,
                   ar_ref, ai_ref, *, n, lw):
    rows = n * n
    rc = min(rows, 512)
    ng = lw // FT_GD
    cs = cs_ref[...].astype(BF16)
    w1 = w1_ref[...].astype(BF16)
    w2 = w2_ref[...].astype(BF16)
    for r0 in range(0, rows, rc):
        for g in range(ng):
            z = jnp.dot(u_ref[r0:r0 + rc, g * FT_GD:(g + 1) * FT_GD], cs,
                        preferred_element_type=F32)
            zr_ref[g, r0:r0 + rc, :] = z[:, :FT_GD]
            zi_ref[g, r0:r0 + rc, :] = -z[:, FT_GD:]

    def gather(re_ref, im_ref, idx):
        xr = jnp.concatenate([re_ref[g, idx, :] for g in range(ng)], axis=1)
        xi = jnp.concatenate([im_ref[g, idx, :] for g in range(ng)], axis=1)
        return jnp.concatenate([xr, xi], axis=0).astype(BF16)

    def stage1(a, carry):
        idx = pl.ds(a, n, stride=n)
        y = jnp.dot(w1, gather(zr_ref, zi_ref, idx), preferred_element_type=F32)
        yr, yi = y[:n], y[n:]
        tc = jnp.concatenate([twc_ref[a]] * ng, axis=1)
        ts = jnp.concatenate([tws_ref[a]] * ng, axis=1)
        zr = yr * tc - yi * ts
        zi = yr * ts + yi * tc
        for g in range(ng):
            ar_ref[g, idx, :] = zr[:, g * FT_GD:(g + 1) * FT_GD]
            ai_ref[g, idx, :] = zi[:, g * FT_GD:(g + 1) * FT_GD]
        return carry

    lax.fori_loop(0, n, stage1, 0, unroll=4)

    def stage2(q, carry):
        r0 = pl.multiple_of(q * n, n)
        y = jnp.dot(w2, gather(ar_ref, ai_ref, pl.ds(r0, n)), preferred_element_type=F32)
        for g in range(ng):
            y_ref[g, pl.ds(q, n, stride=n), :] = y[:, g * FT_GD:(g + 1) * FT_GD]
        return carry

    lax.fori_loop(0, n, stage2, 0, unroll=4)


def _fourier_lat(zb, nb, seq):
    n = math.isqrt(seq)
    assert n * n == seq and n % SUBLANES == 0
    lw = 256
    cc, sc = _dft_tables(FT_GD, FT_GD ** -0.5)
    cs = jnp.asarray(np.concatenate([cc, sc], axis=1), F32)
    pr, ps = _dft_tables(n, 1.0 / n)
    w1 = jnp.asarray(np.block([[pr, ps], [-ps, pr]]), F32)
    k = np.arange(n)
    ang = 2.0 * np.pi * (k[:, None] * k[None, :]) / seq
    twc = jnp.asarray(np.broadcast_to(np.cos(ang)[:, :, None], (n, n, LANES)), F32)
    tws = jnp.asarray(np.broadcast_to(-np.sin(ang)[:, :, None], (n, n, LANES)), F32)
    wr, ws = _dft_tables(n, 1.0)
    w2 = jnp.asarray(np.concatenate([wr, ws], axis=1), F32)
    nh = CH // lw
    return pl.pallas_call(
        functools.partial(_ft_lat_kernel, n=n, lw=lw),
        grid=(nb, nh),
        in_specs=[
            pl.BlockSpec((seq, lw), lambda b, h: (b, ZB_UFT * nh + h)),
            _const_spec(cs.shape),
            _const_spec(w1.shape),
            _const_spec(twc.shape),
            _const_spec(tws.shape),
            _const_spec(w2.shape),
        ],
        out_specs=pl.BlockSpec((lw // FT_GD, seq, FT_GD), lambda b, h: (h, b, 0)),
        out_shape=jax.ShapeDtypeStruct((FT_GROUPS, nb * seq, FT_GD), F32),
        scratch_shapes=[pltpu.VMEM((lw // FT_GD, seq, FT_GD), F32)] * 4,
        compiler_params=_cparams(("parallel", "parallel")),
        name="fourier_lat",
    )(zb, cs, w1, twc, tws, w2)


HG_SAFE = 160.0


def _hgrn_tables(c, rev):
    t = np.arange(c)
    tri = (t[None, :] >= t[:, None]) if rev else (t[None, :] <= t[:, None])
    lv = np.full((c, c), -1, np.int32)
    n, i = 1, 0
    while n < c:
        same = (t[:, None] // (2 * n)) == (t[None, :] // (2 * n))
        hi_t, hi_s = (t[:, None] % (2 * n)) >= n, (t[None, :] % (2 * n)) >= n
        pair = same & (~hi_t & hi_s if rev else hi_t & ~hi_s)
        lv[pair] = i
        n, i = 2 * n, i + 1
    return tri.astype(np.float32), lv


def _split3(x):
    hi = x.astype(BF16)
    r = x - hi.astype(F32)
    mid = r.astype(BF16)
    lo = (r - mid.astype(F32)).astype(BF16)
    return hi, mid, lo


def _hgrn_kernel(hq_ref, hf_ref, hi_ref, lb_ref, tri_ref, lv_ref, *refs, c, nch, rev, has_s0):
    if has_s0:
        s0_ref, o_ref, st_ref, b_ref, k_ref, q_ref = refs
    else:
        o_ref, so_ref, st_ref, b_ref, k_ref, q_ref = refs
    j = pl.program_id(1)

    @pl.when(j == 0)
    def _():
        for h in range(HG_HEADS):
            st_ref[h] = s0_ref[0, h].T if has_s0 else jnp.zeros((HG_D, HG_D), F32)

    la = lb_ref[0:1, :]
    l1 = lb_ref[1:2, :]
    lv = lv_ref[...]
    row8 = lax.broadcasted_iota(jnp.int32, (c // SUBLANES, SUBLANES, HG_D), 1)
    ti = lax.broadcasted_iota(jnp.int32, (c, c), 0)
    si = lax.broadcasted_iota(jnp.int32, (c, c), 1)
    causal = (si >= ti) if rev else (si <= ti)
    half = c // 2
    same_half = jnp.logical_and(causal, (ti < half) == (si < half))
    cross_half = jnp.logical_and(causal, (ti < half) != (si < half))
    row = lax.broadcasted_iota(jnp.int32, (c, HG_D), 0)
    first_rows = (row >= half) if rev else (row < half)
    bl_row = c - 1 if not rev else 0
    mid_row = half if rev else half - 1

    tmax = jnp.float32(0.0)
    for cc in range(nch):
        rows = slice(cc * c, (cc + 1) * c)
        z = hf_ref[rows, :]
        ls = jnp.minimum(z, 0.0) - jnp.log(1.0 + jnp.exp(-jnp.abs(z)))
        cc_ = l1 + ls
        logf = jnp.maximum(la, cc_) + jnp.log(1.0 + jnp.exp(-jnp.abs(la - cc_)))
        k_ref[rows, :] = 1.0 - jnp.exp(logf)
        b = sum(jnp.dot(tri_ref[...], p, preferred_element_type=F32) for p in _split3(logf))
        b_ref[rows, :] = b
        q_ref[rows, :] = _silu(hq_ref[rows, :].astype(F32))
        bm = b[mid_row:mid_row + 1, :]
        bl = b[bl_row:bl_row + 1, :]
        tmax = jnp.maximum(tmax, jnp.maximum(jnp.max(jnp.abs(bm)), jnp.max(jnp.abs(bl - bm))))
    safe = tmax <= HG_SAFE

    def chunk(cc, fast):
        rows = slice(cc * c, (cc + 1) * c)
        b = b_ref[rows, :]
        kk = k_ref[rows, :]
        q = q_ref[rows, :]
        v = hi_ref[rows, :]
        blast = b_ref[cc * c + bl_row:cc * c + bl_row + 1, :]
        bmid = b_ref[cc * c + mid_row:cc * c + mid_row + 1, :]
        brefs = {}
        n = SUBLANES
        while not fast and n < c:
            parts = []
            for m in range(c // (2 * n)):
                r = cc * c + 2 * n * m + (n if rev else n - 1)
                parts.append(jnp.broadcast_to(b_ref[r:r + 1, :], (2 * n, HG_HEADS * HG_D)))
            brefs[n] = jnp.concatenate(parts, axis=0) if len(parts) > 1 else parts[0]
            n *= 2
        outs = []
        for h in range(HG_HEADS):
            hs = slice(h * HG_D, (h + 1) * HG_D)
            qh, kh, bh, vh, blh = q[:, hs], kk[:, hs], b[:, hs], v[:, hs], blast[:, hs]
            st = st_ref[h]
            if fast:
                bmh = bmid[:, hs]
                r1 = 0.5 * bmh
                r2 = 0.5 * (bmh + blh)
                ref = jnp.where(first_rows, r1, r2)
                qn = qh * jnp.exp(bh - ref)
                kn = kh * jnp.exp(ref - bh)
                sn = lax.dot_general(qn.astype(BF16), kn.astype(BF16), (((1,), (1,)), ((), ())),
                                     preferred_element_type=F32)
                e = jnp.exp(-jnp.abs(bh - bmh))
                sx = lax.dot_general((qh * e).astype(BF16), (kh * e).astype(BF16),
                                     (((1,), (1,)), ((), ())), preferred_element_type=F32)
                a = jnp.where(same_half, sn, jnp.where(cross_half, sx, 0.0))
                qe = (qn * jnp.where(first_rows, jnp.exp(r1), jnp.exp(r2))).astype(BF16)
                ke = (kn * jnp.where(first_rows, jnp.exp(blh - r1), jnp.exp(blh - r2))).astype(BF16)
                o = lax.dot_general(qe, st.astype(BF16), (((1,), (1,)), ((), ())),
                                    preferred_element_type=F32)
                o = o + jnp.dot(a.astype(BF16), vh, preferred_element_type=F32)
            else:
                o, ke = _hgrn_levels(qh, kh, bh, vh, blh, st, brefs, hs, lv, row8, c, rev)
            outs.append(o)
            kv = lax.dot_general(vh, ke, (((0,), (0,)), ((), ())), preferred_element_type=F32)
            st_ref[h] = st * jnp.exp(blh) + kv
        o_ref[rows, :] = jnp.concatenate(outs, axis=1)

    order = [nch - 1 - ci if rev else ci for ci in range(nch)]

    @pl.when(safe)
    def _():
        for cc in order:
            chunk(cc, True)

    @pl.when(jnp.logical_not(safe))
    def _():
        for cc in order:
            chunk(cc, False)

    if not has_s0:
        @pl.when(j == pl.num_programs(1) - 1)
        def _():
            for h in range(HG_HEADS):
                so_ref[0, 0, 0, h] = st_ref[h].T


def _hgrn_levels(qh, kh, bh, vh, blh, st, brefs, hs, lv, row8, c, rev):
    qe = (qh * jnp.exp(bh)).astype(BF16)
    o = lax.dot_general(qe, st.astype(BF16), (((1,), (1,)), ((), ())), preferred_element_type=F32)
    b3 = bh.reshape(c // SUBLANES, SUBLANES, HG_D)
    a = jnp.zeros((c, c), F32)
    n, i = 1, 0
    while n < c:
        if n < SUBLANES:
            bref = None
            for g in range(SUBLANES // (2 * n)):
                r = 2 * n * g + (n if rev else n - 1)
                bc = jnp.broadcast_to(b3[:, r:r + 1, :], b3.shape)
                bref = bc if bref is None else jnp.where(row8 >= 2 * n * g, bc, bref)
            bref = bref.reshape(c, HG_D)
        else:
            bref = brefs[n][:, hs]
        e = jnp.exp(-jnp.abs(bh - bref))
        sn = lax.dot_general((qh * e).astype(BF16), (kh * e).astype(BF16),
                             (((1,), (1,)), ((), ())), preferred_element_type=F32)
        a = jnp.where(lv == i, sn, a)
        n, i = 2 * n, i + 1
    o = o + jnp.dot(a.astype(BF16), vh, preferred_element_type=F32)
    o = o + jnp.sum(qh * kh, axis=-1, keepdims=True) * vh.astype(F32)
    return o, (kh * jnp.exp(blh - bh)).astype(BF16)


def _hgrn(zb, zf, lbp, s0, state_out, *, nb, seq, rev, layer, depth, c, lb_rows):
    t = zb.shape[0]
    nblk = seq // lb_rows
    nch = lb_rows // c
    tri, lv = _hgrn_tables(c, rev)
    d = int(rev)

    def rowmap(b, j):
        return b * nblk + (nblk - 1 - j if rev else j)

    in_specs = [
        pl.BlockSpec((lb_rows, CH), lambda b, j: (rowmap(b, j), ZB_HQ)),
        pl.BlockSpec((lb_rows, CH), lambda b, j: (rowmap(b, j), d)),
        pl.BlockSpec((lb_rows, CH), lambda b, j: (rowmap(b, j), ZB_HI)),
        _const_spec((2, CH)),
        _const_spec((c, c)),
        _const_spec((c, c)),
    ]
    args = [zb, zf, zb, lbp, jnp.asarray(tri, BF16), jnp.asarray(lv)]
    has_s0 = s0 is not None
    if has_s0:
        in_specs.append(pl.BlockSpec((1, HG_HEADS, HG_D, HG_D), lambda b, j: (b, 0, 0, 0)))
        args.append(s0)
    sshape = (nb, depth, 2, HG_HEADS, HG_D, HG_D)
    aliases = {}
    if state_out is not None:
        in_specs.append(pl.BlockSpec(memory_space=pl.ANY))
        args.append(state_out)
        aliases = {len(args) - 1: 1}
    kern = functools.partial(_hgrn_kernel, c=c, nch=nch, rev=rev, has_s0=has_s0)
    if state_out is not None:
        kern = functools.partial(_drop_ref, kern, len(args) - 1)
    out_specs = [pl.BlockSpec((lb_rows, CH), lambda b, j: (rowmap(b, j), 0))]
    out_shape = [jax.ShapeDtypeStruct((t, CH), F32)]
    if not has_s0:
        out_specs.append(pl.BlockSpec((1, 1, 1, HG_HEADS, HG_D, HG_D), lambda b, j: (b, layer, d, 0, 0, 0)))
        out_shape.append(jax.ShapeDtypeStruct(sshape, F32))
    outs = pl.pallas_call(
        kern,
        grid=(nb, nblk),
        in_specs=in_specs,
        out_specs=out_specs,
        out_shape=out_shape,
        scratch_shapes=[pltpu.VMEM((HG_HEADS, HG_D, HG_D), F32)] + [pltpu.VMEM((lb_rows, CH), F32)] * 3,
        input_output_aliases=aliases,
        compiler_params=_cparams(("parallel", "arbitrary")),
        name="hgrn_bwd" if rev else "hgrn_fwd",
    )(*args)
    return (outs[0], None) if has_s0 else tuple(outs)


def _drop_ref(kern, pos, *refs):
    return kern(*refs[:pos], *refs[pos + 1:])


def _head_mask(a):
    lane = lax.broadcasted_iota(jnp.int32, (1, 2 * NA_HD), 1)
    return (lane // NA_HD) == a


def _qk(q, k):
    return lax.dot_general(q, k, (((1,), (1,)), ((), ())), preferred_element_type=F32)


def _attn_ctx_kernel(q_ref, k_ref, v_ref, o_ref):
    w = 2 * NA_HD
    for pr in range(NA_HEADS // 2):
        ps = slice(pr * w, (pr + 1) * w)
        q = q_ref[:, ps] * (NA_HD ** -0.5)
        k = k_ref[:, ps]
        v = v_ref[:, ps]
        outs = []
        for a in range(2):
            s = _qk(jnp.where(_head_mask(a), q, jnp.zeros_like(q)), k)
            p = jnp.exp(s - jnp.max(s, axis=-1, keepdims=True))
            l = jnp.sum(p, axis=-1, keepdims=True)
            outs.append(jnp.dot(p.astype(BF16), v, preferred_element_type=F32) / l)
        o_ref[:, ps] = jnp.where(_head_mask(0), outs[0], outs[1]).astype(BF16)


def _attn_ctx(zb, nb, seq):
    return pl.pallas_call(
        _attn_ctx_kernel,
        grid=(nb,),
        in_specs=[
            pl.BlockSpec((seq, CH), lambda b: (b, ZB_NQ)),
            pl.BlockSpec((seq, CH), lambda b: (b, ZB_NK)),
            pl.BlockSpec((seq, CH), lambda b: (b, ZB_NV)),
        ],
        out_specs=pl.BlockSpec((seq, CH), lambda b: (b, 0)),
        out_shape=jax.ShapeDtypeStruct((nb * seq, CH), BF16),
        compiler_params=_cparams(("parallel",)),
        name="attn_ctx",
    )(zb, zb, zb)


NA_TR = 8


def _na_geometry(rows):
    kr = min(WIN_R, rows)
    span = min(NA_TR + WIN_R, rows)
    ntile = rows // NA_TR
    pats, var, starts = [], [], []
    for t in range(ntile):
        r0 = t * NA_TR
        start = int(np.clip(r0 - kr // 2, 0, rows - span))
        qr = r0 + np.arange(NA_TR)
        row_start = np.clip(qr - kr // 2, 0, rows - kr)
        krow = start + np.arange(span)
        rv = (krow[None, :] >= row_start[:, None]) & (krow[None, :] < row_start[:, None] + kr)
        d_row = np.where(rv, krow[None, :] - qr[:, None] + WIN_R - 1, -1)
        for vi, p in enumerate(pats):
            if np.array_equal(p, d_row):
                var.append(vi)
                break
        else:
            var.append(len(pats))
            pats.append(d_row)
        starts.append(start)
    return pats, np.asarray(var, np.int32), np.asarray(starts, np.int32), span


def _na_col_tables(rpb):
    qc = np.arange(GRID_W)
    col_start = np.clip(qc - WIN_C // 2, 0, GRID_W - WIN_C)
    cv = (qc[None, :] >= col_start[:, None]) & (qc[None, :] < col_start[:, None] + WIN_C)
    d_col = qc[None, :] - qc[:, None] + WIN_C - 1
    onehot = (d_col[None] == np.arange(2 * WIN_C - 1)[:, None, None]) & cv[None]
    t = jnp.einsum("hrd,dqk->hrqk", rpb, jnp.asarray(onehot, F32), precision=lax.Precision.HIGHEST)
    t = jnp.where(jnp.asarray(cv)[None, None], t, NEG)
    return jnp.concatenate([t, t], axis=-1)


def _na_kernel(var_ref, start_ref, q_ref, k_ref, v_ref, kc_ref, vc_ref, tab_ref, o_ref, bias_ref,
               *, nkey, pats):
    t = pl.program_id(2)
    var = var_ref[t]
    changed = jnp.logical_or(t == 0, var != var_ref[jnp.maximum(t - 1, 0)])
    left = lax.broadcasted_iota(jnp.int32, (GRID_W, 2 * GRID_W), 1) < GRID_W
    neg = jnp.full((GRID_W, 2 * GRID_W), NEG, F32)

    for vi, d_row in enumerate(pats):
        @pl.when(jnp.logical_and(changed, var == vi))
        def _(d_row=d_row):
            for a in range(2):
                for i in range(d_row.shape[0]):
                    for j in range(0, d_row.shape[1], 2):
                        r0, r1 = int(d_row[i, j]), int(d_row[i, j + 1])
                        b0 = tab_ref[a, r0] if r0 >= 0 else neg
                        b1 = tab_ref[a, r1] if r1 >= 0 else neg
                        blk = neg if (r0 < 0 and r1 < 0) else jnp.where(left, b0, b1)
                        bias_ref[a, i * GRID_W:(i + 1) * GRID_W, j * GRID_W:(j + 2) * GRID_W] = blk

    st = pl.multiple_of(start_ref[t] * GRID_W, GRID_W)
    q = q_ref[...] * (NA_HD ** -0.5)
    kw = k_ref[pl.ds(st, nkey), :]
    vw = v_ref[pl.ds(st, nkey), :]
    kc = kc_ref[0]
    vc = vc_ref[0]
    outs = []
    for a in range(2):
        qa = jnp.where(_head_mask(a), q, jnp.zeros_like(q))
        s1 = _qk(qa, kw) + bias_ref[a]
        s2 = _qk(qa, kc)
        m = jnp.maximum(jnp.max(s1, axis=-1, keepdims=True), jnp.max(s2, axis=-1, keepdims=True))
        p1 = jnp.exp(s1 - m)
        p2 = jnp.exp(s2 - m)
        l = jnp.sum(p1, axis=-1, keepdims=True) + jnp.sum(p2, axis=-1, keepdims=True)
        o = (jnp.dot(p1.astype(BF16), vw, preferred_element_type=F32)
             + jnp.dot(p2.astype(BF16), vc, preferred_element_type=F32))
        outs.append(o / l)
    o_ref[...] = jnp.where(_head_mask(0), outs[0], outs[1]).astype(BF16)


def _na_latent(zb, kctx, vctx, rpb, nb, seq):
    rows = seq // GRID_W
    pats, var, starts, span = _na_geometry(rows)
    nkey = span * GRID_W
    tq = NA_TR * GRID_W
    ntile = rows // NA_TR
    hp = NA_HEADS // 2
    w = 2 * NA_HD
    past = kctx.shape[1]
    assert span % 2 == 0
    tabs = _na_col_tables(rpb)
    nrow = tabs.shape[1]
    grid_spec = pltpu.PrefetchScalarGridSpec(
        num_scalar_prefetch=2,
        grid=(nb, hp, ntile),
        in_specs=[
            pl.BlockSpec((tq, w), lambda b, p, t, *_: (b * ntile + t, ZB_NQ * hp + p)),
            pl.BlockSpec((seq, w), lambda b, p, t, *_: (b, ZB_NK * hp + p)),
            pl.BlockSpec((seq, w), lambda b, p, t, *_: (b, ZB_NV * hp + p)),
            pl.BlockSpec((1, past, w), lambda b, p, t, *_: (b, 0, p)),
            pl.BlockSpec((1, past, w), lambda b, p, t, *_: (b, 0, p)),
            pl.BlockSpec((2, nrow, GRID_W, 2 * GRID_W), lambda b, p, t, *_: (p, 0, 0, 0)),
        ],
        out_specs=pl.BlockSpec((tq, w), lambda b, p, t, *_: (b * ntile + t, p)),
        scratch_shapes=[pltpu.VMEM((2, tq, nkey), F32)],
    )
    return pl.pallas_call(
        functools.partial(_na_kernel, nkey=nkey, pats=pats),
        grid_spec=grid_spec,
        out_shape=jax.ShapeDtypeStruct((nb * seq, CH), BF16),
        compiler_params=_cparams(("parallel", "parallel", "arbitrary")),
        name="na_latent",
    )(jnp.asarray(var), jnp.asarray(starts), zb, zb, zb, kctx, vctx, tabs)


def _merge_kernel(x_ref, mod_ref, gate_ref, yft_ref, of_ref, ob_ref, hg_ref, on_ref, hn_ref,
                  wfo_ref, who_ref, wno_ref, wout_ref, o_ref):
    d = x_ref.shape[1]
    yft = jnp.concatenate([yft_ref[g] for g in range(FT_GROUPS)], axis=1)
    y_ft = jnp.dot(yft.astype(BF16), wfo_ref[...], preferred_element_type=F32)
    o = of_ref[...] + ob_ref[...]
    parts = []
    for h in range(HG_HEADS):
        oh = o[:, h * HG_D:(h + 1) * HG_D]
        ms = jnp.mean(oh * oh, axis=-1, keepdims=True)
        parts.append(oh * lax.rsqrt(ms + EPS))
    oh = jnp.concatenate(parts, axis=1) * hn_ref[...]
    oh = (oh * _silu(hg_ref[...].astype(F32))).astype(BF16)
    y_hg = jnp.dot(oh, who_ref[...], preferred_element_type=F32)
    y_na = jnp.dot(on_ref[...], wno_ref[...], preferred_element_type=F32)
    g = gate_ref[...].astype(F32)
    merged = (_sigmoid(g[:, :d]) * y_ft + _sigmoid(g[:, d:2 * d]) * y_hg
              + _sigmoid(g[:, 2 * d:]) * y_na)
    y = jnp.dot(merged.astype(BF16), wout_ref[...], preferred_element_type=F32)
    o_ref[...] = x_ref[...] + mod_ref[0, 2:3, :] * y


def _merge(x, mod, zb, yft, o_f, o_b, o_n, hg_norm, w_fo, w_ho, w_no, w_out, *, rows_per_seg):
    t, d = x.shape
    tm = ROW_TILE
    spt = rows_per_seg // tm
    assert 3 * d == 6 * CH, "the three merge gates fill the first six bf16 splits"
    row = lambda i: (i, 0)
    return pl.pallas_call(
        _merge_kernel,
        grid=(t // tm,),
        in_specs=[
            pl.BlockSpec((tm, d), row),
            pl.BlockSpec((1, 6, d), lambda i: (i // spt, 0, 0)),
            pl.BlockSpec((tm, 3 * d), row),
            pl.BlockSpec((FT_GROUPS, tm, FT_GD), lambda i: (0, i, 0)),
            pl.BlockSpec((tm, CH), row),
            pl.BlockSpec((tm, CH), row),
            pl.BlockSpec((tm, CH), lambda i: (i, ZB_HG)),
            pl.BlockSpec((tm, CH), row),
            _const_spec((1, CH)),
            _const_spec((CH, d)),
            _const_spec((CH, d)),
            _const_spec((CH, d)),
            _const_spec((d, d)),
        ],
        out_specs=pl.BlockSpec((tm, d), row),
        out_shape=jax.ShapeDtypeStruct((t, d), F32),
        compiler_params=_cparams(("parallel",)),
        name="merge",
    )(x, mod, zb, yft, o_f, o_b, zb, o_n, hg_norm.reshape(1, CH), w_fo, w_ho, w_no, w_out)


FF_CHUNK = 256


def _swiglu_acc(h, w1_ref, w3_ref, w2_ref, lead=()):
    dff = w1_ref.shape[-1]
    acc = None
    for c0 in range(0, dff, FF_CHUNK):
        cs = slice(c0, c0 + FF_CHUNK)
        g = jnp.dot(h, w1_ref[lead + (slice(None), cs)], preferred_element_type=F32)
        u = jnp.dot(h, w3_ref[lead + (slice(None), cs)], preferred_element_type=F32)
        a = (_silu(g) * u).astype(BF16)
        y = jnp.dot(a, w2_ref[lead + (cs, slice(None))], preferred_element_type=F32)
        acc = y if acc is None else acc + y
    return acc


def _ffn_kernel(x_ref, g_ref, mod_ref, w1_ref, w3_ref, w2_ref, *refs, final):
    o_ref = refs[-1]
    x = x_ref[...]
    h = _norm_mod(x, g_ref[...], mod_ref[0, 3:4, :], mod_ref[0, 4:5, :]).astype(BF16)
    y = x + mod_ref[0, 5:6, :] * _swiglu_acc(h, w1_ref, w3_ref, w2_ref)
    if final:
        ms = jnp.mean(y * y, axis=-1, keepdims=True)
        y = y * lax.rsqrt(ms + EPS) * refs[0][...]
    o_ref[...] = y


def _ffn_dense(x, g, mod, w1, w3, w2, norm_final, *, rows_per_seg):
    t, d = x.shape
    dff = w1.shape[1]
    tm = ROW_TILE
    spt = rows_per_seg // tm
    final = norm_final is not None
    in_specs = [
        pl.BlockSpec((tm, d), lambda i: (i, 0)),
        _const_spec((1, d)),
        pl.BlockSpec((1, 6, d), lambda i: (i // spt, 0, 0)),
        _const_spec((d, dff)),
        _const_spec((d, dff)),
        _const_spec((dff, d)),
    ]
    args = [x, g.reshape(1, d), mod, w1, w3, w2]
    if final:
        in_specs.append(_const_spec((1, d)))
        args.append(norm_final.reshape(1, d))
    return pl.pallas_call(
        functools.partial(_ffn_kernel, final=final),
        grid=(t // tm,),
        in_specs=in_specs,
        out_specs=pl.BlockSpec((tm, d), lambda i: (i, 0)),
        out_shape=jax.ShapeDtypeStruct((t, d), F32),
        compiler_params=_cparams(("parallel",)),
        name="ffn_dense",
    )(*args)


def _store_token_tiles(ref, lead, val):
    n, d = val.shape
    assert d == SUBLANES * LANES
    for j in range(SUBLANES):
        ref[lead + (pl.ds(j, n, stride=SUBLANES), slice(None))] = val[:, j * LANES:(j + 1) * LANES]


def _load_token_tiles(ref, lead, n):
    return jnp.concatenate([ref[lead + (pl.ds(j, n, stride=SUBLANES), slice(None))]
                            for j in range(SUBLANES)], axis=1)


def _router_kernel(x_ref, g_ref, mod_ref, wr_ref, h_ref, r_ref):
    h = _norm_mod(x_ref[...], g_ref[...], mod_ref[0, 3:4, :], mod_ref[0, 4:5, :])
    _store_token_tiles(h_ref, (), h)
    logits = jnp.dot(h, wr_ref[...], precision=lax.Precision.HIGHEST,
                     preferred_element_type=F32)
    lane = lax.broadcasted_iota(jnp.int32, logits.shape, 1)
    logits = jnp.where(lane < N_EXPERTS, logits, -jnp.inf)
    m1 = jnp.max(logits, axis=-1, keepdims=True)
    i1 = jnp.min(jnp.where(logits == m1, lane, LANES), axis=-1, keepdims=True)
    rest = jnp.where(lane == i1, -jnp.inf, logits)
    m2 = jnp.max(rest, axis=-1, keepdims=True)
    i2 = jnp.min(jnp.where(rest == m2, lane, LANES), axis=-1, keepdims=True)
    e2 = jnp.exp(m2 - m1)
    w1 = 1.0 / (1.0 + e2)
    w2 = e2 / (1.0 + e2)
    r = jnp.where(lane == 0, i1.astype(F32), jnp.where(lane == 1, i2.astype(F32),
                  jnp.where(lane == 2, w1, jnp.where(lane == 3, w2, 0.0))))
    r_ref[...] = r


def _router(x, g, mod, w_router, *, rows_per_seg, h_all, row_off, total):
    t, d = x.shape
    tm = ROW_TILE
    spt = rows_per_seg // tm
    off = row_off // tm
    wr = jnp.zeros((d, LANES), F32).at[:, :N_EXPERTS].set(w_router)
    in_specs = [
        pl.BlockSpec((tm, d), lambda i: (i, 0)),
        _const_spec((1, d)),
        pl.BlockSpec((1, 6, d), lambda i: (i // spt, 0, 0)),
        _const_spec((d, LANES)),
    ]
    args = [x, g.reshape(1, d), mod, wr]
    kern, aliases = _router_kernel, {}
    if h_all is not None:
        in_specs.append(pl.BlockSpec(memory_space=pl.ANY))
        args.append(h_all)
        kern, aliases = functools.partial(_drop_ref, _router_kernel, 4), {4: 0}
    return pl.pallas_call(
        kern,
        grid=(t // tm,),
        in_specs=in_specs,
        out_specs=[pl.BlockSpec((tm * SUBLANES, LANES), lambda i: (i + off, 0)),
                   pl.BlockSpec((tm, LANES), lambda i: (i, 0))],
        out_shape=[jax.ShapeDtypeStruct((total * SUBLANES, LANES), F32),
                   jax.ShapeDtypeStruct((t, LANES), F32)],
        input_output_aliases=aliases,
        compiler_params=_cparams(("parallel",)),
        name="router",
    )(*args)


MOE_TM = 512


def _moe_ffn_kernel(te_ref, nt_ref, src_ref, h_hbm, w1_ref, w3_ref, w2_ref, y_ref, hbuf, sem):
    i = pl.program_id(0)
    nt = nt_ref[0]
    slot = i % 2

    def gather(tile, s):
        base = tile * MOE_TM

        def issue(r, carry):
            row = pl.multiple_of(src_ref[base + r], SUBLANES)
            pltpu.make_async_copy(h_hbm.at[pl.ds(row, SUBLANES)],
                                  hbuf.at[s, pl.ds(pl.multiple_of(r * SUBLANES, SUBLANES), SUBLANES)],
                                  sem.at[s]).start()
            return carry

        lax.fori_loop(0, MOE_TM, issue, 0, unroll=8)

    @pl.when(jnp.logical_and(i == 0, nt > 0))
    def _():
        gather(0, 0)

    @pl.when(i < nt)
    def _():
        @pl.when(i + 1 < nt)
        def _():
            gather(i + 1, 1 - slot)

        pltpu.make_async_copy(h_hbm.at[pl.ds(0, MOE_TM * SUBLANES)], hbuf.at[slot], sem.at[slot]).wait()
        h = _load_token_tiles(hbuf, (slot,), MOE_TM).astype(BF16)
        _store_token_tiles(y_ref, (), _swiglu_acc(h, w1_ref, w3_ref, w2_ref, lead=(0,)))

    @pl.when(i >= nt)
    def _():
        y_ref[...] = jnp.zeros_like(y_ref)


def _moe_ffn(h, tile_expert, n_tiles, src, w1, w3, w2, ntile_max):
    d = SUBLANES * LANES
    dff = w1.shape[-1]
    grid_spec = pltpu.PrefetchScalarGridSpec(
        num_scalar_prefetch=3,
        grid=(ntile_max,),
        in_specs=[
            pl.BlockSpec(memory_space=pl.ANY),
            pl.BlockSpec((1, d, dff), lambda i, te, nt, src: (te[i], 0, 0)),
            pl.BlockSpec((1, d, dff), lambda i, te, nt, src: (te[i], 0, 0)),
            pl.BlockSpec((1, dff, d), lambda i, te, nt, src: (te[i], 0, 0)),
        ],
        out_specs=pl.BlockSpec((MOE_TM * SUBLANES, LANES), lambda i, te, nt, src: (i, 0)),
        scratch_shapes=[pltpu.VMEM((2, MOE_TM * SUBLANES, LANES), F32), pltpu.SemaphoreType.DMA((2,))],
    )
    return pl.pallas_call(
        _moe_ffn_kernel,
        grid_spec=grid_spec,
        out_shape=jax.ShapeDtypeStruct((ntile_max * MOE_TM * SUBLANES, LANES), F32),
        compiler_params=_cparams(("arbitrary",)),
        name="moe_ffn",
    )(tile_expert, n_tiles, src, h, w1, w3, w2)


def _combine_kernel(pos_ref, x_ref, mod_ref, r_ref, y_hbm, *refs, final, tm):
    o_ref, ybuf, sem = refs[-3:]
    i = pl.program_id(0)
    slot = i % 2

    def gather(tile, s):
        base = 2 * tile * tm

        def issue(r, carry):
            dst = pl.ds(pl.multiple_of(r * SUBLANES, SUBLANES), SUBLANES)
            for k in range(2):
                row = pl.multiple_of(pos_ref[base + 2 * r + k], SUBLANES)
                pltpu.make_async_copy(y_hbm.at[pl.ds(row, SUBLANES)], ybuf.at[s, k, dst],
                                      sem.at[s]).start()
            return carry

        lax.fori_loop(0, tm, issue, 0, unroll=4)

    @pl.when(i == 0)
    def _():
        gather(0, 0)

    @pl.when(i + 1 < pl.num_programs(0))
    def _():
        gather(i + 1, 1 - slot)

    for k in range(2):
        pltpu.make_async_copy(y_hbm.at[pl.ds(0, tm * SUBLANES)], ybuf.at[slot, k], sem.at[slot]).wait()
    r = r_ref[...]
    f = (r[:, 2:3] * _load_token_tiles(ybuf, (slot, 0), tm)
         + r[:, 3:4] * _load_token_tiles(ybuf, (slot, 1), tm))
    y = x_ref[...] + mod_ref[0, 5:6, :] * f
    if final:
        ms = jnp.mean(y * y, axis=-1, keepdims=True)
        y = y * lax.rsqrt(ms + EPS) * refs[0][...]
    o_ref[...] = y


def _moe_combine(x, mod, route, pos, y_sorted, norm_final, *, rows_per_seg):
    t, d = x.shape
    tm = 256
    spt = rows_per_seg // tm
    final = norm_final is not None
    in_specs = [
        pl.BlockSpec((tm, d), lambda i, pos: (i, 0)),
        pl.BlockSpec((1, 6, d), lambda i, pos: (i // spt, 0, 0)),
        pl.BlockSpec((tm, LANES), lambda i, pos: (i, 0)),
        pl.BlockSpec(memory_space=pl.ANY),
    ]
    args = [x, mod, route, y_sorted]
    if final:
        in_specs.append(pl.BlockSpec((1, d), lambda i, pos: (0, 0)))
        args.append(norm_final.reshape(1, d))
    grid_spec = pltpu.PrefetchScalarGridSpec(
        num_scalar_prefetch=1,
        grid=(t // tm,),
        in_specs=in_specs,
        out_specs=pl.BlockSpec((tm, d), lambda i, pos: (i, 0)),
        scratch_shapes=[pltpu.VMEM((2, 2, tm * SUBLANES, LANES), F32), pltpu.SemaphoreType.DMA((2,))],
    )
    return pl.pallas_call(
        functools.partial(_combine_kernel, final=final, tm=tm),
        grid_spec=grid_spec,
        out_shape=jax.ShapeDtypeStruct((t, d), F32),
        compiler_params=_cparams(("arbitrary",)),
        name="moe_combine",
    )(pos.reshape(-1), *args)


def _moe(sets, g, w_router, w1, w3, w2, norm_final):
    t = sum(x.shape[0] for x, _, _ in sets)
    h, routes, off = None, [], 0
    for x, mod, seg in sets:
        h, r = _router(x, g, mod, w_router, rows_per_seg=seg, h_all=h, row_off=off, total=t)
        routes.append(r)
        off += x.shape[0]
    route = jnp.concatenate(routes, axis=0)
    ids = route[:, :2].astype(jnp.int32)
    onehot = (ids[:, :, None] == jnp.arange(N_EXPERTS)[None, None, :]).astype(jnp.int32)
    sel = onehot.sum(axis=1)
    counts = sel.sum(axis=0)
    rank = jnp.cumsum(sel, axis=0) - sel
    padded = ((counts + MOE_TM - 1) // MOE_TM) * MOE_TM
    pstart = jnp.cumsum(padded) - padded
    cstart = jnp.cumsum(counts) - counts
    pos_e = pstart[None, :] + rank
    pos = jnp.take_along_axis(pos_e, ids, axis=1).astype(jnp.int32)
    ntile_max = (2 * t) // MOE_TM + N_EXPERTS
    tok = jnp.arange(t, dtype=jnp.int32)
    key = jnp.sort((ids * t + tok[:, None]).reshape(-1))
    sorted_tok = key % t
    tile_start = jnp.arange(ntile_max, dtype=jnp.int32) * MOE_TM
    pend = pstart + padded
    tile_expert = jnp.minimum(jnp.sum(tile_start[:, None] >= pend[None, :], axis=1), N_EXPERTS - 1)
    n_tiles = (jnp.sum(padded) // MOE_TM).astype(jnp.int32).reshape(1)
    p = jnp.arange(ntile_max * MOE_TM, dtype=jnp.int32)
    e_p = jnp.repeat(tile_expert, MOE_TM)
    r_p = jnp.minimum(p - pstart[e_p], jnp.maximum(counts[e_p] - 1, 0))
    src = sorted_tok[jnp.clip(cstart[e_p] + r_p, 0, 2 * t - 1)].astype(jnp.int32) * SUBLANES
    pos = pos * SUBLANES
    y_sorted = _moe_ffn(h, tile_expert.astype(jnp.int32), n_tiles, src, w1, w3, w2, ntile_max)
    outs, off = [], 0
    for (x, mod, seg), r in zip(sets, routes):
        n = x.shape[0]
        outs.append(_moe_combine(x, mod, r, pos[off:off + n], y_sorted, norm_final, rows_per_seg=seg))
        off += n
    return outs


def _mixer(x, mod, l, depth, p, *, nb, seq, seg, ctx, caches, state_out):
    is_ctx = ctx is None
    outs = _in_proj(x, p["norm_mix"][l], mod, p["w_in"][l], rows_per_seg=seg, seq=seq, layer=l,
                    depth=depth, caches=caches if is_ctx else None)
    zb, zf = outs[0], outs[1]
    new_caches = tuple(outs[2:]) if is_ctx else None
    if is_ctx:
        yft = _fourier_ctx(zb, nb, seq)
        o_n = _attn_ctx(zb, nb, seq)
    else:
        yft = _fourier_lat(zb, nb, seq)
        o_n = _na_latent(zb, ctx[0], ctx[1], p["na_rpb"][l], nb, seq)
    hg_c = 128
    lbr = 256 if is_ctx else 512
    lbr = min(lbr, seq)
    o_f, state_out = _hgrn(zb, zf, p["lbp"][l][0], None if is_ctx else ctx[2],
                           state_out if is_ctx else None, nb=nb, seq=seq, rev=False, layer=l,
                           depth=depth, c=hg_c, lb_rows=lbr)
    o_b, state_out = _hgrn(zb, zf, p["lbp"][l][1], None if is_ctx else ctx[3],
                           state_out if is_ctx else None, nb=nb, seq=seq, rev=True, layer=l,
                           depth=depth, c=hg_c, lb_rows=lbr)
    x = _merge(x, mod, zb, yft, o_f, o_b, o_n, p["hg_norm"][l], p["w_fo"][l], p["w_ho"][l],
               p["w_no"][l], p["w_out"][l], rows_per_seg=seg)
    return x, new_caches, state_out


def kernel(x_prompt, x_sample, cache_k, cache_v, state_hgrn, c, c_ctx, w_mod, b_mod, norm_mix,
           norm_ffn, w_in, hg_lb, hg_norm, na_rpb, w_fo, w_ho, w_no, w_out, w1_d, w3_d, w2_d,
           w_router, w1_e, w3_e, w2_e, norm_final):
    nb_c, seq_c, d = x_prompt.shape
    nb_l, seq_l, _ = x_sample.shape
    depth = w_in.shape[0]
    past = cache_k.shape[2]

    lbs = jnp.cumsum(jax.nn.softmax(hg_lb.astype(F32), axis=0), axis=0)
    lbs = lbs - lbs[0:1]
    lbp = jnp.stack([jnp.log(lbs), jnp.log1p(-lbs)], axis=2)

    bf = lambda w: w.astype(BF16)
    p = dict(norm_mix=norm_mix, norm_ffn=norm_ffn, w_in=bf(w_in), hg_norm=hg_norm, na_rpb=na_rpb,
             w_fo=bf(w_fo), w_ho=bf(w_ho), w_no=bf(w_no), w_out=bf(w_out), w1_d=bf(w1_d),
             w3_d=bf(w3_d), w2_d=bf(w2_d), w_router=w_router, w1_e=bf(w1_e), w3_e=bf(w3_e),
             w2_e=bf(w2_e), lbp=lbp)

    mods = _mod_vectors(jnp.concatenate([c_ctx[None], c], axis=0), w_mod, b_mod)

    x = x_prompt.reshape(nb_c * seq_c, d)
    xs = x_sample.reshape(nb_l * seq_l, d)
    seg_c, seg_l = nb_c * seq_c, seq_l
    caches, state_out = (None, None), None
    for l in range(depth):
        mod_c, mod_l = mods[l, 0:1], mods[l, 1:]
        x, caches, state_out = _mixer(x, mod_c, l, depth, p, nb=nb_c, seq=seq_c, seg=seg_c, ctx=None,
                                      caches=caches, state_out=state_out)
        ctx = (bf(cache_k[:, l].reshape(nb_l, past, CH)), bf(cache_v[:, l].reshape(nb_l, past, CH)),
               state_hgrn[:, l, 0], state_hgrn[:, l, 1])
        xs, _, _ = _mixer(xs, mod_l, l, depth, p, nb=nb_l, seq=seq_l, seg=seg_l, ctx=ctx, caches=None,
                          state_out=None)
        nf = norm_final if l == depth - 1 else None
        i = l // 2
        if l % 2 == 0:
            x = _ffn_dense(x, norm_ffn[l], mod_c, p["w1_d"][i], p["w3_d"][i], p["w2_d"][i], nf,
                           rows_per_seg=seg_c)
            xs = _ffn_dense(xs, norm_ffn[l], mod_l, p["w1_d"][i], p["w3_d"][i], p["w2_d"][i], nf,
                            rows_per_seg=seg_l)
        else:
            x, xs = _moe([(x, mod_c, seg_c), (xs, mod_l, seg_l)], norm_ffn[l], w_router[i],
                         p["w1_e"][i], p["w3_e"][i], p["w2_e"][i], nf)
    y_prompt = x.reshape(nb_c, seq_c, d)
    y_sample = xs.reshape(nb_l, seq_l, d)

    new_cache_k = caches[0].reshape(nb_c, depth, seq_c, NA_HEADS, NA_HD)
    new_cache_v = caches[1].reshape(nb_c, depth, seq_c, NA_HEADS, NA_HD)
    return (y_prompt, y_sample, new_cache_k, new_cache_v, state_out)
```

```python
import functools
import math

import numpy as np
import jax
import jax.numpy as jnp
from jax import lax
from jax.experimental import pallas as pl
from jax.experimental.pallas import tpu as pltpu

F32 = jnp.float32
BF16 = jnp.bfloat16

LANES = 128
SUBLANES = 8
VMEM_LIMIT = 56 * 1024 * 1024
ROW_TILE = 512

FT_GROUPS = 4
FT_GD = 128
HG_HEADS = 4
HG_D = 128
NA_HEADS = 8
NA_HD = 64
GRID_W = 64
WIN_R = 8
WIN_C = 16
N_EXPERTS = 8
EPS = 1e-6
CH = 512
NEG = -1e30

ZB_SRC = (9, 10, 11, 12, 13, 14, 0, 1, 4, 5, 6, 7, 8)
ZB_UFT, ZB_HQ, ZB_HI, ZB_HG, ZB_NQ, ZB_NK, ZB_NV = 6, 7, 8, 9, 10, 11, 12
ZF_SRC = (2, 3)
SRC_NK, SRC_NV = 7, 8


def _cparams(sem):
    return pltpu.CompilerParams(dimension_semantics=sem, vmem_limit_bytes=VMEM_LIMIT)


def _const_spec(shape):
    nd = len(shape)
    return pl.BlockSpec(shape, lambda *_: (0,) * nd, pipeline_mode=pl.Buffered(1))


def _silu(x):
    return x * (1.0 / (1.0 + jnp.exp(-x)))


def _sigmoid(x):
    return 1.0 / (1.0 + jnp.exp(-x))


def _norm_mod(x, g, shift, scale):
    ms = jnp.mean(x * x, axis=-1, keepdims=True)
    return (x * lax.rsqrt(ms + EPS) * g) * (1.0 + scale) + shift


def _mod_kernel(c_ref, w_ref, b_ref, o_ref, *, nv, tn):
    w = w_ref[0]
    for v in range(nv):
        s = _silu(c_ref[v])
        s = jnp.concatenate([s] * (tn // LANES), axis=1)
        o_ref[0, v:v + 1, :] = jnp.sum(w * s, axis=0, keepdims=True) + b_ref[0]


def _mod_vectors(cvecs, w_mod, b_mod):
    nv, d = cvecs.shape
    depth, _, n6 = w_mod.shape
    tn = 512
    cb = jnp.broadcast_to(cvecs[:, :, None], (nv, d, LANES))
    out = pl.pallas_call(
        functools.partial(_mod_kernel, nv=nv, tn=tn),
        grid=(depth, n6 // tn),
        in_specs=[
            pl.BlockSpec((nv, d, LANES), lambda l, j: (0, 0, 0)),
            pl.BlockSpec((1, d, tn), lambda l, j: (l, 0, j)),
            pl.BlockSpec((1, 1, tn), lambda l, j: (l, 0, j)),
        ],
        out_specs=pl.BlockSpec((1, nv, tn), lambda l, j: (l, 0, j)),
        out_shape=jax.ShapeDtypeStruct((depth, nv, n6), F32),
        compiler_params=_cparams(("parallel", "parallel")),
        name="mod_vectors",
    )(cb, w_mod, b_mod.reshape(depth, 1, n6))
    return out.reshape(depth, nv, 6, d)


def _in_kernel(x_ref, g_ref, mod_ref, w_ref, *refs, with_cache):
    if with_cache:
        zb_ref, zf_ref, ck_ref, cv_ref = refs[-4:]
    else:
        zb_ref, zf_ref = refs[-2:]
    h = _norm_mod(x_ref[...], g_ref[...], mod_ref[0, 0:1, :], mod_ref[0, 1:2, :]).astype(BF16)
    for src in range(15):
        z = jnp.dot(h, w_ref[:, src * CH:(src + 1) * CH], preferred_element_type=F32)
        if src in ZB_SRC:
            j = ZB_SRC.index(src)
            zb_ref[:, j * CH:(j + 1) * CH] = z.astype(BF16)
        if src in ZF_SRC:
            j = ZF_SRC.index(src)
            zf_ref[:, j * CH:(j + 1) * CH] = z
        if with_cache and src in (SRC_NK, SRC_NV):
            c_ref = ck_ref if src == SRC_NK else cv_ref
            nseq, _, seq, _ = c_ref.shape
            for s in range(nseq):
                c_ref[s, 0] = z[s * seq:(s + 1) * seq]


def _in_proj(x, g, mod, w_bf, *, rows_per_seg, seq, layer, depth, caches):
    t, d = x.shape
    tm = ROW_TILE
    with_cache = caches is not None
    spt = rows_per_seg // tm
    in_specs = [
        pl.BlockSpec((tm, d), lambda i: (i, 0)),
        _const_spec((1, d)),
        pl.BlockSpec((1, 6, d), lambda i: (i // spt, 0, 0)),
        _const_spec((d, 15 * CH)),
    ]
    args = [x, g.reshape(1, d), mod, w_bf]
    out_specs = [
        pl.BlockSpec((tm, len(ZB_SRC) * CH), lambda i: (i, 0)),
        pl.BlockSpec((tm, len(ZF_SRC) * CH), lambda i: (i, 0)),
    ]
    out_shape = [
        jax.ShapeDtypeStruct((t, len(ZB_SRC) * CH), BF16),
        jax.ShapeDtypeStruct((t, len(ZF_SRC) * CH), F32),
    ]
    aliases = {}
    if with_cache:
        assert tm % seq == 0, "a row tile holds whole context sequences"
        cspec = pl.BlockSpec((tm // seq, 1, seq, CH), lambda i: (i, layer, 0, 0))
        out_specs += [cspec, cspec]
        cshape = jax.ShapeDtypeStruct((t // seq, depth, seq, CH), F32)
        out_shape += [cshape, cshape]
        if caches[0] is not None:
            in_specs += [pl.BlockSpec(memory_space=pl.ANY)] * 2
            args += list(caches)
            aliases = {4: 2, 5: 3}
    return pl.pallas_call(
        functools.partial(_in_kernel, with_cache=with_cache),
        grid=(t // tm,),
        in_specs=in_specs,
        out_specs=out_specs,
        out_shape=out_shape,
        input_output_aliases=aliases,
        compiler_params=_cparams(("parallel",)),
        name="in_proj",
    )(*args)


def _dft_tables(n, scale):
    k = np.arange(n)
    ang = 2.0 * np.pi * ((k[:, None] * k[None, :]) % n) / n
    return np.cos(ang) * scale, np.sin(ang) * scale


def _ft_ctx_kernel(u_ref, cs_ref, lm_ref, y_ref):
    u = u_ref[...]
    cs = cs_ref[...].astype(BF16)
    zc, zs = [], []
    for g in range(FT_GROUPS):
        z = jnp.dot(u[:, g * FT_GD:(g + 1) * FT_GD], cs, preferred_element_type=F32)
        zc.append(z[:, :FT_GD])
        zs.append(z[:, FT_GD:])
    zz = jnp.concatenate([jnp.concatenate(zc, axis=1), jnp.concatenate(zs, axis=1)], axis=0)
    y = jnp.dot(lm_ref[...].astype(BF16), zz.astype(BF16), preferred_element_type=F32)
    for g in range(FT_GROUPS):
        y_ref[g] = y[:, g * FT_GD:(g + 1) * FT_GD]


def _fourier_ctx(zb, nb, seq):
    cc, sc = _dft_tables(FT_GD, FT_GD ** -0.5)
    cl, sl = _dft_tables(seq, seq ** -0.5)
    cs = jnp.asarray(np.concatenate([cc, sc], axis=1), F32)
    lm = jnp.asarray(np.concatenate([cl, -sl], axis=1), F32)
    return pl.pallas_call(
        _ft_ctx_kernel,
        grid=(nb,),
        in_specs=[
            pl.BlockSpec((seq, CH), lambda b: (b, ZB_UFT)),
            _const_spec(cs.shape),
            _const_spec(lm.shape),
        ],
        out_specs=pl.BlockSpec((FT_GROUPS, seq, FT_GD), lambda b: (0, b, 0)),
        out_shape=jax.ShapeDtypeStruct((FT_GROUPS, nb * seq, FT_GD), F32),
        compiler_params=_cparams(("parallel",)),
        name="fourier_ctx",
    )(zb, cs, lm)


def _ft_lat_kernel(u_ref, cs_ref, w1_ref, twc_ref, tws_ref, w2_ref, y_ref, zr_ref, zi_ref,
                   ar_ref, ai_ref, *, n, lw):
    rows = n * n
    rc = min(rows, 512)
    ng = lw // FT_GD
    cs = cs_ref[...].astype(BF16)
    w1 = w1_ref[...].astype(BF16)
    w2 = w2_ref[...].astype(BF16)
    for r0 in range(0, rows, rc):
        for g in range(ng):
            z = jnp.dot(u_ref[r0:r0 + rc, g * FT_GD:(g + 1) * FT_GD], cs,
                        preferred_element_type=F32)
            zr_ref[g, r0:r0 + rc, :] = z[:, :FT_GD]
            zi_ref[g, r0:r0 + rc, :] = -z[:, FT_GD:]

    def gather(re_ref, im_ref, idx):
        xr = jnp.concatenate([re_ref[g, idx, :] for g in range(ng)], axis=1)
        xi = jnp.concatenate([im_ref[g, idx, :] for g in range(ng)], axis=1)
        return jnp.concatenate([xr, xi], axis=0).astype(BF16)

    def stage1(a, carry):
        idx = pl.ds(a, n, stride=n)
        y = jnp.dot(w1, gather(zr_ref, zi_ref, idx), preferred_element_type=F32)
        yr, yi = y[:n], y[n:]
        tc = jnp.concatenate([twc_ref[a]] * ng, axis=1)
        ts = jnp.concatenate([tws_ref[a]] * ng, axis=1)
        zr = yr * tc - yi * ts
        zi = yr * ts + yi * tc
        for g in range(ng):
            ar_ref[g, idx, :] = zr[:, g * FT_GD:(g + 1) * FT_GD]
            ai_ref[g, idx, :] = zi[:, g * FT_GD:(g + 1) * FT_GD]
        return carry

    lax.fori_loop(0, n, stage1, 0, unroll=4)

    def stage2(q, carry):
        r0 = pl.multiple_of(q * n, n)
        y = jnp.dot(w2, gather(ar_ref, ai_ref, pl.ds(r0, n)), preferred_element_type=F32)
        for g in range(ng):
            y_ref[g, pl.ds(q, n, stride=n), :] = y[:, g * FT_GD:(g + 1) * FT_GD]
        return carry

    lax.fori_loop(0, n, stage2, 0, unroll=4)


def _fourier_lat(zb, nb, seq):
    n = math.isqrt(seq)
    assert n * n == seq and n % SUBLANES == 0
    lw = 256
    cc, sc = _dft_tables(FT_GD, FT_GD ** -0.5)
    cs = jnp.asarray(np.concatenate([cc, sc], axis=1), F32)
    pr, ps = _dft_tables(n, 1.0 / n)
    w1 = jnp.asarray(np.block([[pr, ps], [-ps, pr]]), F32)
    k = np.arange(n)
    ang = 2.0 * np.pi * (k[:, None] * k[None, :]) / seq
    twc = jnp.asarray(np.broadcast_to(np.cos(ang)[:, :, None], (n, n, LANES)), F32)
    tws = jnp.asarray(np.broadcast_to(-np.sin(ang)[:, :, None], (n, n, LANES)), F32)
    wr, ws = _dft_tables(n, 1.0)
    w2 = jnp.asarray(np.concatenate([wr, ws], axis=1), F32)
    nh = CH // lw
    return pl.pallas_call(
        functools.partial(_ft_lat_kernel, n=n, lw=lw),
        grid=(nb, nh),
        in_specs=[
            pl.BlockSpec((seq, lw), lambda b, h: (b, ZB_UFT * nh + h)),
            _const_spec(cs.shape),
            _const_spec(w1.shape),
            _const_spec(twc.shape),
            _const_spec(tws.shape),
            _const_spec(w2.shape),
        ],
        out_specs=pl.BlockSpec((lw // FT_GD, seq, FT_GD), lambda b, h: (h, b, 0)),
        out_shape=jax.ShapeDtypeStruct((FT_GROUPS, nb * seq, FT_GD), F32),
        scratch_shapes=[pltpu.VMEM((lw // FT_GD, seq, FT_GD), F32)] * 4,
        compiler_params=_cparams(("parallel", "parallel")),
        name="fourier_lat",
    )(zb, cs, w1, twc, tws, w2)


HG_SAFE = 160.0
HG_CHUNK = 128
HG_BLOCK = 512


def _hgrn_tables(c, rev):
    t = np.arange(c)
    tri = (t[None, :] >= t[:, None]) if rev else (t[None, :] <= t[:, None])
    lv = np.full((c, c), -1, np.int32)
    n, i = 1, 0
    while n < c:
        same = (t[:, None] // (2 * n)) == (t[None, :] // (2 * n))
        hi_t, hi_s = (t[:, None] % (2 * n)) >= n, (t[None, :] % (2 * n)) >= n
        pair = same & (~hi_t & hi_s if rev else hi_t & ~hi_s)
        lv[pair] = i
        n, i = 2 * n, i + 1
    return tri.astype(np.float32), lv


def _split3(x):
    hi = x.astype(BF16)
    r = x - hi.astype(F32)
    mid = r.astype(BF16)
    lo = (r - mid.astype(F32)).astype(BF16)
    return hi, mid, lo


def _hgrn_kernel(hq_ref, hf_ref, hi_ref, lb_ref, tri_ref, lv_ref, *refs, c, nch, rev, has_s0):
    if has_s0:
        s0_ref, o_ref, st_ref, b_ref, k_ref, q_ref = refs
    else:
        o_ref, so_ref, st_ref, b_ref, k_ref, q_ref = refs
    j = pl.program_id(1)

    @pl.when(j == 0)
    def _():
        for h in range(HG_HEADS):
            st_ref[h] = s0_ref[0, h].T if has_s0 else jnp.zeros((HG_D, HG_D), F32)

    la = lb_ref[0:1, :]
    l1 = lb_ref[1:2, :]
    lv = lv_ref[...]
    row8 = lax.broadcasted_iota(jnp.int32, (c // SUBLANES, SUBLANES, HG_D), 1)
    ti = lax.broadcasted_iota(jnp.int32, (c, c), 0)
    si = lax.broadcasted_iota(jnp.int32, (c, c), 1)
    causal = (si >= ti) if rev else (si <= ti)
    half = c // 2
    same_half = jnp.logical_and(causal, (ti < half) == (si < half))
    cross_half = jnp.logical_and(causal, (ti < half) != (si < half))
    row = lax.broadcasted_iota(jnp.int32, (c, HG_D), 0)
    first_rows = (row >= half) if rev else (row < half)
    bl_row = c - 1 if not rev else 0
    mid_row = half if rev else half - 1

    tmax = jnp.float32(0.0)
    for cc in range(nch):
        rows = slice(cc * c, (cc + 1) * c)
        z = hf_ref[rows, :]
        ls = jnp.minimum(z, 0.0) - jnp.log(1.0 + jnp.exp(-jnp.abs(z)))
        cc_ = l1 + ls
        logf = jnp.maximum(la, cc_) + jnp.log(1.0 + jnp.exp(-jnp.abs(la - cc_)))
        k_ref[rows, :] = 1.0 - jnp.exp(logf)
        b = sum(jnp.dot(tri_ref[...], p, preferred_element_type=F32) for p in _split3(logf))
        b_ref[rows, :] = b
        q_ref[rows, :] = _silu(hq_ref[rows, :].astype(F32))
        bm = b[mid_row:mid_row + 1, :]
        bl = b[bl_row:bl_row + 1, :]
        tmax = jnp.maximum(tmax, jnp.maximum(jnp.max(jnp.abs(bm)), jnp.max(jnp.abs(bl - bm))))
    safe = tmax <= HG_SAFE

    def chunk(cc, fast):
        rows = slice(cc * c, (cc + 1) * c)
        b = b_ref[rows, :]
        kk = k_ref[rows, :]
        q = q_ref[rows, :]
        v = hi_ref[rows, :]
        blast = b_ref[cc * c + bl_row:cc * c + bl_row + 1, :]
        bmid = b_ref[cc * c + mid_row:cc * c + mid_row + 1, :]
        brefs = {}
        n = SUBLANES
        while not fast and n < c:
            parts = []
            for m in range(c // (2 * n)):
                r = cc * c + 2 * n * m + (n if rev else n - 1)
                parts.append(jnp.broadcast_to(b_ref[r:r + 1, :], (2 * n, HG_HEADS * HG_D)))
            brefs[n] = jnp.concatenate(parts, axis=0) if len(parts) > 1 else parts[0]
            n *= 2
        outs = []
        for h in range(HG_HEADS):
            hs = slice(h * HG_D, (h + 1) * HG_D)
            qh, kh, bh, vh, blh = q[:, hs], kk[:, hs], b[:, hs], v[:, hs], blast[:, hs]
            st = st_ref[h]
            if fast:
                bmh = bmid[:, hs]
                r1 = 0.5 * bmh
                r2 = 0.5 * (bmh + blh)
                ref = jnp.where(first_rows, r1, r2)
                qn = qh * jnp.exp(bh - ref)
                kn = kh * jnp.exp(ref - bh)
                sn = lax.dot_general(qn.astype(BF16), kn.astype(BF16), (((1,), (1,)), ((), ())),
                                     preferred_element_type=F32)
                e = jnp.exp(-jnp.abs(bh - bmh))
                sx = lax.dot_general((qh * e).astype(BF16), (kh * e).astype(BF16),
                                     (((1,), (1,)), ((), ())), preferred_element_type=F32)
                a = jnp.where(same_half, sn, jnp.where(cross_half, sx, 0.0))
                qe = (qn * jnp.where(first_rows, jnp.exp(r1), jnp.exp(r2))).astype(BF16)
                ke = (kn * jnp.where(first_rows, jnp.exp(blh - r1), jnp.exp(blh - r2))).astype(BF16)
                o = lax.dot_general(qe, st.astype(BF16), (((1,), (1,)), ((), ())),
                                    preferred_element_type=F32)
                o = o + jnp.dot(a.astype(BF16), vh, preferred_element_type=F32)
            else:
                o, ke = _hgrn_levels(qh, kh, bh, vh, blh, st, brefs, hs, lv, row8, c, rev)
            outs.append(o)
            kv = lax.dot_general(vh, ke, (((0,), (0,)), ((), ())), preferred_element_type=F32)
            st_ref[h] = st * jnp.exp(blh) + kv
        o_ref[rows, :] = jnp.concatenate(outs, axis=1)

    order = [nch - 1 - ci if rev else ci for ci in range(nch)]

    @pl.when(safe)
    def _():
        for cc in order:
            chunk(cc, True)

    @pl.when(jnp.logical_not(safe))
    def _():
        for cc in order:
            chunk(cc, False)

    if not has_s0:
        @pl.when(j == pl.num_programs(1) - 1)
        def _():
            for h in range(HG_HEADS):
                so_ref[0, 0, 0, h] = st_ref[h].T


def _hgrn_levels(qh, kh, bh, vh, blh, st, brefs, hs, lv, row8, c, rev):
    qe = (qh * jnp.exp(bh)).astype(BF16)
    o = lax.dot_general(qe, st.astype(BF16), (((1,), (1,)), ((), ())), preferred_element_type=F32)
    b3 = bh.reshape(c // SUBLANES, SUBLANES, HG_D)
    a = jnp.zeros((c, c), F32)
    n, i = 1, 0
    while n < c:
        if n < SUBLANES:
            bref = None
            for g in range(SUBLANES // (2 * n)):
                r = 2 * n * g + (n if rev else n - 1)
                bc = jnp.broadcast_to(b3[:, r:r + 1, :], b3.shape)
                bref = bc if bref is None else jnp.where(row8 >= 2 * n * g, bc, bref)
            bref = bref.reshape(c, HG_D)
        else:
            bref = brefs[n][:, hs]
        e = jnp.exp(-jnp.abs(bh - bref))
        sn = lax.dot_general((qh * e).astype(BF16), (kh * e).astype(BF16),
                             (((1,), (1,)), ((), ())), preferred_element_type=F32)
        a = jnp.where(lv == i, sn, a)
        n, i = 2 * n, i + 1
    o = o + jnp.dot(a.astype(BF16), vh, preferred_element_type=F32)
    o = o + jnp.sum(qh * kh, axis=-1, keepdims=True) * vh.astype(F32)
    return o, (kh * jnp.exp(blh - bh)).astype(BF16)


def _hgrn(zb, zf, lbp, s0, state_out, *, nb, seq, rev, layer, depth, c, lb_rows):
    t = zb.shape[0]
    nblk = seq // lb_rows
    nch = lb_rows // c
    tri, lv = _hgrn_tables(c, rev)
    d = int(rev)

    def rowmap(b, j):
        return b * nblk + (nblk - 1 - j if rev else j)

    in_specs = [
        pl.BlockSpec((lb_rows, CH), lambda b, j: (rowmap(b, j), ZB_HQ)),
        pl.BlockSpec((lb_rows, CH), lambda b, j: (rowmap(b, j), d)),
        pl.BlockSpec((lb_rows, CH), lambda b, j: (rowmap(b, j), ZB_HI)),
        _const_spec((2, CH)),
        _const_spec((c, c)),
        _const_spec((c, c)),
    ]
    args = [zb, zf, zb, lbp, jnp.asarray(tri, BF16), jnp.asarray(lv)]
    has_s0 = s0 is not None
    if has_s0:
        in_specs.append(pl.BlockSpec((1, HG_HEADS, HG_D, HG_D), lambda b, j: (b, 0, 0, 0)))
        args.append(s0)
    sshape = (nb, depth, 2, HG_HEADS, HG_D, HG_D)
    aliases = {}
    if state_out is not None:
        in_specs.append(pl.BlockSpec(memory_space=pl.ANY))
        args.append(state_out)
        aliases = {len(args) - 1: 1}
    kern = functools.partial(_hgrn_kernel, c=c, nch=nch, rev=rev, has_s0=has_s0)
    if state_out is not None:
        kern = functools.partial(_drop_ref, kern, len(args) - 1)
    out_specs = [pl.BlockSpec((lb_rows, CH), lambda b, j: (rowmap(b, j), 0))]
    out_shape = [jax.ShapeDtypeStruct((t, CH), F32)]
    if not has_s0:
        out_specs.append(pl.BlockSpec((1, 1, 1, HG_HEADS, HG_D, HG_D), lambda b, j: (b, layer, d, 0, 0, 0)))
        out_shape.append(jax.ShapeDtypeStruct(sshape, F32))
    outs = pl.pallas_call(
        kern,
        grid=(nb, nblk),
        in_specs=in_specs,
        out_specs=out_specs,
        out_shape=out_shape,
        scratch_shapes=[pltpu.VMEM((HG_HEADS, HG_D, HG_D), F32)] + [pltpu.VMEM((lb_rows, CH), F32)] * 3,
        input_output_aliases=aliases,
        compiler_params=_cparams(("parallel", "arbitrary")),
        name="hgrn_bwd" if rev else "hgrn_fwd",
    )(*args)
    return (outs[0], None) if has_s0 else tuple(outs)


def _drop_ref(kern, pos, *refs):
    return kern(*refs[:pos], *refs[pos + 1:])


def _head_mask(a):
    lane = lax.broadcasted_iota(jnp.int32, (1, 2 * NA_HD), 1)
    return (lane // NA_HD) == a


def _qk(q, k):
    return lax.dot_general(q, k, (((1,), (1,)), ((), ())), preferred_element_type=F32)


def _attn_ctx_kernel(q_ref, k_ref, v_ref, o_ref):
    w = 2 * NA_HD
    for pr in range(NA_HEADS // 2):
        ps = slice(pr * w, (pr + 1) * w)
        q = q_ref[:, ps] * (NA_HD ** -0.5)
        k = k_ref[:, ps]
        v = v_ref[:, ps]
        outs = []
        for a in range(2):
            s = _qk(jnp.where(_head_mask(a), q, jnp.zeros_like(q)), k)
            p = jnp.exp(s - jnp.max(s, axis=-1, keepdims=True))
            l = jnp.sum(p, axis=-1, keepdims=True)
            outs.append(jnp.dot(p.astype(BF16), v, preferred_element_type=F32) / l)
        o_ref[:, ps] = jnp.where(_head_mask(0), outs[0], outs[1]).astype(BF16)


def _attn_ctx(zb, nb, seq):
    return pl.pallas_call(
        _attn_ctx_kernel,
        grid=(nb,),
        in_specs=[
            pl.BlockSpec((seq, CH), lambda b: (b, ZB_NQ)),
            pl.BlockSpec((seq, CH), lambda b: (b, ZB_NK)),
            pl.BlockSpec((seq, CH), lambda b: (b, ZB_NV)),
        ],
        out_specs=pl.BlockSpec((seq, CH), lambda b: (b, 0)),
        out_shape=jax.ShapeDtypeStruct((nb * seq, CH), BF16),
        compiler_params=_cparams(("parallel",)),
        name="attn_ctx",
    )(zb, zb, zb)


NA_TR = 8


def _na_geometry(rows):
    kr = min(WIN_R, rows)
    span = min(NA_TR + WIN_R, rows)
    ntile = rows // NA_TR
    pats, var, starts = [], [], []
    for t in range(ntile):
        r0 = t * NA_TR
        start = int(np.clip(r0 - kr // 2, 0, rows - span))
        qr = r0 + np.arange(NA_TR)
        row_start = np.clip(qr - kr // 2, 0, rows - kr)
        krow = start + np.arange(span)
        rv = (krow[None, :] >= row_start[:, None]) & (krow[None, :] < row_start[:, None] + kr)
        d_row = np.where(rv, krow[None, :] - qr[:, None] + WIN_R - 1, -1)
        for vi, p in enumerate(pats):
            if np.array_equal(p, d_row):
                var.append(vi)
                break
        else:
            var.append(len(pats))
            pats.append(d_row)
        starts.append(start)
    return pats, np.asarray(var, np.int32), np.asarray(starts, np.int32), span


def _na_col_tables(rpb):
    qc = np.arange(GRID_W)
    col_start = np.clip(qc - WIN_C // 2, 0, GRID_W - WIN_C)
    cv = (qc[None, :] >= col_start[:, None]) & (qc[None, :] < col_start[:, None] + WIN_C)
    d_col = qc[None, :] - qc[:, None] + WIN_C - 1
    onehot = (d_col[None] == np.arange(2 * WIN_C - 1)[:, None, None]) & cv[None]
    t = jnp.einsum("hrd,dqk->hrqk", rpb, jnp.asarray(onehot, F32), precision=lax.Precision.HIGHEST)
    t = jnp.where(jnp.asarray(cv)[None, None], t, NEG)
    return jnp.concatenate([t, t], axis=-1)


def _na_kernel(var_ref, start_ref, q_ref, k_ref, v_ref, kc_ref, vc_ref, tab_ref, o_ref, bias_ref,
               *, nkey, pats):
    t = pl.program_id(2)
    var = var_ref[t]
    changed = jnp.logical_or(t == 0, var != var_ref[jnp.maximum(t - 1, 0)])
    left = lax.broadcasted_iota(jnp.int32, (GRID_W, 2 * GRID_W), 1) < GRID_W
    neg = jnp.full((GRID_W, 2 * GRID_W), NEG, F32)

    for vi, d_row in enumerate(pats):
        @pl.when(jnp.logical_and(changed, var == vi))
        def _(d_row=d_row):
            for a in range(2):
                for i in range(d_row.shape[0]):
                    for j in range(0, d_row.shape[1], 2):
                        r0, r1 = int(d_row[i, j]), int(d_row[i, j + 1])
                        b0 = tab_ref[a, r0] if r0 >= 0 else neg
                        b1 = tab_ref[a, r1] if r1 >= 0 else neg
                        blk = neg if (r0 < 0 and r1 < 0) else jnp.where(left, b0, b1)
                        bias_ref[a, i * GRID_W:(i + 1) * GRID_W, j * GRID_W:(j + 2) * GRID_W] = blk

    st = pl.multiple_of(start_ref[t] * GRID_W, GRID_W)
    q = q_ref[...] * (NA_HD ** -0.5)
    kw = k_ref[pl.ds(st, nkey), :]
    vw = v_ref[pl.ds(st, nkey), :]
    kc = kc_ref[0]
    vc = vc_ref[0]
    outs = []
    for a in range(2):
        qa = jnp.where(_head_mask(a), q, jnp.zeros_like(q))
        s1 = _qk(qa, kw) + bias_ref[a]
        s2 = _qk(qa, kc)
        m = jnp.maximum(jnp.max(s1, axis=-1, keepdims=True), jnp.max(s2, axis=-1, keepdims=True))
        p1 = jnp.exp(s1 - m)
        p2 = jnp.exp(s2 - m)
        l = jnp.sum(p1, axis=-1, keepdims=True) + jnp.sum(p2, axis=-1, keepdims=True)
        o = (jnp.dot(p1.astype(BF16), vw, preferred_element_type=F32)
             + jnp.dot(p2.astype(BF16), vc, preferred_element_type=F32))
        outs.append(o / l)
    o_ref[...] = jnp.where(_head_mask(0), outs[0], outs[1]).astype(BF16)


def _na_latent(zb, kctx, vctx, rpb, nb, seq):
    rows = seq // GRID_W
    pats, var, starts, span = _na_geometry(rows)
    nkey = span * GRID_W
    tq = NA_TR * GRID_W
    ntile = rows // NA_TR
    hp = NA_HEADS // 2
    w = 2 * NA_HD
    past = kctx.shape[1]
    assert span % 2 == 0
    tabs = _na_col_tables(rpb)
    nrow = tabs.shape[1]
    grid_spec = pltpu.PrefetchScalarGridSpec(
        num_scalar_prefetch=2,
        grid=(nb, hp, ntile),
        in_specs=[
            pl.BlockSpec((tq, w), lambda b, p, t, *_: (b * ntile + t, ZB_NQ * hp + p)),
            pl.BlockSpec((seq, w), lambda b, p, t, *_: (b, ZB_NK * hp + p)),
            pl.BlockSpec((seq, w), lambda b, p, t, *_: (b, ZB_NV * hp + p)),
            pl.BlockSpec((1, past, w), lambda b, p, t, *_: (b, 0, p)),
            pl.BlockSpec((1, past, w), lambda b, p, t, *_: (b, 0, p)),
            pl.BlockSpec((2, nrow, GRID_W, 2 * GRID_W), lambda b, p, t, *_: (p, 0, 0, 0)),
        ],
        out_specs=pl.BlockSpec((tq, w), lambda b, p, t, *_: (b * ntile + t, p)),
        scratch_shapes=[pltpu.VMEM((2, tq, nkey), F32)],
    )
    return pl.pallas_call(
        functools.partial(_na_kernel, nkey=nkey, pats=pats),
        grid_spec=grid_spec,
        out_shape=jax.ShapeDtypeStruct((nb * seq, CH), BF16),
        compiler_params=_cparams(("parallel", "parallel", "arbitrary")),
        name="na_latent",
    )(jnp.asarray(var), jnp.asarray(starts), zb, zb, zb, kctx, vctx, tabs)


def _merge_kernel(x_ref, mod_ref, gate_ref, yft_ref, of_ref, ob_ref, hg_ref, on_ref, hn_ref,
                  wfo_ref, who_ref, wno_ref, wout_ref, o_ref):
    d = x_ref.shape[1]
    yft = jnp.concatenate([yft_ref[g] for g in range(FT_GROUPS)], axis=1)
    y_ft = jnp.dot(yft.astype(BF16), wfo_ref[...], preferred_element_type=F32)
    o = of_ref[...] + ob_ref[...]
    parts = []
    for h in range(HG_HEADS):
        oh = o[:, h * HG_D:(h + 1) * HG_D]
        ms = jnp.mean(oh * oh, axis=-1, keepdims=True)
        parts.append(oh * lax.rsqrt(ms + EPS))
    oh = jnp.concatenate(parts, axis=1) * hn_ref[...]
    oh = (oh * _silu(hg_ref[...].astype(F32))).astype(BF16)
    y_hg = jnp.dot(oh, who_ref[...], preferred_element_type=F32)
    y_na = jnp.dot(on_ref[...], wno_ref[...], preferred_element_type=F32)
    g = gate_ref[...].astype(F32)
    merged = (_sigmoid(g[:, :d]) * y_ft + _sigmoid(g[:, d:2 * d]) * y_hg
              + _sigmoid(g[:, 2 * d:]) * y_na)
    y = jnp.dot(merged.astype(BF16), wout_ref[...], preferred_element_type=F32)
    o_ref[...] = x_ref[...] + mod_ref[0, 2:3, :] * y


def _merge(x, mod, zb, yft, o_f, o_b, o_n, hg_norm, w_fo, w_ho, w_no, w_out, *, rows_per_seg):
    t, d = x.shape
    tm = ROW_TILE
    spt = rows_per_seg // tm
    assert 3 * d == 6 * CH, "the three merge gates fill the first six bf16 splits"
    row = lambda i: (i, 0)
    return pl.pallas_call(
        _merge_kernel,
        grid=(t // tm,),
        in_specs=[
            pl.BlockSpec((tm, d), row),
            pl.BlockSpec((1, 6, d), lambda i: (i // spt, 0, 0)),
            pl.BlockSpec((tm, 3 * d), row),
            pl.BlockSpec((FT_GROUPS, tm, FT_GD), lambda i: (0, i, 0)),
            pl.BlockSpec((tm, CH), row),
            pl.BlockSpec((tm, CH), row),
            pl.BlockSpec((tm, CH), lambda i: (i, ZB_HG)),
            pl.BlockSpec((tm, CH), row),
            _const_spec((1, CH)),
            _const_spec((CH, d)),
            _const_spec((CH, d)),
            _const_spec((CH, d)),
            _const_spec((d, d)),
        ],
        out_specs=pl.BlockSpec((tm, d), row),
        out_shape=jax.ShapeDtypeStruct((t, d), F32),
        compiler_params=_cparams(("parallel",)),
        name="merge",
    )(x, mod, zb, yft, o_f, o_b, zb, o_n, hg_norm.reshape(1, CH), w_fo, w_ho, w_no, w_out)


FF_CHUNK = 256


def _swiglu_acc(h, w1_ref, w3_ref, w2_ref, lead=()):
    dff = w1_ref.shape[-1]
    acc = None
    for c0 in range(0, dff, FF_CHUNK):
        cs = slice(c0, c0 + FF_CHUNK)
        g = jnp.dot(h, w1_ref[lead + (slice(None), cs)], preferred_element_type=F32)
        u = jnp.dot(h, w3_ref[lead + (slice(None), cs)], preferred_element_type=F32)
        a = (_silu(g) * u).astype(BF16)
        y = jnp.dot(a, w2_ref[lead + (cs, slice(None))], preferred_element_type=F32)
        acc = y if acc is None else acc + y
    return acc


def _ffn_kernel(x_ref, g_ref, mod_ref, w1_ref, w3_ref, w2_ref, *refs, final):
    o_ref = refs[-1]
    x = x_ref[...]
    h = _norm_mod(x, g_ref[...], mod_ref[0, 3:4, :], mod_ref[0, 4:5, :]).astype(BF16)
    y = x + mod_ref[0, 5:6, :] * _swiglu_acc(h, w1_ref, w3_ref, w2_ref)
    if final:
        ms = jnp.mean(y * y, axis=-1, keepdims=True)
        y = y * lax.rsqrt(ms + EPS) * refs[0][...]
    o_ref[...] = y


def _ffn_dense(x, g, mod, w1, w3, w2, norm_final, *, rows_per_seg):
    t, d = x.shape
    dff = w1.shape[1]
    tm = ROW_TILE
    spt = rows_per_seg // tm
    final = norm_final is not None
    in_specs = [
        pl.BlockSpec((tm, d), lambda i: (i, 0)),
        _const_spec((1, d)),
        pl.BlockSpec((1, 6, d), lambda i: (i // spt, 0, 0)),
        _const_spec((d, dff)),
        _const_spec((d, dff)),
        _const_spec((dff, d)),
    ]
    args = [x, g.reshape(1, d), mod, w1, w3, w2]
    if final:
        in_specs.append(_const_spec((1, d)))
        args.append(norm_final.reshape(1, d))
    return pl.pallas_call(
        functools.partial(_ffn_kernel, final=final),
        grid=(t // tm,),
        in_specs=in_specs,
        out_specs=pl.BlockSpec((tm, d), lambda i: (i, 0)),
        out_shape=jax.ShapeDtypeStruct((t, d), F32),
        compiler_params=_cparams(("parallel",)),
        name="ffn_dense",
    )(*args)


def _store_token_tiles(ref, lead, val):
    n, d = val.shape
    assert d == SUBLANES * LANES
    for j in range(SUBLANES):
        ref[lead + (pl.ds(j, n, stride=SUBLANES), slice(None))] = val[:, j * LANES:(j + 1) * LANES]


def _load_token_tiles(ref, lead, n):
    return jnp.concatenate([ref[lead + (pl.ds(j, n, stride=SUBLANES), slice(None))]
                            for j in range(SUBLANES)], axis=1)


def _router_kernel(x_ref, g_ref, mod_ref, wr_ref, h_ref, r_ref):
    h = _norm_mod(x_ref[...], g_ref[...], mod_ref[0, 3:4, :], mod_ref[0, 4:5, :])
    _store_token_tiles(h_ref, (), h)
    logits = jnp.dot(h, wr_ref[...], precision=lax.Precision.HIGHEST,
                     preferred_element_type=F32)
    lane = lax.broadcasted_iota(jnp.int32, logits.shape, 1)
    logits = jnp.where(lane < N_EXPERTS, logits, -jnp.inf)
    m1 = jnp.max(logits, axis=-1, keepdims=True)
    i1 = jnp.min(jnp.where(logits == m1, lane, LANES), axis=-1, keepdims=True)
    rest = jnp.where(lane == i1, -jnp.inf, logits)
    m2 = jnp.max(rest, axis=-1, keepdims=True)
    i2 = jnp.min(jnp.where(rest == m2, lane, LANES), axis=-1, keepdims=True)
    e2 = jnp.exp(m2 - m1)
    w1 = 1.0 / (1.0 + e2)
    w2 = e2 / (1.0 + e2)
    r = jnp.where(lane == 0, i1.astype(F32), jnp.where(lane == 1, i2.astype(F32),
                  jnp.where(lane == 2, w1, jnp.where(lane == 3, w2, 0.0))))
    r_ref[...] = r


def _router(x, g, mod, w_router, *, rows_per_seg, h_all, row_off, total):
    t, d = x.shape
    tm = ROW_TILE
    spt = rows_per_seg // tm
    off = row_off // tm
    wr = jnp.zeros((d, LANES), F32).at[:, :N_EXPERTS].set(w_router)
    in_specs = [
        pl.BlockSpec((tm, d), lambda i: (i, 0)),
        _const_spec((1, d)),
        pl.BlockSpec((1, 6, d), lambda i: (i // spt, 0, 0)),
        _const_spec((d, LANES)),
    ]
    args = [x, g.reshape(1, d), mod, wr]
    kern, aliases = _router_kernel, {}
    if h_all is not None:
        in_specs.append(pl.BlockSpec(memory_space=pl.ANY))
        args.append(h_all)
        kern, aliases = functools.partial(_drop_ref, _router_kernel, 4), {4: 0}
    return pl.pallas_call(
        kern,
        grid=(t // tm,),
        in_specs=in_specs,
        out_specs=[pl.BlockSpec((tm * SUBLANES, LANES), lambda i: (i + off, 0)),
                   pl.BlockSpec((tm, LANES), lambda i: (i, 0))],
        out_shape=[jax.ShapeDtypeStruct((total * SUBLANES, LANES), F32),
                   jax.ShapeDtypeStruct((t, LANES), F32)],
        input_output_aliases=aliases,
        compiler_params=_cparams(("parallel",)),
        name="router",
    )(*args)


MOE_TM = 512


def _moe_ffn_kernel(te_ref, nt_ref, src_ref, h_hbm, w1_ref, w3_ref, w2_ref, y_ref, hbuf, sem):
    i = pl.program_id(0)
    nt = nt_ref[0]
    slot = i % 2

    def gather(tile, s):
        base = tile * MOE_TM

        def issue(r, carry):
            row = pl.multiple_of(src_ref[base + r], SUBLANES)
            pltpu.make_async_copy(h_hbm.at[pl.ds(row, SUBLANES)],
                                  hbuf.at[s, pl.ds(pl.multiple_of(r * SUBLANES, SUBLANES), SUBLANES)],
                                  sem.at[s]).start()
            return carry

        lax.fori_loop(0, MOE_TM, issue, 0, unroll=8)

    @pl.when(jnp.logical_and(i == 0, nt > 0))
    def _():
        gather(0, 0)

    @pl.when(i < nt)
    def _():
        @pl.when(i + 1 < nt)
        def _():
            gather(i + 1, 1 - slot)

        pltpu.make_async_copy(h_hbm.at[pl.ds(0, MOE_TM * SUBLANES)], hbuf.at[slot], sem.at[slot]).wait()
        h = _load_token_tiles(hbuf, (slot,), MOE_TM).astype(BF16)
        _store_token_tiles(y_ref, (), _swiglu_acc(h, w1_ref, w3_ref, w2_ref, lead=(0,)))

    @pl.when(i >= nt)
    def _():
        y_ref[...] = jnp.zeros_like(y_ref)


def _moe_ffn(h, tile_expert, n_tiles, src, w1, w3, w2, ntile_max):
    d = SUBLANES * LANES
    dff = w1.shape[-1]
    grid_spec = pltpu.PrefetchScalarGridSpec(
        num_scalar_prefetch=3,
        grid=(ntile_max,),
        in_specs=[
            pl.BlockSpec(memory_space=pl.ANY),
            pl.BlockSpec((1, d, dff), lambda i, te, nt, src: (te[i], 0, 0)),
            pl.BlockSpec((1, d, dff), lambda i, te, nt, src: (te[i], 0, 0)),
            pl.BlockSpec((1, dff, d), lambda i, te, nt, src: (te[i], 0, 0)),
        ],
        out_specs=pl.BlockSpec((MOE_TM * SUBLANES, LANES), lambda i, te, nt, src: (i, 0)),
        scratch_shapes=[pltpu.VMEM((2, MOE_TM * SUBLANES, LANES), F32), pltpu.SemaphoreType.DMA((2,))],
    )
    return pl.pallas_call(
        _moe_ffn_kernel,
        grid_spec=grid_spec,
        out_shape=jax.ShapeDtypeStruct((ntile_max * MOE_TM * SUBLANES, LANES), F32),
        compiler_params=_cparams(("arbitrary",)),
        name="moe_ffn",
    )(tile_expert, n_tiles, src, h, w1, w3, w2)


def _combine_kernel(pos_ref, x_ref, mod_ref, r_ref, y_hbm, *refs, final, tm):
    o_ref, ybuf, sem = refs[-3:]
    i = pl.program_id(0)
    slot = i % 2

    def gather(tile, s):
        base = 2 * tile * tm

        def issue(r, carry):
            dst = pl.ds(pl.multiple_of(r * SUBLANES, SUBLANES), SUBLANES)
            for k in range(2):
                row = pl.multiple_of(pos_ref[base + 2 * r + k], SUBLANES)
                pltpu.make_async_copy(y_hbm.at[pl.ds(row, SUBLANES)], ybuf.at[s, k, dst],
                                      sem.at[s]).start()
            return carry

        lax.fori_loop(0, tm, issue, 0, unroll=8)

    @pl.when(i == 0)
    def _():
        gather(0, 0)

    @pl.when(i + 1 < pl.num_programs(0))
    def _():
        gather(i + 1, 1 - slot)

    for k in range(2):
        pltpu.make_async_copy(y_hbm.at[pl.ds(0, tm * SUBLANES)], ybuf.at[slot, k], sem.at[slot]).wait()
    r = r_ref[...]
    f = (r[:, 2:3] * _load_token_tiles(ybuf, (slot, 0), tm)
         + r[:, 3:4] * _load_token_tiles(ybuf, (slot, 1), tm))
    y = x_ref[...] + mod_ref[0, 5:6, :] * f
    if final:
        ms = jnp.mean(y * y, axis=-1, keepdims=True)
        y = y * lax.rsqrt(ms + EPS) * refs[0][...]
    o_ref[...] = y


def _moe_combine(x, mod, route, pos, y_sorted, norm_final, *, rows_per_seg):
    t, d = x.shape
    tm = MOE_TM // 2
    spt = rows_per_seg // tm
    final = norm_final is not None
    in_specs = [
        pl.BlockSpec((tm, d), lambda i, pos: (i, 0)),
        pl.BlockSpec((1, 6, d), lambda i, pos: (i // spt, 0, 0)),
        pl.BlockSpec((tm, LANES), lambda i, pos: (i, 0)),
        pl.BlockSpec(memory_space=pl.ANY),
    ]
    args = [x, mod, route, y_sorted]
    if final:
        in_specs.append(pl.BlockSpec((1, d), lambda i, pos: (0, 0)))
        args.append(norm_final.reshape(1, d))
    grid_spec = pltpu.PrefetchScalarGridSpec(
        num_scalar_prefetch=1,
        grid=(t // tm,),
        in_specs=in_specs,
        out_specs=pl.BlockSpec((tm, d), lambda i, pos: (i, 0)),
        scratch_shapes=[pltpu.VMEM((2, 2, tm * SUBLANES, LANES), F32), pltpu.SemaphoreType.DMA((2,))],
    )
    return pl.pallas_call(
        functools.partial(_combine_kernel, final=final, tm=tm),
        grid_spec=grid_spec,
        out_shape=jax.ShapeDtypeStruct((t, d), F32),
        compiler_params=_cparams(("arbitrary",)),
        name="moe_combine",
    )(pos.reshape(-1), *args)


def _moe(sets, g, w_router, w1, w3, w2, norm_final):
    t = sum(x.shape[0] for x, _, _ in sets)
    h, routes, off = None, [], 0
    for x, mod, seg in sets:
        h, r = _router(x, g, mod, w_router, rows_per_seg=seg, h_all=h, row_off=off, total=t)
        routes.append(r)
        off += x.shape[0]
    route = jnp.concatenate(routes, axis=0)
    ids = route[:, :2].astype(jnp.int32)
    onehot = (ids[:, :, None] == jnp.arange(N_EXPERTS)[None, None, :]).astype(jnp.int32)
    sel = onehot.sum(axis=1)
    counts = sel.sum(axis=0)
    rank = jnp.cumsum(sel, axis=0) - sel
    padded = ((counts + MOE_TM - 1) // MOE_TM) * MOE_TM
    pstart = jnp.cumsum(padded) - padded
    cstart = jnp.cumsum(counts) - counts
    pos_e = pstart[None, :] + rank
    pos = jnp.take_along_axis(pos_e, ids, axis=1).astype(jnp.int32)
    ntile_max = (2 * t) // MOE_TM + N_EXPERTS
    tok = jnp.arange(t, dtype=jnp.int32)
    key = jnp.sort((ids * t + tok[:, None]).reshape(-1))
    sorted_tok = key % t
    tile_start = jnp.arange(ntile_max, dtype=jnp.int32) * MOE_TM
    pend = pstart + padded
    tile_expert = jnp.minimum(jnp.sum(tile_start[:, None] >= pend[None, :], axis=1), N_EXPERTS - 1)
    n_tiles = (jnp.sum(padded) // MOE_TM).astype(jnp.int32).reshape(1)
    p = jnp.arange(ntile_max * MOE_TM, dtype=jnp.int32)
    e_p = jnp.repeat(tile_expert, MOE_TM)
    r_p = jnp.minimum(p - pstart[e_p], jnp.maximum(counts[e_p] - 1, 0))
    src = sorted_tok[jnp.clip(cstart[e_p] + r_p, 0, 2 * t - 1)].astype(jnp.int32) * SUBLANES
    pos = pos * SUBLANES
    y_sorted = _moe_ffn(h, tile_expert.astype(jnp.int32), n_tiles, src, w1, w3, w2, ntile_max)
    outs, off = [], 0
    for (x, mod, seg), r in zip(sets, routes):
        n = x.shape[0]
        outs.append(_moe_combine(x, mod, r, pos[off:off + n], y_sorted, norm_final, rows_per_seg=seg))
        off += n
    return outs


def _mixer(x, mod, l, depth, p, *, nb, seq, seg, ctx, caches, state_out):
    is_ctx = ctx is None
    outs = _in_proj(x, p["norm_mix"][l], mod, p["w_in"][l], rows_per_seg=seg, seq=seq, layer=l,
                    depth=depth, caches=caches if is_ctx else None)
    zb, zf = outs[0], outs[1]
    new_caches = tuple(outs[2:]) if is_ctx else None
    if is_ctx:
        yft = _fourier_ctx(zb, nb, seq)
        o_n = _attn_ctx(zb, nb, seq)
    else:
        yft = _fourier_lat(zb, nb, seq)
        o_n = _na_latent(zb, ctx[0], ctx[1], p["na_rpb"][l], nb, seq)
    hg_c = HG_CHUNK
    lbr = min(HG_BLOCK, seq)
    o_f, state_out = _hgrn(zb, zf, p["lbp"][l][0], None if is_ctx else ctx[2],
                           state_out if is_ctx else None, nb=nb, seq=seq, rev=False, layer=l,
                           depth=depth, c=hg_c, lb_rows=lbr)
    o_b, state_out = _hgrn(zb, zf, p["lbp"][l][1], None if is_ctx else ctx[3],
                           state_out if is_ctx else None, nb=nb, seq=seq, rev=True, layer=l,
                           depth=depth, c=hg_c, lb_rows=lbr)
    x = _merge(x, mod, zb, yft, o_f, o_b, o_n, p["hg_norm"][l], p["w_fo"][l], p["w_ho"][l],
               p["w_no"][l], p["w_out"][l], rows_per_seg=seg)
    return x, new_caches, state_out


def kernel(x_prompt, x_sample, cache_k, cache_v, state_hgrn, c, c_ctx, w_mod, b_mod, norm_mix,
           norm_ffn, w_in, hg_lb, hg_norm, na_rpb, w_fo, w_ho, w_no, w_out, w1_d, w3_d, w2_d,
           w_router, w1_e, w3_e, w2_e, norm_final):
    nb_c, seq_c, d = x_prompt.shape
    nb_l, seq_l, _ = x_sample.shape
    depth = w_in.shape[0]
    past = cache_k.shape[2]

    lbs = jnp.cumsum(jax.nn.softmax(hg_lb.astype(F32), axis=0), axis=0)
    lbs = lbs - lbs[0:1]
    lbp = jnp.stack([jnp.log(lbs), jnp.log1p(-lbs)], axis=2)

    bf = lambda w: w.astype(BF16)
    p = dict(norm_mix=norm_mix, norm_ffn=norm_ffn, w_in=bf(w_in), hg_norm=hg_norm, na_rpb=na_rpb,
             w_fo=bf(w_fo), w_ho=bf(w_ho), w_no=bf(w_no), w_out=bf(w_out), w1_d=bf(w1_d),
             w3_d=bf(w3_d), w2_d=bf(w2_d), w_router=w_router, w1_e=bf(w1_e), w3_e=bf(w3_e),
             w2_e=bf(w2_e), lbp=lbp)

    mods = _mod_vectors(jnp.concatenate([c_ctx[None], c], axis=0), w_mod, b_mod)

    x = x_prompt.reshape(nb_c * seq_c, d)
    xs = x_sample.reshape(nb_l * seq_l, d)
    seg_c, seg_l = nb_c * seq_c, seq_l
    caches, state_out = (None, None), None
    for l in range(depth):
        mod_c, mod_l = mods[l, 0:1], mods[l, 1:]
        x, caches, state_out = _mixer(x, mod_c, l, depth, p, nb=nb_c, seq=seq_c, seg=seg_c, ctx=None,
                                      caches=caches, state_out=state_out)
        ctx = (bf(cache_k[:, l].reshape(nb_l, past, CH)), bf(cache_v[:, l].reshape(nb_l, past, CH)),
               state_hgrn[:, l, 0], state_hgrn[:, l, 1])
        xs, _, _ = _mixer(xs, mod_l, l, depth, p, nb=nb_l, seq=seq_l, seg=seg_l, ctx=ctx, caches=None,
                          state_out=None)
        nf = norm_final if l == depth - 1 else None
        i = l // 2
        if l % 2 == 0:
            x = _ffn_dense(x, norm_ffn[l], mod_c, p["w1_d"][i], p["w3_d"][i], p["w2_d"][i], nf,
                           rows_per_seg=seg_c)
            xs = _ffn_dense(xs, norm_ffn[l], mod_l, p["w1_d"][i], p["w3_d"][i], p["w2_d"][i], nf,
                            rows_per_seg=seg_l)
        else:
            x, xs = _moe([(x, mod_c, seg_c), (xs, mod_l, seg_l)], norm_ffn[l], w_router[i],
                         p["w1_e"][i], p["w3_e"][i], p["w2_e"][i], nf)
    y_prompt = x.reshape(nb_c, seq_c, d)
    y_sample = xs.reshape(nb_l, seq_l, d)

    new_cache_k = caches[0].reshape(nb_c, depth, seq_c, NA_HEADS, NA_HD)
    new_cache_v = caches[1].reshape(nb_c, depth, seq_c, NA_HEADS, NA_HD)
    return (y_prompt, y_sample, new_cache_k, new_cache_v, state_out)
```

```python
import functools
import math

import numpy as np
import jax
import jax.numpy as jnp
from jax import lax
from jax.experimental import pallas as pl
from jax.experimental.pallas import tpu as pltpu

F32 = jnp.float32
BF16 = jnp.bfloat16

LANES = 128
SUBLANES = 8
VMEM_LIMIT = 56 * 1024 * 1024
ROW_TILE = 512

FT_GROUPS = 4
FT_GD = 128
HG_HEADS = 4
HG_D = 128
NA_HEADS = 8
NA_HD = 64
GRID_W = 64
WIN_R = 8
WIN_C = 16
N_EXPERTS = 8
EPS = 1e-6
CH = 512
NEG = -1e30

ZB_SRC = (9, 10, 11, 12, 13, 14, 0, 1, 4, 5, 6, 7, 8)
ZB_UFT, ZB_HQ, ZB_HI, ZB_HG, ZB_NQ, ZB_NK, ZB_NV = 6, 7, 8, 9, 10, 11, 12
ZF_SRC = (2, 3)
SRC_NK, SRC_NV = 7, 8


def _cparams(sem):
    return pltpu.CompilerParams(dimension_semantics=sem, vmem_limit_bytes=VMEM_LIMIT)


def _const_spec(shape):
    nd = len(shape)
    return pl.BlockSpec(shape, lambda *_: (0,) * nd, pipeline_mode=pl.Buffered(1))


def _silu(x):
    return x * (1.0 / (1.0 + jnp.exp(-x)))


def _sigmoid(x):
    return 1.0 / (1.0 + jnp.exp(-x))


def _norm_mod(x, g, shift, scale):
    ms = jnp.mean(x * x, axis=-1, keepdims=True)
    return (x * lax.rsqrt(ms + EPS) * g) * (1.0 + scale) + shift


def _mod_kernel(c_ref, w_ref, b_ref, o_ref, *, nv, tn):
    w = w_ref[0]
    for v in range(nv):
        s = _silu(c_ref[v])
        s = jnp.concatenate([s] * (tn // LANES), axis=1)
        o_ref[0, v:v + 1, :] = jnp.sum(w * s, axis=0, keepdims=True) + b_ref[0]


def _mod_vectors(cvecs, w_mod, b_mod):
    nv, d = cvecs.shape
    depth, _, n6 = w_mod.shape
    tn = 512
    cb = jnp.broadcast_to(cvecs[:, :, None], (nv, d, LANES))
    out = pl.pallas_call(
        functools.partial(_mod_kernel, nv=nv, tn=tn),
        grid=(depth, n6 // tn),
        in_specs=[
            pl.BlockSpec((nv, d, LANES), lambda l, j: (0, 0, 0)),
            pl.BlockSpec((1, d, tn), lambda l, j: (l, 0, j)),
            pl.BlockSpec((1, 1, tn), lambda l, j: (l, 0, j)),
        ],
        out_specs=pl.BlockSpec((1, nv, tn), lambda l, j: (l, 0, j)),
        out_shape=jax.ShapeDtypeStruct((depth, nv, n6), F32),
        compiler_params=_cparams(("parallel", "parallel")),
        name="mod_vectors",
    )(cb, w_mod, b_mod.reshape(depth, 1, n6))
    return out.reshape(depth, nv, 6, d)


def _in_kernel(x_ref, g_ref, mod_ref, w_ref, *refs, with_cache):
    if with_cache:
        zb_ref, zf_ref, ck_ref, cv_ref = refs[-4:]
    else:
        zb_ref, zf_ref = refs[-2:]
    h = _norm_mod(x_ref[...], g_ref[...], mod_ref[0, 0:1, :], mod_ref[0, 1:2, :]).astype(BF16)
    for src in range(15):
        z = jnp.dot(h, w_ref[:, src * CH:(src + 1) * CH], preferred_element_type=F32)
        if src in ZB_SRC:
            j = ZB_SRC.index(src)
            zb_ref[:, j * CH:(j + 1) * CH] = z.astype(BF16)
        if src in ZF_SRC:
            j = ZF_SRC.index(src)
            zf_ref[:, j * CH:(j + 1) * CH] = z
        if with_cache and src in (SRC_NK, SRC_NV):
            c_ref = ck_ref if src == SRC_NK else cv_ref
            nseq, _, seq, _ = c_ref.shape
            for s in range(nseq):
                c_ref[s, 0] = z[s * seq:(s + 1) * seq]


def _in_proj(x, g, mod, w_bf, *, rows_per_seg, seq, layer, depth, caches):
    t, d = x.shape
    tm = ROW_TILE
    with_cache = caches is not None
    spt = rows_per_seg // tm
    in_specs = [
        pl.BlockSpec((tm, d), lambda i: (i, 0)),
        _const_spec((1, d)),
        pl.BlockSpec((1, 6, d), lambda i: (i // spt, 0, 0)),
        _const_spec((d, 15 * CH)),
    ]
    args = [x, g.reshape(1, d), mod, w_bf]
    out_specs = [
        pl.BlockSpec((tm, len(ZB_SRC) * CH), lambda i: (i, 0)),
        pl.BlockSpec((tm, len(ZF_SRC) * CH), lambda i: (i, 0)),
    ]
    out_shape = [
        jax.ShapeDtypeStruct((t, len(ZB_SRC) * CH), BF16),
        jax.ShapeDtypeStruct((t, len(ZF_SRC) * CH), F32),
    ]
    aliases = {}
    if with_cache:
        assert tm % seq == 0, "a row tile holds whole context sequences"
        cspec = pl.BlockSpec((tm // seq, 1, seq, CH), lambda i: (i, layer, 0, 0))
        out_specs += [cspec, cspec]
        cshape = jax.ShapeDtypeStruct((t // seq, depth, seq, CH), F32)
        out_shape += [cshape, cshape]
        if caches[0] is not None:
            in_specs += [pl.BlockSpec(memory_space=pl.ANY)] * 2
            args += list(caches)
            aliases = {4: 2, 5: 3}
    return pl.pallas_call(
        functools.partial(_in_kernel, with_cache=with_cache),
        grid=(t // tm,),
        in_specs=in_specs,
        out_specs=out_specs,
        out_shape=out_shape,
        input_output_aliases=aliases,
        compiler_params=_cparams(("parallel",)),
        name="in_proj",
    )(*args)


def _dft_tables(n, scale):
    k = np.arange(n)
    ang = 2.0 * np.pi * ((k[:, None] * k[None, :]) % n) / n
    return np.cos(ang) * scale, np.sin(ang) * scale


def _ft_ctx_kernel(u_ref, cs_ref, lm_ref, y_ref):
    u = u_ref[...]
    cs = cs_ref[...].astype(BF16)
    zc, zs = [], []
    for g in range(FT_GROUPS):
        z = jnp.dot(u[:, g * FT_GD:(g + 1) * FT_GD], cs, preferred_element_type=F32)
        zc.append(z[:, :FT_GD])
        zs.append(z[:, FT_GD:])
    zz = jnp.concatenate([jnp.concatenate(zc, axis=1), jnp.concatenate(zs, axis=1)], axis=0)
    y = jnp.dot(lm_ref[...].astype(BF16), zz.astype(BF16), preferred_element_type=F32)
    for g in range(FT_GROUPS):
        y_ref[g] = y[:, g * FT_GD:(g + 1) * FT_GD]


def _fourier_ctx(zb, nb, seq):
    cc, sc = _dft_tables(FT_GD, FT_GD ** -0.5)
    cl, sl = _dft_tables(seq, seq ** -0.5)
    cs = jnp.asarray(np.concatenate([cc, sc], axis=1), F32)
    lm = jnp.asarray(np.concatenate([cl, -sl], axis=1), F32)
    return pl.pallas_call(
        _ft_ctx_kernel,
        grid=(nb,),
        in_specs=[
            pl.BlockSpec((seq, CH), lambda b: (b, ZB_UFT)),
            _const_spec(cs.shape),
            _const_spec(lm.shape),
        ],
        out_specs=pl.BlockSpec((FT_GROUPS, seq, FT_GD), lambda b: (0, b, 0)),
        out_shape=jax.ShapeDtypeStruct((FT_GROUPS, nb * seq, FT_GD), F32),
        compiler_params=_cparams(("parallel",)),
        name="fourier_ctx",
    )(zb, cs, lm)


def _ft_lat_kernel(u_ref, cs_ref, w1_ref, twc_ref, tws_ref, w2_ref, y_ref, zr_ref, zi_ref,
                   ar_ref, ai_ref, *, n, lw):
    rows = n * n
    rc = min(rows, 512)
    ng = lw // FT_GD
    cs = cs_ref[...].astype(BF16)
    w1 = w1_ref[...].astype(BF16)
    w2 = w2_ref[...].astype(BF16)
    for r0 in range(0, rows, rc):
        for g in range(ng):
            z = jnp.dot(u_ref[r0:r0 + rc, g * FT_GD:(g + 1) * FT_GD], cs,
                        preferred_element_type=F32)
            zr_ref[g, r0:r0 + rc, :] = z[:, :FT_GD]
            zi_ref[g, r0:r0 + rc, :] = -z[:, FT_GD:]

    def gather(re_ref, im_ref, idx):
        xr = jnp.concatenate([re_ref[g, idx, :] for g in range(ng)], axis=1)
        xi = jnp.concatenate([im_ref[g, idx, :] for g in range(ng)], axis=1)
        return jnp.concatenate([xr, xi], axis=0).astype(BF16)

    def stage1(a, carry):
        idx = pl.ds(a, n, stride=n)
        y = jnp.dot(w1, gather(zr_ref, zi_ref, idx), preferred_element_type=F32)
        yr, yi = y[:n], y[n:]
        tc = jnp.concatenate([twc_ref[a]] * ng, axis=1)
        ts = jnp.concatenate([tws_ref[a]] * ng, axis=1)
        zr = yr * tc - yi * ts
        zi = yr * ts + yi * tc
        for g in range(ng):
            ar_ref[g, idx, :] = zr[:, g * FT_GD:(g + 1) * FT_GD]
            ai_ref[g, idx, :] = zi[:, g * FT_GD:(g + 1) * FT_GD]
        return carry

    lax.fori_loop(0, n, stage1, 0, unroll=8)

    def stage2(q, carry):
        r0 = pl.multiple_of(q * n, n)
        y = jnp.dot(w2, gather(ar_ref, ai_ref, pl.ds(r0, n)), preferred_element_type=F32)
        for g in range(ng):
            y_ref[g, pl.ds(q, n, stride=n), :] = y[:, g * FT_GD:(g + 1) * FT_GD]
        return carry

    lax.fori_loop(0, n, stage2, 0, unroll=8)


def _fourier_lat(zb, nb, seq):
    n = math.isqrt(seq)
    assert n * n == seq and n % SUBLANES == 0
    lw = 256
    cc, sc = _dft_tables(FT_GD, FT_GD ** -0.5)
    cs = jnp.asarray(np.concatenate([cc, sc], axis=1), F32)
    pr, ps = _dft_tables(n, 1.0 / n)
    w1 = jnp.asarray(np.block([[pr, ps], [-ps, pr]]), F32)
    k = np.arange(n)
    ang = 2.0 * np.pi * (k[:, None] * k[None, :]) / seq
    twc = jnp.asarray(np.broadcast_to(np.cos(ang)[:, :, None], (n, n, LANES)), F32)
    tws = jnp.asarray(np.broadcast_to(-np.sin(ang)[:, :, None], (n, n, LANES)), F32)
    wr, ws = _dft_tables(n, 1.0)
    w2 = jnp.asarray(np.concatenate([wr, ws], axis=1), F32)
    nh = CH // lw
    return pl.pallas_call(
        functools.partial(_ft_lat_kernel, n=n, lw=lw),
        grid=(nb, nh),
        in_specs=[
            pl.BlockSpec((seq, lw), lambda b, h: (b, ZB_UFT * nh + h)),
            _const_spec(cs.shape),
            _const_spec(w1.shape),
            _const_spec(twc.shape),
            _const_spec(tws.shape),
            _const_spec(w2.shape),
        ],
        out_specs=pl.BlockSpec((lw // FT_GD, seq, FT_GD), lambda b, h: (h, b, 0)),
        out_shape=jax.ShapeDtypeStruct((FT_GROUPS, nb * seq, FT_GD), F32),
        scratch_shapes=[pltpu.VMEM((lw // FT_GD, seq, FT_GD), F32)] * 4,
        compiler_params=_cparams(("parallel", "parallel")),
        name="fourier_lat",
    )(zb, cs, w1, twc, tws, w2)


HG_SAFE = 160.0
HG_CHUNK = 128
HG_BLOCK = 512


def _hgrn_tables(c, rev):
    t = np.arange(c)
    tri = (t[None, :] >= t[:, None]) if rev else (t[None, :] <= t[:, None])
    lv = np.full((c, c), -1, np.int32)
    n, i = 1, 0
    while n < c:
        same = (t[:, None] // (2 * n)) == (t[None, :] // (2 * n))
        hi_t, hi_s = (t[:, None] % (2 * n)) >= n, (t[None, :] % (2 * n)) >= n
        pair = same & (~hi_t & hi_s if rev else hi_t & ~hi_s)
        lv[pair] = i
        n, i = 2 * n, i + 1
    return tri.astype(np.float32), lv


def _split3(x):
    hi = x.astype(BF16)
    r = x - hi.astype(F32)
    mid = r.astype(BF16)
    lo = (r - mid.astype(F32)).astype(BF16)
    return hi, mid, lo


def _hgrn_kernel(hq_ref, hf_ref, hi_ref, lb_ref, tri_ref, lv_ref, *refs, c, nch, rev, has_s0):
    if has_s0:
        s0_ref, o_ref, st_ref, b_ref, k_ref, q_ref = refs
    else:
        o_ref, so_ref, st_ref, b_ref, k_ref, q_ref = refs
    j = pl.program_id(1)

    @pl.when(j == 0)
    def _():
        for h in range(HG_HEADS):
            st_ref[h] = s0_ref[0, h].T if has_s0 else jnp.zeros((HG_D, HG_D), F32)

    la = lb_ref[0:1, :]
    l1 = lb_ref[1:2, :]
    lv = lv_ref[...]
    row8 = lax.broadcasted_iota(jnp.int32, (c // SUBLANES, SUBLANES, HG_D), 1)
    ti = lax.broadcasted_iota(jnp.int32, (c, c), 0)
    si = lax.broadcasted_iota(jnp.int32, (c, c), 1)
    causal = (si >= ti) if rev else (si <= ti)
    half = c // 2
    same_half = jnp.logical_and(causal, (ti < half) == (si < half))
    cross_half = jnp.logical_and(causal, (ti < half) != (si < half))
    row = lax.broadcasted_iota(jnp.int32, (c, HG_D), 0)
    first_rows = (row >= half) if rev else (row < half)
    bl_row = c - 1 if not rev else 0
    mid_row = half if rev else half - 1

    tmax = jnp.float32(0.0)
    for cc in range(nch):
        rows = slice(cc * c, (cc + 1) * c)
        z = hf_ref[rows, :]
        ls = jnp.minimum(z, 0.0) - jnp.log(1.0 + jnp.exp(-jnp.abs(z)))
        cc_ = l1 + ls
        logf = jnp.maximum(la, cc_) + jnp.log(1.0 + jnp.exp(-jnp.abs(la - cc_)))
        k_ref[rows, :] = 1.0 - jnp.exp(logf)
        b = sum(jnp.dot(tri_ref[...], p, preferred_element_type=F32) for p in _split3(logf))
        b_ref[rows, :] = b
        q_ref[rows, :] = _silu(hq_ref[rows, :].astype(F32))
        bm = b[mid_row:mid_row + 1, :]
        bl = b[bl_row:bl_row + 1, :]
        tmax = jnp.maximum(tmax, jnp.maximum(jnp.max(jnp.abs(bm)), jnp.max(jnp.abs(bl - bm))))
    safe = tmax <= HG_SAFE

    def chunk(cc, fast):
        rows = slice(cc * c, (cc + 1) * c)
        b = b_ref[rows, :]
        kk = k_ref[rows, :]
        q = q_ref[rows, :]
        v = hi_ref[rows, :]
        blast = b_ref[cc * c + bl_row:cc * c + bl_row + 1, :]
        bmid = b_ref[cc * c + mid_row:cc * c + mid_row + 1, :]
        brefs = {}
        n = SUBLANES
        while not fast and n < c:
            parts = []
            for m in range(c // (2 * n)):
                r = cc * c + 2 * n * m + (n if rev else n - 1)
                parts.append(jnp.broadcast_to(b_ref[r:r + 1, :], (2 * n, HG_HEADS * HG_D)))
            brefs[n] = jnp.concatenate(parts, axis=0) if len(parts) > 1 else parts[0]
            n *= 2
        outs = []
        for h in range(HG_HEADS):
            hs = slice(h * HG_D, (h + 1) * HG_D)
            qh, kh, bh, vh, blh = q[:, hs], kk[:, hs], b[:, hs], v[:, hs], blast[:, hs]
            st = st_ref[h]
            if fast:
                bmh = bmid[:, hs]
                r1 = 0.5 * bmh
                r2 = 0.5 * (bmh + blh)
                ref = jnp.where(first_rows, r1, r2)
                qn = qh * jnp.exp(bh - ref)
                kn = kh * jnp.exp(ref - bh)
                sn = lax.dot_general(qn.astype(BF16), kn.astype(BF16), (((1,), (1,)), ((), ())),
                                     preferred_element_type=F32)
                e = jnp.exp(-jnp.abs(bh - bmh))
                sx = lax.dot_general((qh * e).astype(BF16), (kh * e).astype(BF16),
                                     (((1,), (1,)), ((), ())), preferred_element_type=F32)
                a = jnp.where(same_half, sn, jnp.where(cross_half, sx, 0.0))
                qe = (qn * jnp.where(first_rows, jnp.exp(r1), jnp.exp(r2))).astype(BF16)
                ke = (kn * jnp.where(first_rows, jnp.exp(blh - r1), jnp.exp(blh - r2))).astype(BF16)
                o = lax.dot_general(qe, st.astype(BF16), (((1,), (1,)), ((), ())),
                                    preferred_element_type=F32)
                o = o + jnp.dot(a.astype(BF16), vh, preferred_element_type=F32)
            else:
                o, ke = _hgrn_levels(qh, kh, bh, vh, blh, st, brefs, hs, lv, row8, c, rev)
            outs.append(o)
            kv = lax.dot_general(vh, ke, (((0,), (0,)), ((), ())), preferred_element_type=F32)
            st_ref[h] = st * jnp.exp(blh) + kv
        o_ref[rows, :] = jnp.concatenate(outs, axis=1)

    order = [nch - 1 - ci if rev else ci for ci in range(nch)]

    @pl.when(safe)
    def _():
        for cc in order:
            chunk(cc, True)

    @pl.when(jnp.logical_not(safe))
    def _():
        for cc in order:
            chunk(cc, False)

    if not has_s0:
        @pl.when(j == pl.num_programs(1) - 1)
        def _():
            for h in range(HG_HEADS):
                so_ref[0, 0, 0, h] = st_ref[h].T


def _hgrn_levels(qh, kh, bh, vh, blh, st, brefs, hs, lv, row8, c, rev):
    qe = (qh * jnp.exp(bh)).astype(BF16)
    o = lax.dot_general(qe, st.astype(BF16), (((1,), (1,)), ((), ())), preferred_element_type=F32)
    b3 = bh.reshape(c // SUBLANES, SUBLANES, HG_D)
    a = jnp.zeros((c, c), F32)
    n, i = 1, 0
    while n < c:
        if n < SUBLANES:
            bref = None
            for g in range(SUBLANES // (2 * n)):
                r = 2 * n * g + (n if rev else n - 1)
                bc = jnp.broadcast_to(b3[:, r:r + 1, :], b3.shape)
                bref = bc if bref is None else jnp.where(row8 >= 2 * n * g, bc, bref)
            bref = bref.reshape(c, HG_D)
        else:
            bref = brefs[n][:, hs]
        e = jnp.exp(-jnp.abs(bh - bref))
        sn = lax.dot_general((qh * e).astype(BF16), (kh * e).astype(BF16),
                             (((1,), (1,)), ((), ())), preferred_element_type=F32)
        a = jnp.where(lv == i, sn, a)
        n, i = 2 * n, i + 1
    o = o + jnp.dot(a.astype(BF16), vh, preferred_element_type=F32)
    o = o + jnp.sum(qh * kh, axis=-1, keepdims=True) * vh.astype(F32)
    return o, (kh * jnp.exp(blh - bh)).astype(BF16)


def _hgrn(zb, zf, lbp, s0, state_out, *, nb, seq, rev, layer, depth, c, lb_rows):
    t = zb.shape[0]
    nblk = seq // lb_rows
    nch = lb_rows // c
    tri, lv = _hgrn_tables(c, rev)
    d = int(rev)

    def rowmap(b, j):
        return b * nblk + (nblk - 1 - j if rev else j)

    in_specs = [
        pl.BlockSpec((lb_rows, CH), lambda b, j: (rowmap(b, j), ZB_HQ)),
        pl.BlockSpec((lb_rows, CH), lambda b, j: (rowmap(b, j), d)),
        pl.BlockSpec((lb_rows, CH), lambda b, j: (rowmap(b, j), ZB_HI)),
        _const_spec((2, CH)),
        _const_spec((c, c)),
        _const_spec((c, c)),
    ]
    args = [zb, zf, zb, lbp, jnp.asarray(tri, BF16), jnp.asarray(lv)]
    has_s0 = s0 is not None
    if has_s0:
        in_specs.append(pl.BlockSpec((1, HG_HEADS, HG_D, HG_D), lambda b, j: (b, 0, 0, 0)))
        args.append(s0)
    sshape = (nb, depth, 2, HG_HEADS, HG_D, HG_D)
    aliases = {}
    if state_out is not None:
        in_specs.append(pl.BlockSpec(memory_space=pl.ANY))
        args.append(state_out)
        aliases = {len(args) - 1: 1}
    kern = functools.partial(_hgrn_kernel, c=c, nch=nch, rev=rev, has_s0=has_s0)
    if state_out is not None:
        kern = functools.partial(_drop_ref, kern, len(args) - 1)
    out_specs = [pl.BlockSpec((lb_rows, CH), lambda b, j: (rowmap(b, j), 0))]
    out_shape = [jax.ShapeDtypeStruct((t, CH), F32)]
    if not has_s0:
        out_specs.append(pl.BlockSpec((1, 1, 1, HG_HEADS, HG_D, HG_D), lambda b, j: (b, layer, d, 0, 0, 0)))
        out_shape.append(jax.ShapeDtypeStruct(sshape, F32))
    outs = pl.pallas_call(
        kern,
        grid=(nb, nblk),
        in_specs=in_specs,
        out_specs=out_specs,
        out_shape=out_shape,
        scratch_shapes=[pltpu.VMEM((HG_HEADS, HG_D, HG_D), F32)] + [pltpu.VMEM((lb_rows, CH), F32)] * 3,
        input_output_aliases=aliases,
        compiler_params=_cparams(("parallel", "arbitrary")),
        name="hgrn_bwd" if rev else "hgrn_fwd",
    )(*args)
    return (outs[0], None) if has_s0 else tuple(outs)


def _drop_ref(kern, pos, *refs):
    return kern(*refs[:pos], *refs[pos + 1:])


def _head_mask(a):
    lane = lax.broadcasted_iota(jnp.int32, (1, 2 * NA_HD), 1)
    return (lane // NA_HD) == a


def _qk(q, k):
    return lax.dot_general(q, k, (((1,), (1,)), ((), ())), preferred_element_type=F32)


def _attn_ctx_kernel(q_ref, k_ref, v_ref, o_ref):
    w = 2 * NA_HD
    for pr in range(NA_HEADS // 2):
        ps = slice(pr * w, (pr + 1) * w)
        q = q_ref[:, ps] * (NA_HD ** -0.5)
        k = k_ref[:, ps]
        v = v_ref[:, ps]
        outs = []
        for a in range(2):
            s = _qk(jnp.where(_head_mask(a), q, jnp.zeros_like(q)), k)
            p = jnp.exp(s - jnp.max(s, axis=-1, keepdims=True))
            l = jnp.sum(p, axis=-1, keepdims=True)
            outs.append(jnp.dot(p.astype(BF16), v, preferred_element_type=F32) / l)
        o_ref[:, ps] = jnp.where(_head_mask(0), outs[0], outs[1]).astype(BF16)


def _attn_ctx(zb, nb, seq):
    return pl.pallas_call(
        _attn_ctx_kernel,
        grid=(nb,),
        in_specs=[
            pl.BlockSpec((seq, CH), lambda b: (b, ZB_NQ)),
            pl.BlockSpec((seq, CH), lambda b: (b, ZB_NK)),
            pl.BlockSpec((seq, CH), lambda b: (b, ZB_NV)),
        ],
        out_specs=pl.BlockSpec((seq, CH), lambda b: (b, 0)),
        out_shape=jax.ShapeDtypeStruct((nb * seq, CH), BF16),
        compiler_params=_cparams(("parallel",)),
        name="attn_ctx",
    )(zb, zb, zb)


NA_TR = 8


def _na_geometry(rows):
    kr = min(WIN_R, rows)
    span = min(NA_TR + WIN_R, rows)
    ntile = rows // NA_TR
    pats, var, starts = [], [], []
    for t in range(ntile):
        r0 = t * NA_TR
        start = int(np.clip(r0 - kr // 2, 0, rows - span))
        qr = r0 + np.arange(NA_TR)
        row_start = np.clip(qr - kr // 2, 0, rows - kr)
        krow = start + np.arange(span)
        rv = (krow[None, :] >= row_start[:, None]) & (krow[None, :] < row_start[:, None] + kr)
        d_row = np.where(rv, krow[None, :] - qr[:, None] + WIN_R - 1, -1)
        for vi, p in enumerate(pats):
            if np.array_equal(p, d_row):
                var.append(vi)
                break
        else:
            var.append(len(pats))
            pats.append(d_row)
        starts.append(start)
    return pats, np.asarray(var, np.int32), np.asarray(starts, np.int32), span


def _na_col_tables(rpb):
    qc = np.arange(GRID_W)
    col_start = np.clip(qc - WIN_C // 2, 0, GRID_W - WIN_C)
    cv = (qc[None, :] >= col_start[:, None]) & (qc[None, :] < col_start[:, None] + WIN_C)
    d_col = qc[None, :] - qc[:, None] + WIN_C - 1
    onehot = (d_col[None] == np.arange(2 * WIN_C - 1)[:, None, None]) & cv[None]
    t = jnp.einsum("hrd,dqk->hrqk", rpb, jnp.asarray(onehot, F32), precision=lax.Precision.HIGHEST)
    t = jnp.where(jnp.asarray(cv)[None, None], t, NEG)
    return jnp.concatenate([t, t], axis=-1)


def _na_kernel(var_ref, start_ref, q_ref, k_ref, v_ref, kc_ref, vc_ref, tab_ref, o_ref, bias_ref,
               *, nkey, pats):
    t = pl.program_id(2)
    var = var_ref[t]
    changed = jnp.logical_or(t == 0, var != var_ref[jnp.maximum(t - 1, 0)])
    left = lax.broadcasted_iota(jnp.int32, (GRID_W, 2 * GRID_W), 1) < GRID_W
    neg = jnp.full((GRID_W, 2 * GRID_W), NEG, F32)

    for vi, d_row in enumerate(pats):
        @pl.when(jnp.logical_and(changed, var == vi))
        def _(d_row=d_row):
            for a in range(2):
                for i in range(d_row.shape[0]):
                    for j in range(0, d_row.shape[1], 2):
                        r0, r1 = int(d_row[i, j]), int(d_row[i, j + 1])
                        b0 = tab_ref[a, r0] if r0 >= 0 else neg
                        b1 = tab_ref[a, r1] if r1 >= 0 else neg
                        blk = neg if (r0 < 0 and r1 < 0) else jnp.where(left, b0, b1)
                        bias_ref[a, i * GRID_W:(i + 1) * GRID_W, j * GRID_W:(j + 2) * GRID_W] = blk

    st = pl.multiple_of(start_ref[t] * GRID_W, GRID_W)
    q = q_ref[...] * (NA_HD ** -0.5)
    kw = k_ref[pl.ds(st, nkey), :]
    vw = v_ref[pl.ds(st, nkey), :]
    kc = kc_ref[0]
    vc = vc_ref[0]
    outs = []
    for a in range(2):
        qa = jnp.where(_head_mask(a), q, jnp.zeros_like(q))
        s1 = _qk(qa, kw) + bias_ref[a]
        s2 = _qk(qa, kc)
        m = jnp.maximum(jnp.max(s1, axis=-1, keepdims=True), jnp.max(s2, axis=-1, keepdims=True))
        p1 = jnp.exp(s1 - m)
        p2 = jnp.exp(s2 - m)
        l = jnp.sum(p1, axis=-1, keepdims=True) + jnp.sum(p2, axis=-1, keepdims=True)
        o = (jnp.dot(p1.astype(BF16), vw, preferred_element_type=F32)
             + jnp.dot(p2.astype(BF16), vc, preferred_element_type=F32))
        outs.append(o / l)
    o_ref[...] = jnp.where(_head_mask(0), outs[0], outs[1]).astype(BF16)


def _na_latent(zb, kctx, vctx, rpb, nb, seq):
    rows = seq // GRID_W
    pats, var, starts, span = _na_geometry(rows)
    nkey = span * GRID_W
    tq = NA_TR * GRID_W
    ntile = rows // NA_TR
    hp = NA_HEADS // 2
    w = 2 * NA_HD
    past = kctx.shape[1]
    assert span % 2 == 0
    tabs = _na_col_tables(rpb)
    nrow = tabs.shape[1]
    grid_spec = pltpu.PrefetchScalarGridSpec(
        num_scalar_prefetch=2,
        grid=(nb, hp, ntile),
        in_specs=[
            pl.BlockSpec((tq, w), lambda b, p, t, *_: (b * ntile + t, ZB_NQ * hp + p)),
            pl.BlockSpec((seq, w), lambda b, p, t, *_: (b, ZB_NK * hp + p)),
            pl.BlockSpec((seq, w), lambda b, p, t, *_: (b, ZB_NV * hp + p)),
            pl.BlockSpec((1, past, w), lambda b, p, t, *_: (b, 0, p)),
            pl.BlockSpec((1, past, w), lambda b, p, t, *_: (b, 0, p)),
            pl.BlockSpec((2, nrow, GRID_W, 2 * GRID_W), lambda b, p, t, *_: (p, 0, 0, 0)),
        ],
        out_specs=pl.BlockSpec((tq, w), lambda b, p, t, *_: (b * ntile + t, p)),
        scratch_shapes=[pltpu.VMEM((2, tq, nkey), F32)],
    )
    return pl.pallas_call(
        functools.partial(_na_kernel, nkey=nkey, pats=pats),
        grid_spec=grid_spec,
        out_shape=jax.ShapeDtypeStruct((nb * seq, CH), BF16),
        compiler_params=_cparams(("parallel", "parallel", "arbitrary")),
        name="na_latent",
    )(jnp.asarray(var), jnp.asarray(starts), zb, zb, zb, kctx, vctx, tabs)


def _merge_kernel(x_ref, mod_ref, gate_ref, yft_ref, of_ref, ob_ref, hg_ref, on_ref, hn_ref,
                  wfo_ref, who_ref, wno_ref, wout_ref, o_ref):
    d = x_ref.shape[1]
    yft = jnp.concatenate([yft_ref[g] for g in range(FT_GROUPS)], axis=1)
    y_ft = jnp.dot(yft.astype(BF16), wfo_ref[...], preferred_element_type=F32)
    o = of_ref[...] + ob_ref[...]
    parts = []
    for h in range(HG_HEADS):
        oh = o[:, h * HG_D:(h + 1) * HG_D]
        ms = jnp.mean(oh * oh, axis=-1, keepdims=True)
        parts.append(oh * lax.rsqrt(ms + EPS))
    oh = jnp.concatenate(parts, axis=1) * hn_ref[...]
    oh = (oh * _silu(hg_ref[...].astype(F32))).astype(BF16)
    y_hg = jnp.dot(oh, who_ref[...], preferred_element_type=F32)
    y_na = jnp.dot(on_ref[...], wno_ref[...], preferred_element_type=F32)
    g = gate_ref[...].astype(F32)
    merged = (_sigmoid(g[:, :d]) * y_ft + _sigmoid(g[:, d:2 * d]) * y_hg
              + _sigmoid(g[:, 2 * d:]) * y_na)
    y = jnp.dot(merged.astype(BF16), wout_ref[...], preferred_element_type=F32)
    o_ref[...] = x_ref[...] + mod_ref[0, 2:3, :] * y


def _merge(x, mod, zb, yft, o_f, o_b, o_n, hg_norm, w_fo, w_ho, w_no, w_out, *, rows_per_seg):
    t, d = x.shape
    tm = ROW_TILE
    spt = rows_per_seg // tm
    assert 3 * d == 6 * CH, "the three merge gates fill the first six bf16 splits"
    row = lambda i: (i, 0)
    return pl.pallas_call(
        _merge_kernel,
        grid=(t // tm,),
        in_specs=[
            pl.BlockSpec((tm, d), row),
            pl.BlockSpec((1, 6, d), lambda i: (i // spt, 0, 0)),
            pl.BlockSpec((tm, 3 * d), row),
            pl.BlockSpec((FT_GROUPS, tm, FT_GD), lambda i: (0, i, 0)),
            pl.BlockSpec((tm, CH), row),
            pl.BlockSpec((tm, CH), row),
            pl.BlockSpec((tm, CH), lambda i: (i, ZB_HG)),
            pl.BlockSpec((tm, CH), row),
            _const_spec((1, CH)),
            _const_spec((CH, d)),
            _const_spec((CH, d)),
            _const_spec((CH, d)),
            _const_spec((d, d)),
        ],
        out_specs=pl.BlockSpec((tm, d), row),
        out_shape=jax.ShapeDtypeStruct((t, d), F32),
        compiler_params=_cparams(("parallel",)),
        name="merge",
    )(x, mod, zb, yft, o_f, o_b, zb, o_n, hg_norm.reshape(1, CH), w_fo, w_ho, w_no, w_out)


FF_CHUNK = 256


def _swiglu_acc(h, w1_ref, w3_ref, w2_ref, lead=()):
    dff = w1_ref.shape[-1]
    acc = None
    for c0 in range(0, dff, FF_CHUNK):
        cs = slice(c0, c0 + FF_CHUNK)
        g = jnp.dot(h, w1_ref[lead + (slice(None), cs)], preferred_element_type=F32)
        u = jnp.dot(h, w3_ref[lead + (slice(None), cs)], preferred_element_type=F32)
        a = (_silu(g) * u).astype(BF16)
        y = jnp.dot(a, w2_ref[lead + (cs, slice(None))], preferred_element_type=F32)
        acc = y if acc is None else acc + y
    return acc


def _ffn_kernel(x_ref, g_ref, mod_ref, w1_ref, w3_ref, w2_ref, *refs, final):
    o_ref = refs[-1]
    x = x_ref[...]
    h = _norm_mod(x, g_ref[...], mod_ref[0, 3:4, :], mod_ref[0, 4:5, :]).astype(BF16)
    y = x + mod_ref[0, 5:6, :] * _swiglu_acc(h, w1_ref, w3_ref, w2_ref)
    if final:
        ms = jnp.mean(y * y, axis=-1, keepdims=True)
        y = y * lax.rsqrt(ms + EPS) * refs[0][...]
    o_ref[...] = y


def _ffn_dense(x, g, mod, w1, w3, w2, norm_final, *, rows_per_seg):
    t, d = x.shape
    dff = w1.shape[1]
    tm = ROW_TILE
    spt = rows_per_seg // tm
    final = norm_final is not None
    in_specs = [
        pl.BlockSpec((tm, d), lambda i: (i, 0)),
        _const_spec((1, d)),
        pl.BlockSpec((1, 6, d), lambda i: (i // spt, 0, 0)),
        _const_spec((d, dff)),
        _const_spec((d, dff)),
        _const_spec((dff, d)),
    ]
    args = [x, g.reshape(1, d), mod, w1, w3, w2]
    if final:
        in_specs.append(_const_spec((1, d)))
        args.append(norm_final.reshape(1, d))
    return pl.pallas_call(
        functools.partial(_ffn_kernel, final=final),
        grid=(t // tm,),
        in_specs=in_specs,
        out_specs=pl.BlockSpec((tm, d), lambda i: (i, 0)),
        out_shape=jax.ShapeDtypeStruct((t, d), F32),
        compiler_params=_cparams(("parallel",)),
        name="ffn_dense",
    )(*args)


def _store_token_tiles(ref, lead, val):
    n, d = val.shape
    assert d == SUBLANES * LANES
    for j in range(SUBLANES):
        ref[lead + (pl.ds(j, n, stride=SUBLANES), slice(None))] = val[:, j * LANES:(j + 1) * LANES]


def _load_token_tiles(ref, lead, n):
    return jnp.concatenate([ref[lead + (pl.ds(j, n, stride=SUBLANES), slice(None))]
                            for j in range(SUBLANES)], axis=1)


def _router_kernel(x_ref, g_ref, mod_ref, wr_ref, h_ref, r_ref):
    h = _norm_mod(x_ref[...], g_ref[...], mod_ref[0, 3:4, :], mod_ref[0, 4:5, :])
    _store_token_tiles(h_ref, (), h)
    logits = jnp.dot(h, wr_ref[...], precision=lax.Precision.HIGHEST,
                     preferred_element_type=F32)
    lane = lax.broadcasted_iota(jnp.int32, logits.shape, 1)
    logits = jnp.where(lane < N_EXPERTS, logits, -jnp.inf)
    m1 = jnp.max(logits, axis=-1, keepdims=True)
    i1 = jnp.min(jnp.where(logits == m1, lane, LANES), axis=-1, keepdims=True)
    rest = jnp.where(lane == i1, -jnp.inf, logits)
    m2 = jnp.max(rest, axis=-1, keepdims=True)
    i2 = jnp.min(jnp.where(rest == m2, lane, LANES), axis=-1, keepdims=True)
    e2 = jnp.exp(m2 - m1)
    w1 = 1.0 / (1.0 + e2)
    w2 = e2 / (1.0 + e2)
    r = jnp.where(lane == 0, i1.astype(F32), jnp.where(lane == 1, i2.astype(F32),
                  jnp.where(lane == 2, w1, jnp.where(lane == 3, w2, 0.0))))
    r_ref[...] = r


def _router(x, g, mod, w_router, *, rows_per_seg, h_all, row_off, total):
    t, d = x.shape
    tm = ROW_TILE
    spt = rows_per_seg // tm
    off = row_off // tm
    wr = jnp.zeros((d, LANES), F32).at[:, :N_EXPERTS].set(w_router)
    in_specs = [
        pl.BlockSpec((tm, d), lambda i: (i, 0)),
        _const_spec((1, d)),
        pl.BlockSpec((1, 6, d), lambda i: (i // spt, 0, 0)),
        _const_spec((d, LANES)),
    ]
    args = [x, g.reshape(1, d), mod, wr]
    kern, aliases = _router_kernel, {}
    if h_all is not None:
        in_specs.append(pl.BlockSpec(memory_space=pl.ANY))
        args.append(h_all)
        kern, aliases = functools.partial(_drop_ref, _router_kernel, 4), {4: 0}
    return pl.pallas_call(
        kern,
        grid=(t // tm,),
        in_specs=in_specs,
        out_specs=[pl.BlockSpec((tm * SUBLANES, LANES), lambda i: (i + off, 0)),
                   pl.BlockSpec((tm, LANES), lambda i: (i, 0))],
        out_shape=[jax.ShapeDtypeStruct((total * SUBLANES, LANES), F32),
                   jax.ShapeDtypeStruct((t, LANES), F32)],
        input_output_aliases=aliases,
        compiler_params=_cparams(("parallel",)),
        name="router",
    )(*args)


MOE_TM = 512


def _moe_ffn_kernel(te_ref, nt_ref, src_ref, h_hbm, w1_ref, w3_ref, w2_ref, y_ref, hbuf, sem):
    i = pl.program_id(0)
    nt = nt_ref[0]
    slot = i % 2

    def gather(tile, s):
        base = tile * MOE_TM

        def issue(r, carry):
            row = pl.multiple_of(src_ref[base + r], SUBLANES)
            pltpu.make_async_copy(h_hbm.at[pl.ds(row, SUBLANES)],
                                  hbuf.at[s, pl.ds(pl.multiple_of(r * SUBLANES, SUBLANES), SUBLANES)],
                                  sem.at[s]).start()
            return carry

        lax.fori_loop(0, MOE_TM, issue, 0, unroll=8)

    @pl.when(jnp.logical_and(i == 0, nt > 0))
    def _():
        gather(0, 0)

    @pl.when(i < nt)
    def _():
        @pl.when(i + 1 < nt)
        def _():
            gather(i + 1, 1 - slot)

        pltpu.make_async_copy(h_hbm.at[pl.ds(0, MOE_TM * SUBLANES)], hbuf.at[slot], sem.at[slot]).wait()
        h = _load_token_tiles(hbuf, (slot,), MOE_TM).astype(BF16)
        _store_token_tiles(y_ref, (), _swiglu_acc(h, w1_ref, w3_ref, w2_ref, lead=(0,)))

    @pl.when(i >= nt)
    def _():
        y_ref[...] = jnp.zeros_like(y_ref)


def _moe_ffn(h, tile_expert, n_tiles, src, w1, w3, w2, ntile_max):
    d = SUBLANES * LANES
    dff = w1.shape[-1]
    grid_spec = pltpu.PrefetchScalarGridSpec(
        num_scalar_prefetch=3,
        grid=(ntile_max,),
        in_specs=[
            pl.BlockSpec(memory_space=pl.ANY),
            pl.BlockSpec((1, d, dff), lambda i, te, nt, src: (te[i], 0, 0)),
            pl.BlockSpec((1, d, dff), lambda i, te, nt, src: (te[i], 0, 0)),
            pl.BlockSpec((1, dff, d), lambda i, te, nt, src: (te[i], 0, 0)),
        ],
        out_specs=pl.BlockSpec((MOE_TM * SUBLANES, LANES), lambda i, te, nt, src: (i, 0)),
        scratch_shapes=[pltpu.VMEM((2, MOE_TM * SUBLANES, LANES), F32), pltpu.SemaphoreType.DMA((2,))],
    )
    return pl.pallas_call(
        _moe_ffn_kernel,
        grid_spec=grid_spec,
        out_shape=jax.ShapeDtypeStruct((ntile_max * MOE_TM * SUBLANES, LANES), F32),
        compiler_params=_cparams(("arbitrary",)),
        name="moe_ffn",
    )(tile_expert, n_tiles, src, h, w1, w3, w2)


def _combine_kernel(pos_ref, x_ref, mod_ref, r_ref, y_hbm, *refs, final, tm):
    o_ref, ybuf, sem = refs[-3:]
    i = pl.program_id(0)
    slot = i % 2

    def gather(tile, s):
        base = 2 * tile * tm

        def issue(r, carry):
            dst = pl.ds(pl.multiple_of(r * SUBLANES, SUBLANES), SUBLANES)
            for k in range(2):
                row = pl.multiple_of(pos_ref[base + 2 * r + k], SUBLANES)
                pltpu.make_async_copy(y_hbm.at[pl.ds(row, SUBLANES)], ybuf.at[s, k, dst],
                                      sem.at[s]).start()
            return carry

        lax.fori_loop(0, tm, issue, 0, unroll=8)

    @pl.when(i == 0)
    def _():
        gather(0, 0)

    @pl.when(i + 1 < pl.num_programs(0))
    def _():
        gather(i + 1, 1 - slot)

    for k in range(2):
        pltpu.make_async_copy(y_hbm.at[pl.ds(0, tm * SUBLANES)], ybuf.at[slot, k], sem.at[slot]).wait()
    r = r_ref[...]
    f = (r[:, 2:3] * _load_token_tiles(ybuf, (slot, 0), tm)
         + r[:, 3:4] * _load_token_tiles(ybuf, (slot, 1), tm))
    y = x_ref[...] + mod_ref[0, 5:6, :] * f
    if final:
        ms = jnp.mean(y * y, axis=-1, keepdims=True)
        y = y * lax.rsqrt(ms + EPS) * refs[0][...]
    o_ref[...] = y


def _moe_combine(x, mod, route, pos, y_sorted, norm_final, *, rows_per_seg):
    t, d = x.shape
    tm = MOE_TM // 2
    spt = rows_per_seg // tm
    final = norm_final is not None
    in_specs = [
        pl.BlockSpec((tm, d), lambda i, pos: (i, 0)),
        pl.BlockSpec((1, 6, d), lambda i, pos: (i // spt, 0, 0)),
        pl.BlockSpec((tm, LANES), lambda i, pos: (i, 0)),
        pl.BlockSpec(memory_space=pl.ANY),
    ]
    args = [x, mod, route, y_sorted]
    if final:
        in_specs.append(pl.BlockSpec((1, d), lambda i, pos: (0, 0)))
        args.append(norm_final.reshape(1, d))
    grid_spec = pltpu.PrefetchScalarGridSpec(
        num_scalar_prefetch=1,
        grid=(t // tm,),
        in_specs=in_specs,
        out_specs=pl.BlockSpec((tm, d), lambda i, pos: (i, 0)),
        scratch_shapes=[pltpu.VMEM((2, 2, tm * SUBLANES, LANES), F32), pltpu.SemaphoreType.DMA((2,))],
    )
    return pl.pallas_call(
        functools.partial(_combine_kernel, final=final, tm=tm),
        grid_spec=grid_spec,
        out_shape=jax.ShapeDtypeStruct((t, d), F32),
        compiler_params=_cparams(("arbitrary",)),
        name="moe_combine",
    )(pos.reshape(-1), *args)


def _moe(sets, g, w_router, w1, w3, w2, norm_final):
    t = sum(x.shape[0] for x, _, _ in sets)
    h, routes, off = None, [], 0
    for x, mod, seg in sets:
        h, r = _router(x, g, mod, w_router, rows_per_seg=seg, h_all=h, row_off=off, total=t)
        routes.append(r)
        off += x.shape[0]
    route = jnp.concatenate(routes, axis=0)
    ids = route[:, :2].astype(jnp.int32)
    onehot = (ids[:, :, None] == jnp.arange(N_EXPERTS)[None, None, :]).astype(jnp.int32)
    sel = onehot.sum(axis=1)
    counts = sel.sum(axis=0)
    rank = jnp.cumsum(sel, axis=0) - sel
    padded = ((counts + MOE_TM - 1) // MOE_TM) * MOE_TM
    pstart = jnp.cumsum(padded) - padded
    cstart = jnp.cumsum(counts) - counts
    pos_e = pstart[None, :] + rank
    pos = jnp.take_along_axis(pos_e, ids, axis=1).astype(jnp.int32)
    ntile_max = (2 * t) // MOE_TM + N_EXPERTS
    tok = jnp.arange(t, dtype=jnp.int32)
    key = jnp.sort((ids * t + tok[:, None]).reshape(-1))
    sorted_tok = key % t
    tile_start = jnp.arange(ntile_max, dtype=jnp.int32) * MOE_TM
    pend = pstart + padded
    tile_expert = jnp.minimum(jnp.sum(tile_start[:, None] >= pend[None, :], axis=1), N_EXPERTS - 1)
    n_tiles = (jnp.sum(padded) // MOE_TM).astype(jnp.int32).reshape(1)
    p = jnp.arange(ntile_max * MOE_TM, dtype=jnp.int32)
    e_p = jnp.repeat(tile_expert, MOE_TM)
    r_p = jnp.minimum(p - pstart[e_p], jnp.maximum(counts[e_p] - 1, 0))
    src = sorted_tok[jnp.clip(cstart[e_p] + r_p, 0, 2 * t - 1)].astype(jnp.int32) * SUBLANES
    pos = pos * SUBLANES
    y_sorted = _moe_ffn(h, tile_expert.astype(jnp.int32), n_tiles, src, w1, w3, w2, ntile_max)
    outs, off = [], 0
    for (x, mod, seg), r in zip(sets, routes):
        n = x.shape[0]
        outs.append(_moe_combine(x, mod, r, pos[off:off + n], y_sorted, norm_final, rows_per_seg=seg))
        off += n
    return outs


def _mixer(x, mod, l, depth, p, *, nb, seq, seg, ctx, caches, state_out):
    is_ctx = ctx is None
    outs = _in_proj(x, p["norm_mix"][l], mod, p["w_in"][l], rows_per_seg=seg, seq=seq, layer=l,
                    depth=depth, caches=caches if is_ctx else None)
    zb, zf = outs[0], outs[1]
    new_caches = tuple(outs[2:]) if is_ctx else None
    if is_ctx:
        yft = _fourier_ctx(zb, nb, seq)
        o_n = _attn_ctx(zb, nb, seq)
    else:
        yft = _fourier_lat(zb, nb, seq)
        o_n = _na_latent(zb, ctx[0], ctx[1], p["na_rpb"][l], nb, seq)
    hg_c = HG_CHUNK
    lbr = min(HG_BLOCK, seq)
    o_f, state_out = _hgrn(zb, zf, p["lbp"][l][0], None if is_ctx else ctx[2],
                           state_out if is_ctx else None, nb=nb, seq=seq, rev=False, layer=l,
                           depth=depth, c=hg_c, lb_rows=lbr)
    o_b, state_out = _hgrn(zb, zf, p["lbp"][l][1], None if is_ctx else ctx[3],
                           state_out if is_ctx else None, nb=nb, seq=seq, rev=True, layer=l,
                           depth=depth, c=hg_c, lb_rows=lbr)
    x = _merge(x, mod, zb, yft, o_f, o_b, o_n, p["hg_norm"][l], p["w_fo"][l], p["w_ho"][l],
               p["w_no"][l], p["w_out"][l], rows_per_seg=seg)
    return x, new_caches, state_out


def kernel(x_prompt, x_sample, cache_k, cache_v, state_hgrn, c, c_ctx, w_mod, b_mod, norm_mix,
           norm_ffn, w_in, hg_lb, hg_norm, na_rpb, w_fo, w_ho, w_no, w_out, w1_d, w3_d, w2_d,
           w_router, w1_e, w3_e, w2_e, norm_final):
    nb_c, seq_c, d = x_prompt.shape
    nb_l, seq_l, _ = x_sample.shape
    depth = w_in.shape[0]
    past = cache_k.shape[2]

    lbs = jnp.cumsum(jax.nn.softmax(hg_lb.astype(F32), axis=0), axis=0)
    lbs = lbs - lbs[0:1]
    lbp = jnp.stack([jnp.log(lbs), jnp.log1p(-lbs)], axis=2)

    bf = lambda w: w.astype(BF16)
    p = dict(norm_mix=norm_mix, norm_ffn=norm_ffn, w_in=bf(w_in), hg_norm=hg_norm, na_rpb=na_rpb,
             w_fo=bf(w_fo), w_ho=bf(w_ho), w_no=bf(w_no), w_out=bf(w_out), w1_d=bf(w1_d),
             w3_d=bf(w3_d), w2_d=bf(w2_d), w_router=w_router, w1_e=bf(w1_e), w3_e=bf(w3_e),
             w2_e=bf(w2_e), lbp=lbp)

    mods = _mod_vectors(jnp.concatenate([c_ctx[None], c], axis=0), w_mod, b_mod)

    x = x_prompt.reshape(nb_c * seq_c, d)
    xs = x_sample.reshape(nb_l * seq_l, d)
    seg_c, seg_l = nb_c * seq_c, seq_l
    caches, state_out = (None, None), None
    for l in range(depth):
        mod_c, mod_l = mods[l, 0:1], mods[l, 1:]
        x, caches, state_out = _mixer(x, mod_c, l, depth, p, nb=nb_c, seq=seq_c, seg=seg_c, ctx=None,
                                      caches=caches, state_out=state_out)
        ctx = (bf(cache_k[:, l].reshape(nb_l, past, CH)), bf(cache_v[:, l].reshape(nb_l, past, CH)),
               state_hgrn[:, l, 0], state_hgrn[:, l, 1])
        xs, _, _ = _mixer(xs, mod_l, l, depth, p, nb=nb_l, seq=seq_l, seg=seg_l, ctx=ctx, caches=None,
                          state_out=None)
        nf = norm_final if l == depth - 1 else None
        i = l // 2
        if l % 2 == 0:
            x = _ffn_dense(x, norm_ffn[l], mod_c, p["w1_d"][i], p["w3_d"][i], p["w2_d"][i], nf,
                           rows_per_seg=seg_c)
            xs = _ffn_dense(xs, norm_ffn[l], mod_l, p["w1_d"][i], p["w3_d"][i], p["w2_d"][i], nf,
                            rows_per_seg=seg_l)
        else:
            x, xs = _moe([(x, mod_c, seg_c), (xs, mod_l, seg_l)], norm_ffn[l], w_router[i],
                         p["w1_e"][i], p["w3_e"][i], p["w2_e"][i], nf)
    y_prompt = x.reshape(nb_c, seq_c, d)
    y_sample = xs.reshape(nb_l, seq_l, d)

    new_cache_k = caches[0].reshape(nb_c, depth, seq_c, NA_HEADS, NA_HD)
    new_cache_v = caches[1].reshape(nb_c, depth, seq_c, NA_HEADS, NA_HD)
    return (y_prompt, y_sample, new_cache_k, new_cache_v, state_out)
```
